```python
import math
import jax
import jax.numpy as jnp
from jax import lax
import numpy as np

D_MODEL = 1024
BATCH = 2
SEQ = 8192
DEPTH = 1
DEC_BATCH = 32
DEC_SEQ = 8
PAST_LEN = 16384
PAGE_SIZE = 128

MIX_WIDTH = D_MODEL
DA_WIDTH = MIX_WIDTH // 2
DA_HEAD_DIM = 64
DA_HEADS = DA_WIDTH // (2 * DA_HEAD_DIM)
HG_WIDTH = MIX_WIDTH - DA_WIDTH
HG_EXPAND = 128
HG_HEADS = HG_WIDTH // HG_EXPAND
HG_DV = HG_WIDTH // HG_HEADS
HG_CHUNK = 64
Q_BLOCK = 128
N_GROUPS = 4
EXPERTS_PER_GROUP = 4
N_EXPERTS = N_GROUPS * EXPERTS_PER_GROUP
TOP_K_INNER = 2
EXPERT_FF = D_MODEL // 4
IN_COLS = 3 * DA_WIDTH + 4 * HG_WIDTH
EPS = 1e-6
NEG_INF = -1e30

kernel_name = "hymba_diffattn_hgrn2_hiermoe_step"


def _rms(x, w):
    xf = x.astype(jnp.float32)
    y = xf * lax.rsqrt(jnp.mean(xf * xf, axis=-1, keepdims=True) + EPS)
    return y.astype(x.dtype) * w


def _split_proj(xn, w_in):
    z = jnp.einsum("btd,dc->btc", xn, w_in)
    bounds = np.cumsum([0, DA_WIDTH, DA_WIDTH, DA_WIDTH, HG_WIDTH, HG_WIDTH, HG_WIDTH, HG_WIDTH])
    return [z[..., int(bounds[i]):int(bounds[i + 1])] for i in range(7)]


def _da_qkv(zq, zk, zv, qn_w, kn_w):
    B, T, _ = zq.shape
    q = _rms(zq.reshape(B, T, DA_HEADS, 2, DA_HEAD_DIM), qn_w) * (DA_HEAD_DIM ** -0.5)
    k = _rms(zk.reshape(B, T, DA_HEADS, 2, DA_HEAD_DIM), kn_w)
    v = zv.reshape(B, T, DA_HEADS, 2 * DA_HEAD_DIM)
    return q, k, v


def _da_lambda(lq1, lk1, lq2, lk2, layer_idx):
    lam_init = 0.8 - 0.6 * math.exp(-0.3 * layer_idx)
    f32 = jnp.float32
    lam = (jnp.exp(jnp.sum(lq1.astype(f32) * lk1.astype(f32)))
           - jnp.exp(jnp.sum(lq2.astype(f32) * lk2.astype(f32))) + lam_init)
    return lam, lam_init


def _diff_map(s, lam):
    p = jax.nn.softmax(s, axis=-1)
    return p[:, :, 0] - lam * p[:, :, 1]


def _da_prompt(q, k, v, lam):
    B, T = q.shape[:2]
    qb = math.gcd(T, Q_BLOCK)
    nb = T // qb
    q_blocks = q.reshape(B, nb, qb, DA_HEADS, 2, DA_HEAD_DIM).swapaxes(0, 1)
    kpos = jnp.arange(T)

    def one(args):
        i, qblk = args
        qpos = i * qb + jnp.arange(qb)
        s = jnp.einsum("bqhmd,bshmd->bhmqs", qblk, k).astype(jnp.float32)
        s = jnp.where(kpos[None, :] <= qpos[:, None], s, NEG_INF)
        a = _diff_map(s, lam).astype(v.dtype)
        return jnp.einsum("bhqs,bshe->bqhe", a, v)

    out = lax.map(one, (jnp.arange(nb), q_blocks))
    return out.swapaxes(0, 1).reshape(B, T, DA_HEADS, 2 * DA_HEAD_DIM)


def _da_decode(q, k_new, v_new, past_k, past_v, lam):
    T = q.shape[1]
    P = past_k.shape[1]
    s_past = jnp.einsum("bqhmd,bshmd->bhmqs", q, past_k).astype(jnp.float32)
    s_new = jnp.einsum("bqhmd,bshmd->bhmqs", q, k_new).astype(jnp.float32)
    causal = jnp.tril(jnp.ones((T, T), dtype=bool))
    s_new = jnp.where(causal, s_new, NEG_INF)
    a = _diff_map(jnp.concatenate([s_past, s_new], axis=-1), lam).astype(v_new.dtype)
    return (jnp.einsum("bhqs,bshe->bqhe", a[..., :P], past_v)
            + jnp.einsum("bhqs,bshe->bqhe", a[..., P:], v_new))


def _hgrn2_scan(q, k, v, logf, S0):
    B, T, H, DK = q.shape
    C = math.gcd(T, HG_CHUNK)
    n = T // C

    def chunks(a):
        return a.reshape(B, n, C, *a.shape[2:]).swapaxes(0, 1)

    causal = jnp.tril(jnp.ones((C, C), dtype=bool))[None, :, :, None, None]

    def step(S, inp):
        qc, kc, vc, gc = inp
        b = jnp.cumsum(gc, axis=1)
        decay = jnp.exp(jnp.where(causal, b[:, :, None] - b[:, None, :], -jnp.inf))
        A = jnp.einsum("bthk,bshk,btshk->bhts", qc, kc, decay)
        o = (jnp.einsum("bhts,bshv->bthv", A, vc)
             + jnp.einsum("bthk,bhkv->bthv", qc * jnp.exp(b), S))
        b_end = b[:, -1]
        S_new = (jnp.exp(b_end)[..., None] * S
                 + jnp.einsum("bshk,bshv->bhkv", kc * jnp.exp(b_end[:, None] - b), vc))
        return S_new, o

    S_fin, o = lax.scan(step, S0, (chunks(q), chunks(k), chunks(v), chunks(logf)))
    return S_fin, o.swapaxes(0, 1).reshape(B, T, H, v.shape[-1])


def _hgrn2(zq, zf, zi, zg, lb, S0, norm_w):
    B, T, _ = zq.shape
    shp = (B, T, HG_HEADS, HG_EXPAND)
    f = lb + (1.0 - lb) * jax.nn.sigmoid(zf.astype(jnp.float32))
    q = jax.nn.silu(zq.astype(jnp.float32)).reshape(shp)
    kk = (1.0 - f).reshape(shp)
    logf = jnp.log(f).reshape(shp)
    v = zi.astype(jnp.float32).reshape(B, T, HG_HEADS, HG_DV)
    S_fin, o = _hgrn2_scan(q, kk, v, logf, S0.astype(jnp.float32))
    o = _rms(o.astype(zq.dtype), norm_w) * jax.nn.silu(zg).reshape(B, T, HG_HEADS, HG_DV)
    return o.reshape(B, T, HG_WIDTH), S_fin.astype(S0.dtype)


def _hier_moe(xn, w_rg, b_rg, w_re, b_re, w_gate, w_up, w_down):
    B, T, _ = xn.shape
    g_logit = (jnp.einsum("btd,dg->btg", xn, w_rg) + b_rg).astype(jnp.float32)
    g_idx = jnp.argmax(g_logit, axis=-1)
    g_w = jnp.take_along_axis(jax.nn.softmax(g_logit, axis=-1), g_idx[..., None], axis=-1)
    e_logit = (jnp.einsum("btd,de->bte", xn, w_re) + b_re).astype(jnp.float32)
    e_logit = e_logit.reshape(B, T, N_GROUPS, EXPERTS_PER_GROUP)
    e_sel = jnp.take_along_axis(e_logit, g_idx[..., None, None], axis=2)[:, :, 0]
    top_v, top_i = lax.top_k(e_sel, TOP_K_INNER)
    w_pair = jax.nn.softmax(top_v, axis=-1) * g_w
    expert_id = g_idx[..., None] * EXPERTS_PER_GROUP + top_i
    gates = jnp.sum(jax.nn.one_hot(expert_id, N_EXPERTS, dtype=jnp.float32) * w_pair[..., None], axis=-2)
    h = jax.nn.silu(jnp.einsum("btd,edf->btef", xn, w_gate)) * jnp.einsum("btd,edf->btef", xn, w_up)
    h = h * gates[..., None].astype(h.dtype)
    return jnp.einsum("btef,efd->btd", h, w_down)


def _layer(x, S0, past_k, past_v, layer_idx, norm1_w, w_in, qn_w, kn_w, lq1, lk1, lq2, lk2,
           subln_w, lb, hg_norm_w, w_out, norm2_w, w_rg, b_rg, w_re, b_re, w_gate, w_up, w_down):
    B, T, _ = x.shape
    xn = _rms(x, norm1_w)
    zq, zk, zv, hq, hf, hi, hg = _split_proj(xn, w_in)
    q, k, v = _da_qkv(zq, zk, zv, qn_w, kn_w)
    lam, lam_init = _da_lambda(lq1, lk1, lq2, lk2, layer_idx)
    if past_k is None:
        o_da = _da_prompt(q, k, v, lam)
    else:
        o_da = _da_decode(q, k, v, past_k, past_v, lam)
    o_da = (_rms(o_da, subln_w) * (1.0 - lam_init)).reshape(B, T, DA_WIDTH)
    o_hg, S_new = _hgrn2(hq, hf, hi, hg, lb, S0, hg_norm_w)
    h = x + jnp.einsum("btc,cd->btd", jnp.concatenate([o_da, o_hg], axis=-1), w_out)
    y = h + _hier_moe(_rms(h, norm2_w), w_rg, b_rg, w_re, b_re, w_gate, w_up, w_down)
    return y, k.reshape(B, T, DA_HEADS, 2 * DA_HEAD_DIM), v, S_new


def setup_inputs(seed: int = 0) -> dict:
    key = jax.random.key(seed)
    ks = jax.random.split(key, 26)
    f32 = jnp.float32

    def nrm(k, shape, scale):
        return jax.random.normal(k, shape, f32) * scale

    n_pages = PAST_LEN // PAGE_SIZE
    n_phys = (5 * DEC_BATCH * n_pages) // 4
    page_table = jax.random.permutation(ks[5], n_phys)[: DEC_BATCH * n_pages]
    page_table = page_table.reshape(DEC_BATCH, n_pages).astype(jnp.int32)
    return {
        "x_prompt": nrm(ks[0], (BATCH, SEQ, D_MODEL), 1.0),
        "x_sample": nrm(ks[1], (DEC_BATCH, DEC_SEQ, D_MODEL), 1.0),
        "cache_k": nrm(ks[2], (DEPTH, n_phys, PAGE_SIZE, DA_HEADS, 2 * DA_HEAD_DIM), 1.0),
        "cache_v": nrm(ks[3], (DEPTH, n_phys, PAGE_SIZE, DA_HEADS, 2 * DA_HEAD_DIM), 1.0),
        "state_hgrn": nrm(ks[4], (DEPTH, DEC_BATCH, HG_HEADS, HG_EXPAND, HG_DV), 0.3),
        "page_table": page_table,
        "norm1_w": 1.0 + nrm(ks[6], (DEPTH, D_MODEL), 0.02),
        "w_in": nrm(ks[7], (DEPTH, D_MODEL, IN_COLS), D_MODEL ** -0.5),
        "q_norm_w": 1.0 + nrm(ks[8], (DEPTH, DA_HEAD_DIM), 0.02),
        "k_norm_w": 1.0 + nrm(ks[9], (DEPTH, DA_HEAD_DIM), 0.02),
        "lambda_q1": nrm(ks[10], (DEPTH, DA_HEAD_DIM), 0.1),
        "lambda_k1": nrm(ks[11], (DEPTH, DA_HEAD_DIM), 0.1),
        "lambda_q2": nrm(ks[12], (DEPTH, DA_HEAD_DIM), 0.1),
        "lambda_k2": nrm(ks[13], (DEPTH, DA_HEAD_DIM), 0.1),
        "subln_w": 1.0 + nrm(ks[14], (DEPTH, 2 * DA_HEAD_DIM), 0.02),
        "hg_lb": nrm(ks[15], (DEPTH + 1, HG_WIDTH), 0.5),
        "hg_norm_w": 1.0 + nrm(ks[16], (DEPTH, HG_DV), 0.02),
        "w_out": nrm(ks[17], (DEPTH, MIX_WIDTH, D_MODEL), MIX_WIDTH ** -0.5),
        "norm2_w": 1.0 + nrm(ks[18], (DEPTH, D_MODEL), 0.02),
        "w_router_grp": nrm(ks[19], (DEPTH, D_MODEL, N_GROUPS), D_MODEL ** -0.5),
        "b_router_grp": nrm(ks[20], (DEPTH, N_GROUPS), 0.01),
        "w_router_exp": nrm(ks[21], (DEPTH, D_MODEL, N_EXPERTS), D_MODEL ** -0.5),
        "b_router_exp": nrm(ks[22], (DEPTH, N_EXPERTS), 0.01),
        "w_gate": nrm(ks[23], (DEPTH, N_EXPERTS, D_MODEL, EXPERT_FF), D_MODEL ** -0.5),
        "w_up": nrm(ks[24], (DEPTH, N_EXPERTS, D_MODEL, EXPERT_FF), D_MODEL ** -0.5),
        "w_down": nrm(ks[25], (DEPTH, N_EXPERTS, EXPERT_FF, D_MODEL), EXPERT_FF ** -0.5),
    }


def reference(x_prompt, x_sample, cache_k, cache_v, state_hgrn, page_table, norm1_w, w_in,
              q_norm_w, k_norm_w, lambda_q1, lambda_k1, lambda_q2, lambda_k2, subln_w, hg_lb,
              hg_norm_w, w_out, norm2_w, w_router_grp, b_router_grp, w_router_exp, b_router_exp,
              w_gate, w_up, w_down):
    dec_batch, n_pages = page_table.shape
    past_len = n_pages * cache_k.shape[2]
    lb_all = jnp.cumsum(jax.nn.softmax(hg_lb.astype(jnp.float32), axis=0), axis=0)
    yp, ys = x_prompt, x_sample
    kp, vp, sp, kss, vss, sss = [], [], [], [], [], []
    for l in range(DEPTH):
        w = (norm1_w[l], w_in[l], q_norm_w[l], k_norm_w[l], lambda_q1[l], lambda_k1[l],
             lambda_q2[l], lambda_k2[l], subln_w[l], lb_all[l], hg_norm_w[l], w_out[l], norm2_w[l],
             w_router_grp[l], b_router_grp[l], w_router_exp[l], b_router_exp[l],
             w_gate[l], w_up[l], w_down[l])
        S0p = jnp.zeros((x_prompt.shape[0], HG_HEADS, HG_EXPAND, HG_DV), x_prompt.dtype)
        yp, k_r, v_r, s_r = _layer(yp, S0p, None, None, l, *w)
        past_k = cache_k[l][page_table].reshape(dec_batch, past_len, DA_HEADS, 2, DA_HEAD_DIM)
        past_v = cache_v[l][page_table].reshape(dec_batch, past_len, DA_HEADS, 2 * DA_HEAD_DIM)
        ys, k_s, v_s, s_s = _layer(ys, state_hgrn[l], past_k, past_v, l, *w)
        kp.append(k_r); vp.append(v_r); sp.append(s_r)
        kss.append(k_s); vss.append(v_s); sss.append(s_s)
    y_prompt = yp
    y_sample = ys
    k_prompt = jnp.stack(kp)
    v_prompt = jnp.stack(vp)
    hgrn_prompt = jnp.stack(sp)
    k_sample = jnp.stack(kss)
    v_sample = jnp.stack(vss)
    hgrn_sample = jnp.stack(sss)
    return (y_prompt, y_sample, k_prompt, v_prompt, hgrn_prompt, k_sample, v_sample, hgrn_sample)
```

```python
import functools
import math

import jax
import jax.numpy as jnp
from jax import lax
from jax.experimental import pallas as pl
from jax.experimental.pallas import tpu as pltpu

F32 = jnp.float32
BF16 = jnp.bfloat16

DA_HEAD_DIM = 64
DA_HEAD_W = 2 * DA_HEAD_DIM
HG_DK = 128
HG_DV = 128
HG_CHUNK = 128
N_GROUPS = 4
EXPERTS_PER_GROUP = 4
N_EXPERTS = N_GROUPS * EXPERTS_PER_GROUP
EPS = 1e-6
NEG_INF = -1e30
LANES = 128
VMEM_LIMIT = 56 * 1024 * 1024

_NT = (((1,), (1,)), ((), ()))


def _row_tile(n, pref):
    for t in range(min(pref, n), 7, -1):
        if n % t == 0 and t % 8 == 0:
            return t
    return n


def _sigmoid(x):
    return 1.0 / (1.0 + jnp.exp(-x))


def _params(*sem):
    return pltpu.CompilerParams(dimension_semantics=sem, vmem_limit_bytes=VMEM_LIMIT)


def _inproj_kernel(x_ref, n1_ref, w_ref, qw_ref, kw_ref, lb_ref, gm_ref,
                   q_ref, k_ref, v_ref, hq_ref, f_ref, hi_ref, hg_ref, *, da_w, hg_w):
    x = x_ref[...]
    xn = x * lax.rsqrt(jnp.mean(x * x, axis=-1, keepdims=True) + EPS) * n1_ref[...]
    xb = xn.astype(BF16)

    def proj(lo, width):
        return jnp.dot(xb, w_ref[:, lo:lo + width], preferred_element_type=F32)

    gm = gm_ref[...]

    def group_rms(t, w):
        sq = t * t
        hi = sq.astype(BF16)
        lo = (sq - hi.astype(F32)).astype(BF16)
        ms = (jnp.dot(hi, gm, preferred_element_type=F32)
              + jnp.dot(lo, gm, preferred_element_type=F32))
        return t * lax.rsqrt(ms + EPS) * w

    q_ref[...] = (group_rms(proj(0, da_w), qw_ref[...]) * (DA_HEAD_DIM ** -0.5)).astype(q_ref.dtype)
    k_ref[...] = group_rms(proj(da_w, da_w), kw_ref[...])
    v_ref[...] = proj(2 * da_w, da_w)
    base = 3 * da_w
    hq = proj(base, hg_w)
    hq_ref[...] = hq * _sigmoid(hq)
    lb = lb_ref[...]
    f_ref[...] = lb + (1.0 - lb) * _sigmoid(proj(base + hg_w, hg_w))
    hi_ref[...] = proj(base + 2 * hg_w, hg_w)
    hg = proj(base + 3 * hg_w, hg_w)
    hg_ref[...] = hg * _sigmoid(hg)


def _inproj(x2d, n1, w_in_bf, qw, kw, lb, gm, q_dtype):
    n, d = x2d.shape
    da_w = qw.shape[1]
    hg_w = lb.shape[1]
    tm = _row_tile(n, 512)
    row = lambda w: pl.BlockSpec((tm, w), lambda i: (i, 0))
    full = lambda a: pl.BlockSpec(a.shape, lambda i: (0, 0))
    out_shapes = [jax.ShapeDtypeStruct((n, da_w), q_dtype)]
    out_shapes += [jax.ShapeDtypeStruct((n, da_w), F32)] * 2
    out_shapes += [jax.ShapeDtypeStruct((n, hg_w), F32)] * 4
    return pl.pallas_call(
        functools.partial(_inproj_kernel, da_w=da_w, hg_w=hg_w),
        grid=(n // tm,),
        in_specs=[row(d), full(n1), full(w_in_bf), full(qw), full(kw), full(lb), full(gm)],
        out_specs=[row(da_w)] * 3 + [row(hg_w)] * 4,
        out_shape=out_shapes,
        compiler_params=_params("parallel"),
    )(x2d, n1, w_in_bf, qw, kw, lb, gm)


def _lambda(lq1_ref, lk1_ref, lq2_ref, lk2_ref, lam_init):
    s1 = jnp.sum(lq1_ref[...] * lk1_ref[...], axis=-1, keepdims=True)
    s2 = jnp.sum(lq2_ref[...] * lk2_ref[...], axis=-1, keepdims=True)
    return jnp.exp(s1) - jnp.exp(s2) + lam_init


def _softmax_update(s, m, l, acc, vb):
    m_new = jnp.maximum(m, jnp.max(s, axis=-1, keepdims=True))
    p = jnp.exp(s - m_new)
    alpha = jnp.exp(m - m_new)
    l_new = alpha * l + jnp.sum(p, axis=-1, keepdims=True)
    acc_new = alpha * acc + jnp.dot(p.astype(BF16), vb, preferred_element_type=F32)
    return m_new, l_new, acc_new


def _sub_ln(o, sw, lam_init):
    y = o * lax.rsqrt(jnp.mean(o * o, axis=-1, keepdims=True) + EPS)
    return y * sw * (1.0 - lam_init)


def _attn_prompt_kernel(q_ref, k_ref, v_ref, lq1_ref, lk1_ref, lq2_ref, lk2_ref, sw_ref,
                        o_ref, *, blk, lam_init):
    qi = pl.program_id(2)
    q = q_ref[...]
    lane = lax.broadcasted_iota(jnp.int32, q.shape, 1)
    zero = jnp.zeros_like(q)
    qa = jnp.where(lane < DA_HEAD_DIM, q, zero)
    qb = jnp.where(lane >= DA_HEAD_DIM, q, zero)

    def chunk(start, carry, diagonal):
        m1, l1, a1, m2, l2, a2 = carry
        kb = k_ref[pl.ds(start, blk), :].astype(BF16)
        vb = v_ref[pl.ds(start, blk), :].astype(BF16)
        s1 = lax.dot_general(qa, kb, _NT, preferred_element_type=F32)
        s2 = lax.dot_general(qb, kb, _NT, preferred_element_type=F32)
        if diagonal:
            r = lax.broadcasted_iota(jnp.int32, s1.shape, 0)
            c = lax.broadcasted_iota(jnp.int32, s1.shape, 1)
            s1 = jnp.where(c <= r, s1, NEG_INF)
            s2 = jnp.where(c <= r, s2, NEG_INF)
        m1, l1, a1 = _softmax_update(s1, m1, l1, a1, vb)
        m2, l2, a2 = _softmax_update(s2, m2, l2, a2, vb)
        return m1, l1, a1, m2, l2, a2

    m0 = jnp.full((blk, 1), NEG_INF, F32)
    l0 = jnp.zeros((blk, 1), F32)
    a0 = jnp.zeros((blk, DA_HEAD_W), F32)
    carry = chunk(pl.multiple_of(qi * blk, blk), (m0, l0, a0, m0, l0, a0), True)
    carry = lax.fori_loop(
        0, qi, lambda j, c: chunk(pl.multiple_of(j * blk, blk), c, False), carry)
    m1, l1, a1, m2, l2, a2 = carry
    lam = _lambda(lq1_ref, lk1_ref, lq2_ref, lk2_ref, lam_init)
    o = a1 * (1.0 / l1) - lam * (a2 * (1.0 / l2))
    o_ref[...] = _sub_ln(o, sw_ref[...], lam_init)


def _attn_prompt(q, k, v, lq1, lk1, lq2, lk2, sw, batch, seq, lam_init):
    n, da_w = q.shape
    heads = da_w // DA_HEAD_W
    blk = _row_tile(seq, 512)
    nq = seq // blk
    small = lambda a: pl.BlockSpec(a.shape, lambda b, h, i: (0, 0))
    return pl.pallas_call(
        functools.partial(_attn_prompt_kernel, blk=blk, lam_init=lam_init),
        grid=(batch, heads, nq),
        in_specs=[
            pl.BlockSpec((blk, DA_HEAD_W), lambda b, h, i: (b * nq + i, h)),
            pl.BlockSpec((seq, DA_HEAD_W), lambda b, h, i: (b, h)),
            pl.BlockSpec((seq, DA_HEAD_W), lambda b, h, i: (b, h)),
            small(lq1), small(lk1), small(lq2), small(lk2), small(sw),
        ],
        out_specs=pl.BlockSpec((blk, DA_HEAD_W), lambda b, h, i: (b * nq + i, h)),
        out_shape=jax.ShapeDtypeStruct((n, da_w), F32),
        compiler_params=_params("parallel", "parallel", "arbitrary"),
    )(q, k, v, lq1, lk1, lq2, lk2, sw)


def _attn_decode_kernel(pt_ref, q_ref, kn_ref, vn_ref, lq1_ref, lk1_ref, lq2_ref, lk2_ref,
                        sw_ref, *rest, pages_per_step, heads, page, lam_init):
    del pt_ref
    k_refs = rest[:pages_per_step]
    v_refs = rest[pages_per_step:2 * pages_per_step]
    o_ref = rest[2 * pages_per_step]
    m_sc, l_sc, acc_sc = rest[2 * pages_per_step + 1:]
    g = pl.program_id(1)
    q = q_ref[...]
    t_new = q.shape[0]
    lane = lax.broadcasted_iota(jnp.int32, (t_new, DA_HEAD_W), 1)

    for h in range(heads):
        hs = slice(h * DA_HEAD_W, (h + 1) * DA_HEAD_W)
        qh = q[:, hs]
        q2 = jnp.concatenate([jnp.where(lane < DA_HEAD_DIM, qh, 0.0),
                              jnp.where(lane >= DA_HEAD_DIM, qh, 0.0)], axis=0).astype(BF16)

        @pl.when(g == 0)
        def _new_tokens():
            pad = jnp.zeros((LANES - t_new, DA_HEAD_W), F32)
            kn = jnp.concatenate([kn_ref[:, hs], pad], axis=0).astype(BF16)
            vn = jnp.concatenate([vn_ref[:, hs], pad], axis=0).astype(BF16)
            s = lax.dot_general(q2, kn, _NT, preferred_element_type=F32)
            r = lax.broadcasted_iota(jnp.int32, s.shape, 0)
            c = lax.broadcasted_iota(jnp.int32, s.shape, 1)
            tok = jnp.where(r >= t_new, r - t_new, r)
            s = jnp.where(c <= tok, s, NEG_INF)
            m = jnp.max(s, axis=-1, keepdims=True)
            p = jnp.exp(s - m)
            m_sc[h] = m
            l_sc[h] = jnp.sum(p, axis=-1, keepdims=True)
            acc_sc[h] = jnp.dot(p.astype(BF16), vn, preferred_element_type=F32)

        kb = jnp.concatenate([r_[pl.ds(h, page, stride=heads), :] for r_ in k_refs],
                             axis=0).astype(BF16)
        vb = jnp.concatenate([r_[pl.ds(h, page, stride=heads), :] for r_ in v_refs],
                             axis=0).astype(BF16)
        s = lax.dot_general(q2, kb, _NT, preferred_element_type=F32)
        m, l, acc = _softmax_update(s, m_sc[h], l_sc[h], acc_sc[h], vb)
        m_sc[h] = m
        l_sc[h] = l
        acc_sc[h] = acc

    @pl.when(g == pl.num_programs(1) - 1)
    def _finish():
        lam = _lambda(lq1_ref, lk1_ref, lq2_ref, lk2_ref, lam_init)
        sw = sw_ref[...]
        for h in range(heads):
            on = acc_sc[h] * (1.0 / l_sc[h])
            o = on[:t_new] - lam * on[t_new:]
            o_ref[:, h * DA_HEAD_W:(h + 1) * DA_HEAD_W] = _sub_ln(o, sw, lam_init)


def _attn_decode(q3, kn3, vn3, cache_k3, cache_v3, page_table, lq1, lk1, lq2, lk2, sw,
                 heads, lam_init):
    nb, t_new, da_w = q3.shape
    n_pages = page_table.shape[1]
    page = cache_k3.shape[1] // heads
    pps = math.gcd(n_pages, 8)
    n_steps = n_pages // pps
    tok = pl.BlockSpec((None, t_new, da_w), lambda b, g, pt: (b, 0, 0))
    small = lambda a: pl.BlockSpec(a.shape, lambda b, g, pt: (0, 0))

    def page_spec(j):
        return pl.BlockSpec((None, page * heads, DA_HEAD_W),
                            lambda b, g, pt: (pt[b, g * pps + j], 0, 0))

    grid_spec = pltpu.PrefetchScalarGridSpec(
        num_scalar_prefetch=1,
        grid=(nb, n_steps),
        in_specs=[tok, tok, tok, small(lq1), small(lk1), small(lq2), small(lk2), small(sw)]
        + [page_spec(j) for j in range(pps)] * 2,
        out_specs=tok,
        scratch_shapes=[pltpu.VMEM((heads, 2 * t_new, 1), F32),
                        pltpu.VMEM((heads, 2 * t_new, 1), F32),
                        pltpu.VMEM((heads, 2 * t_new, DA_HEAD_W), F32)],
    )
    return pl.pallas_call(
        functools.partial(_attn_decode_kernel, pages_per_step=pps, heads=heads, page=page,
                          lam_init=lam_init),
        grid_spec=grid_spec,
        out_shape=jax.ShapeDtypeStruct((nb, t_new, da_w), F32),
        compiler_params=_params("parallel", "arbitrary"),
    )(page_table, q3, kn3, vn3, lq1, lk1, lq2, lk2, sw,
      *([cache_k3] * pps), *([cache_v3] * pps))


def _hgrn_kernel(q_ref, f_ref, v_ref, g_ref, s0_ref, nw_ref, o_ref, s_out_ref, st_sc,
                 *, heads, n_chunks):
    c_len = HG_CHUNK
    step = pl.program_id(1)

    @pl.when(step == 0)
    def _load_state():
        for h in range(heads):
            st_sc[h] = s0_ref[h].T

    r = lax.broadcasted_iota(jnp.int32, (c_len, c_len), 0)
    c = lax.broadcasted_iota(jnp.int32, (c_len, c_len), 1)
    tri = jnp.where(c <= r, 1.0, 0.0).astype(BF16)
    levels = [16, 32, 64, 128]
    level_masks = []
    for lv in levels:
        half = lv // 2
        same = (r // lv) == (c // lv)
        level_masks.append(same & ((r % lv) >= half) & ((c % lv) < half))
    sub = lax.broadcasted_iota(jnp.int32, (c_len // 8, 8, 1), 1)
    nw = nw_ref[...]

    def cumsum_rows(x):
        x1 = x.astype(BF16)
        r1 = x - x1.astype(F32)
        x2 = r1.astype(BF16)
        x3 = (r1 - x2.astype(F32)).astype(BF16)
        return (jnp.dot(tri, x1, preferred_element_type=F32)
                + jnp.dot(tri, x2, preferred_element_type=F32)
                + jnp.dot(tri, x3, preferred_element_type=F32))

    def one_chunk(ci, _):
        r0 = pl.multiple_of(ci * c_len, c_len)
        for h in range(heads):
            hs = slice(h * HG_DK, (h + 1) * HG_DK)
            q = q_ref[pl.ds(r0, c_len), hs]
            f = f_ref[pl.ds(r0, c_len), hs]
            v = v_ref[pl.ds(r0, c_len), hs]
            kk = 1.0 - f
            b = cumsum_rows(jnp.log(f))

            q3 = q.reshape(c_len // 8, 8, HG_DK)
            b3 = b.reshape(c_len // 8, 8, HG_DK)
            k3 = kk.reshape(c_len // 8, 8, HG_DK)
            v3 = v.reshape(c_len // 8, 8, HG_DV)
            o3 = jnp.zeros((c_len // 8, 8, HG_DV), F32)
            for s in range(8):
                dec = jnp.exp(jnp.minimum(b3 - b3[:, s:s + 1, :], 0.0))
                a = jnp.sum(q3 * k3[:, s:s + 1, :] * dec, axis=-1, keepdims=True)
                o3 = o3 + jnp.where(sub >= s, a, 0.0) * v3[:, s:s + 1, :]
            o = o3.reshape(c_len, HG_DV)

            a_mat = jnp.zeros((c_len, c_len), F32)
            for lv, msk in zip(levels, level_masks):
                bl = b.reshape(c_len // lv, lv, HG_DK)
                mid = bl[:, lv // 2 - 1:lv // 2, :]
                e = jnp.exp(-jnp.abs(bl - mid)).reshape(c_len, HG_DK)
                a_lv = lax.dot_general((q * e).astype(BF16), (kk * e).astype(BF16), _NT,
                                       preferred_element_type=F32)
                a_mat = a_mat + jnp.where(msk, a_lv, 0.0)
            o = o + jnp.dot(a_mat.astype(BF16), v.astype(BF16), preferred_element_type=F32)

            st = st_sc[h]
            o = o + lax.dot_general((q * jnp.exp(b)).astype(BF16), st.astype(BF16), _NT,
                                    preferred_element_type=F32)
            b_end = b[c_len - 1:c_len, :]
            k_end = kk * jnp.exp(b_end - b)
            st_sc[h] = st * jnp.exp(b_end) + jnp.dot(
                v.T.astype(BF16), k_end.astype(BF16), preferred_element_type=F32)

            y = o * lax.rsqrt(jnp.mean(o * o, axis=-1, keepdims=True) + EPS) * nw
            o_ref[pl.ds(r0, c_len), hs] = y * g_ref[pl.ds(r0, c_len), hs]
        return 0

    lax.fori_loop(0, n_chunks, one_chunk, 0)

    @pl.when(step == pl.num_programs(1) - 1)
    def _store_state():
        for h in range(heads):
            s_out_ref[h] = st_sc[h].T


def _hgrn(hq, f, hi, hg, s0, nw, batch, seq):
    n, hg_w = hq.shape
    heads = hg_w // HG_DK
    tb = HG_CHUNK * math.gcd(seq // HG_CHUNK, 4)
    steps = seq // tb
    row = pl.BlockSpec((tb, hg_w), lambda b, t: (b * steps + t, 0))
    st = pl.BlockSpec((None, heads, HG_DK, HG_DV), lambda b, t: (b, 0, 0, 0))
    return pl.pallas_call(
        functools.partial(_hgrn_kernel, heads=heads, n_chunks=tb // HG_CHUNK),
        grid=(batch, steps),
        in_specs=[row, row, row, row, st, pl.BlockSpec(nw.shape, lambda b, t: (0, 0))],
        out_specs=[row, st],
        out_shape=[jax.ShapeDtypeStruct((n, hg_w), F32),
                   jax.ShapeDtypeStruct(s0.shape, F32)],
        scratch_shapes=[pltpu.VMEM((heads, HG_DV, HG_DK), F32)],
        compiler_params=_params("parallel", "arbitrary"),
    )(hq, f, hi, hg, s0, nw)


def _outproj_kernel(oda_ref, ohg_ref, x_ref, wo_ref, n2_ref, wrh_ref, wrl_ref, br_ref,
                    h_ref, xn_ref, gates_ref, *, da_w):
    o = (jnp.dot(oda_ref[...].astype(BF16), wo_ref[:da_w, :], preferred_element_type=F32)
         + jnp.dot(ohg_ref[...].astype(BF16), wo_ref[da_w:, :], preferred_element_type=F32))
    h = x_ref[...] + o
    h_ref[...] = h
    xn = h * lax.rsqrt(jnp.mean(h * h, axis=-1, keepdims=True) + EPS) * n2_ref[...]
    xh = xn.astype(BF16)
    xn_ref[...] = xh
    xl = (xn - xh.astype(F32)).astype(BF16)
    wrh = wrh_ref[...]
    logits = (jnp.dot(xh, wrh, preferred_element_type=F32)
              + jnp.dot(xl, wrh, preferred_element_type=F32)
              + jnp.dot(xh, wrl_ref[...], preferred_element_type=F32)) + br_ref[...]
    lane = lax.broadcasted_iota(jnp.int32, logits.shape, 1)
    lane_f = lane.astype(F32)
    big = float(LANES)
    is_g = (lane >= N_EXPERTS) & (lane < N_EXPERTS + N_GROUPS)
    gl = jnp.where(is_g, logits, NEG_INF)
    g_max = jnp.max(gl, axis=-1, keepdims=True)
    g_idx = jnp.min(jnp.where(gl == g_max, lane_f, big), axis=-1, keepdims=True) - N_EXPERTS
    g_w = 1.0 / jnp.sum(jnp.exp(gl - g_max), axis=-1, keepdims=True)
    grp_of_lane = (lane // EXPERTS_PER_GROUP).astype(F32)
    sel = (lane < N_EXPERTS) & (grp_of_lane == g_idx)
    el = jnp.where(sel, logits, NEG_INF)
    v1 = jnp.max(el, axis=-1, keepdims=True)
    i1 = jnp.min(jnp.where(sel, jnp.where(el == v1, lane_f, big), big), axis=-1, keepdims=True)
    el2 = jnp.where(lane_f == i1, NEG_INF, el)
    v2 = jnp.max(el2, axis=-1, keepdims=True)
    sel2 = sel & (lane_f != i1)
    i2 = jnp.min(jnp.where(sel2, jnp.where(el2 == v2, lane_f, big), big), axis=-1, keepdims=True)
    t = jnp.exp(v2 - v1)
    p1 = 1.0 / (1.0 + t)
    p2 = t * p1
    gates_ref[...] = (jnp.where(lane_f == i1, p1 * g_w, 0.0)
                      + jnp.where(lane_f == i2, p2 * g_w, 0.0))


def _outproj(oda, ohg, x2d, wo_bf, n2, wr_hi, wr_lo, br):
    n, d = x2d.shape
    da_w = oda.shape[1]
    tm = _row_tile(n, 512)
    row = lambda w: pl.BlockSpec((tm, w), lambda i: (i, 0))
    full = lambda a: pl.BlockSpec(a.shape, lambda i: (0, 0))
    return pl.pallas_call(
        functools.partial(_outproj_kernel, da_w=da_w),
        grid=(n // tm,),
        in_specs=[row(da_w), row(ohg.shape[1]), row(d), full(wo_bf), full(n2), full(wr_hi),
                  full(wr_lo), full(br)],
        out_specs=[row(d), row(d), row(LANES)],
        out_shape=[jax.ShapeDtypeStruct((n, d), F32), jax.ShapeDtypeStruct((n, d), BF16),
                   jax.ShapeDtypeStruct((n, LANES), F32)],
        compiler_params=_params("parallel"),
    )(oda, ohg, x2d, wo_bf, n2, wr_hi, wr_lo, br)


def _moe_kernel(x_ref, h_ref, gates_ref, wg_ref, wu_ref, wd_ref, y_ref):
    e = pl.program_id(1)

    @pl.when(e == 0)
    def _init():
        y_ref[...] = h_ref[...]

    x = x_ref[...]
    a = jnp.dot(x, wg_ref[...], preferred_element_type=F32)
    u = jnp.dot(x, wu_ref[...], preferred_element_type=F32)
    gates = gates_ref[...]
    lane = lax.broadcasted_iota(jnp.int32, gates.shape, 1)
    gate = jnp.sum(jnp.where(lane == e, gates, 0.0), axis=-1, keepdims=True)
    hid = (a * _sigmoid(a)) * u * gate
    y_ref[...] += jnp.dot(hid.astype(BF16), wd_ref[...], preferred_element_type=F32)


def _moe(xn_bf, h, gates, wg_bf, wu_bf, wd_bf):
    n, d = h.shape
    n_exp, _, ff = wg_bf.shape
    tm = _row_tile(n, 1024)
    row = lambda w: pl.BlockSpec((tm, w), lambda i, e: (i, 0))
    return pl.pallas_call(
        _moe_kernel,
        grid=(n // tm, n_exp),
        in_specs=[row(d), row(d), row(LANES),
                  pl.BlockSpec((None, d, ff), lambda i, e: (e, 0, 0)),
                  pl.BlockSpec((None, d, ff), lambda i, e: (e, 0, 0)),
                  pl.BlockSpec((None, ff, d), lambda i, e: (e, 0, 0))],
        out_specs=row(d),
        out_shape=jax.ShapeDtypeStruct((n, d), F32),
        compiler_params=_params("parallel", "arbitrary"),
    )(xn_bf, h, gates, wg_bf, wu_bf, wd_bf)


def _pad_tokens(a, batch, seq, seq_pad, value):
    a3 = a.reshape(batch, seq, a.shape[-1])
    a3 = jnp.pad(a3, ((0, 0), (0, seq_pad - seq), (0, 0)), constant_values=value)
    return a3.reshape(batch * seq_pad, a.shape[-1])


def _layer(x, s0, past, layer_idx, w):
    batch, seq, d = x.shape
    n = batch * seq
    x2d = x.reshape(n, d)
    da_w = w["qw"].shape[1]
    heads = da_w // DA_HEAD_W
    lam_init = 0.8 - 0.6 * math.exp(-0.3 * layer_idx)
    lam_args = (w["lq1"], w["lk1"], w["lq2"], w["lk2"], w["sw"])

    q, k, v, hq, f, hi, hg = _inproj(x2d, w["n1"], w["w_in"], w["qw"], w["kw"], w["lb"], w["gm"],
                                     BF16 if past is None else F32)
    if past is None:
        oda = _attn_prompt(q, k, v, *lam_args, batch, seq, lam_init)
    else:
        cache_k3, cache_v3, page_table = past
        r3 = lambda a: a.reshape(batch, seq, da_w)
        oda = _attn_decode(r3(q), r3(k), r3(v), cache_k3, cache_v3, page_table, *lam_args,
                           heads, lam_init).reshape(n, da_w)

    seq_pad = -(-seq // HG_CHUNK) * HG_CHUNK
    if seq_pad != seq:
        hq_p, hi_p, hg_p = (_pad_tokens(a, batch, seq, seq_pad, 0.0) for a in (hq, hi, hg))
        f_p = _pad_tokens(f, batch, seq, seq_pad, 1.0)
        ohg, s_new = _hgrn(hq_p, f_p, hi_p, hg_p, s0, w["hnw"], batch, seq_pad)
        ohg = ohg.reshape(batch, seq_pad, -1)[:, :seq].reshape(n, -1)
    else:
        ohg, s_new = _hgrn(hq, f, hi, hg, s0, w["hnw"], batch, seq)

    h, xn_bf, gates = _outproj(oda, ohg, x2d, w["w_out"], w["n2"], w["wr_hi"], w["wr_lo"], w["br"])
    y = _moe(xn_bf, h, gates, w["wg"], w["wu"], w["wd"])
    kv_shape = (batch, seq, heads, DA_HEAD_W)
    return y.reshape(batch, seq, d), k.reshape(kv_shape), v.reshape(kv_shape), s_new


def _layer_weights(l, lb_all, norm1_w, w_in, q_norm_w, k_norm_w, lambda_q1, lambda_k1, lambda_q2,
                   lambda_k2, subln_w, hg_norm_w, w_out, norm2_w, w_router_grp, b_router_grp,
                   w_router_exp, b_router_exp, w_gate, w_up, w_down):
    d = w_in.shape[1]
    da_w = w_out.shape[1] // 2
    n_grp_norm = da_w // DA_HEAD_DIM
    row = lambda a: a.reshape(1, -1).astype(F32)
    g_id = jnp.arange(da_w) // DA_HEAD_DIM
    gm = jnp.where(g_id[:, None] == g_id[None, :], 1.0 / DA_HEAD_DIM, 0.0).astype(BF16)
    wr = jnp.zeros((d, LANES), F32)
    wr = wr.at[:, :N_EXPERTS].set(w_router_exp[l]).at[:, N_EXPERTS:N_EXPERTS + N_GROUPS].set(
        w_router_grp[l])
    wr_hi = wr.astype(BF16)
    wr_lo = (wr - wr_hi.astype(F32)).astype(BF16)
    br = jnp.zeros((1, LANES), F32)
    br = br.at[0, :N_EXPERTS].set(b_router_exp[l]).at[0, N_EXPERTS:N_EXPERTS + N_GROUPS].set(
        b_router_grp[l])
    return dict(
        n1=row(norm1_w[l]), w_in=w_in[l].astype(BF16),
        qw=row(jnp.tile(q_norm_w[l], n_grp_norm)), kw=row(jnp.tile(k_norm_w[l], n_grp_norm)),
        lb=row(lb_all[l]), gm=gm,
        lq1=row(lambda_q1[l]), lk1=row(lambda_k1[l]), lq2=row(lambda_q2[l]), lk2=row(lambda_k2[l]),
        sw=row(subln_w[l]), hnw=row(hg_norm_w[l]),
        w_out=w_out[l].astype(BF16), n2=row(norm2_w[l]), wr_hi=wr_hi, wr_lo=wr_lo, br=br,
        wg=w_gate[l].astype(BF16), wu=w_up[l].astype(BF16), wd=w_down[l].astype(BF16),
    )


def kernel(x_prompt, x_sample, cache_k, cache_v, state_hgrn, page_table, norm1_w, w_in, q_norm_w, k_norm_w, lambda_q1, lambda_k1, lambda_q2, lambda_k2, subln_w, hg_lb, hg_norm_w, w_out, norm2_w, w_router_grp, b_router_grp, w_router_exp, b_router_exp, w_gate, w_up, w_down):
    depth = norm1_w.shape[0]
    n_phys, page, heads, head_w = cache_k.shape[1:]
    lb_all = jnp.cumsum(jax.nn.softmax(hg_lb.astype(F32), axis=0), axis=0)
    yp, ys = x_prompt, x_sample
    outs = [[] for _ in range(6)]
    for l in range(depth):
        w = _layer_weights(l, lb_all, norm1_w, w_in, q_norm_w, k_norm_w, lambda_q1, lambda_k1,
                           lambda_q2, lambda_k2, subln_w, hg_norm_w, w_out, norm2_w,
                           w_router_grp, b_router_grp, w_router_exp, b_router_exp,
                           w_gate, w_up, w_down)
        s0p = jnp.zeros((x_prompt.shape[0],) + state_hgrn.shape[2:], F32)
        yp, k_r, v_r, s_r = _layer(yp, s0p, None, l, w)
        past = (cache_k[l].reshape(n_phys, page * heads, head_w),
                cache_v[l].reshape(n_phys, page * heads, head_w), page_table)
        ys, k_s, v_s, s_s = _layer(ys, state_hgrn[l], past, l, w)
        for lst, val in zip(outs, (k_r, v_r, s_r, k_s, v_s, s_s)):
            lst.append(val)
    kp, vp, sp, kss, vss, sss = (jnp.stack(o) for o in outs)
    return (yp, ys, kp, vp, sp, kss, vss, sss)
```

```python
import functools
import math

import jax
import jax.numpy as jnp
from jax import lax
from jax.experimental import pallas as pl
from jax.experimental.pallas import tpu as pltpu

F32 = jnp.float32
BF16 = jnp.bfloat16

DA_HEAD_DIM = 64
DA_HEAD_W = 2 * DA_HEAD_DIM
HG_DK = 128
HG_DV = 128
HG_CHUNK = 128
N_GROUPS = 4
EXPERTS_PER_GROUP = 4
N_EXPERTS = N_GROUPS * EXPERTS_PER_GROUP
EPS = 1e-6
NEG_INF = -1e30
LANES = 128
VMEM_LIMIT = 56 * 1024 * 1024

_NT = (((1,), (1,)), ((), ()))


def _row_tile(n, pref):
    for t in range(min(pref, n), 7, -1):
        if n % t == 0 and t % 8 == 0:
            return t
    return n


def _sigmoid(x):
    return 1.0 / (1.0 + jnp.exp(-x))


def _params(*sem):
    return pltpu.CompilerParams(dimension_semantics=sem, vmem_limit_bytes=VMEM_LIMIT)


def _inproj_kernel(x_ref, n1_ref, w_ref, qw_ref, kw_ref, lb_ref, gm_ref,
                   q_ref, k_ref, v_ref, hq_ref, f_ref, hi_ref, hg_ref, *attn_refs, da_w, hg_w):
    x = x_ref[...]
    xn = x * lax.rsqrt(jnp.mean(x * x, axis=-1, keepdims=True) + EPS) * n1_ref[...]
    xb = xn.astype(BF16)

    def proj(lo, width):
        return jnp.dot(xb, w_ref[:, lo:lo + width], preferred_element_type=F32)

    gm = gm_ref[...]

    def group_rms(t, w):
        sq = t * t
        hi = sq.astype(BF16)
        lo = (sq - hi.astype(F32)).astype(BF16)
        ms = (jnp.dot(hi, gm, preferred_element_type=F32)
              + jnp.dot(lo, gm, preferred_element_type=F32))
        return t * lax.rsqrt(ms + EPS) * w

    q_ref[...] = (group_rms(proj(0, da_w), qw_ref[...]) * (DA_HEAD_DIM ** -0.5)).astype(q_ref.dtype)
    k = group_rms(proj(da_w, da_w), kw_ref[...])
    v = proj(2 * da_w, da_w)
    k_ref[...] = k
    v_ref[...] = v
    if attn_refs:
        kb_ref, vt_ref = attn_refs
        kb_ref[...] = k.astype(BF16)
        vt_ref[...] = v.T.astype(BF16)
    base = 3 * da_w
    hq = proj(base, hg_w)
    hq_ref[...] = hq * _sigmoid(hq)
    lb = lb_ref[...]
    f_ref[...] = lb + (1.0 - lb) * _sigmoid(proj(base + hg_w, hg_w))
    hi_ref[...] = proj(base + 2 * hg_w, hg_w)
    hg = proj(base + 3 * hg_w, hg_w)
    hg_ref[...] = hg * _sigmoid(hg)


def _inproj(x2d, n1, w_in_bf, qw, kw, lb, gm, for_prompt):
    n, d = x2d.shape
    da_w = qw.shape[1]
    hg_w = lb.shape[1]
    tm = _row_tile(n, 512)
    row = lambda w: pl.BlockSpec((tm, w), lambda i: (i, 0))
    full = lambda a: pl.BlockSpec(a.shape, lambda i: (0, 0))
    out_shapes = [jax.ShapeDtypeStruct((n, da_w), BF16 if for_prompt else F32)]
    out_shapes += [jax.ShapeDtypeStruct((n, da_w), F32)] * 2
    out_shapes += [jax.ShapeDtypeStruct((n, hg_w), F32)] * 4
    out_specs = [row(da_w)] * 3 + [row(hg_w)] * 4
    if for_prompt:
        out_shapes += [jax.ShapeDtypeStruct((n, da_w), BF16), jax.ShapeDtypeStruct((da_w, n), BF16)]
        out_specs += [row(da_w), pl.BlockSpec((da_w, tm), lambda i: (0, i))]
    return pl.pallas_call(
        functools.partial(_inproj_kernel, da_w=da_w, hg_w=hg_w),
        grid=(n // tm,),
        in_specs=[row(d), full(n1), full(w_in_bf), full(qw), full(kw), full(lb), full(gm)],
        out_specs=out_specs,
        out_shape=out_shapes,
        compiler_params=_params("parallel"),
    )(x2d, n1, w_in_bf, qw, kw, lb, gm)


def _lambda(lq1_ref, lk1_ref, lq2_ref, lk2_ref, lam_init):
    s1 = jnp.sum(lq1_ref[...] * lk1_ref[...], axis=-1, keepdims=True)
    s2 = jnp.sum(lq2_ref[...] * lk2_ref[...], axis=-1, keepdims=True)
    return jnp.exp(s1) - jnp.exp(s2) + lam_init


def _softmax_update(s, m, l, acc, vb):
    m_new = jnp.maximum(m, jnp.max(s, axis=-1, keepdims=True))
    p = jnp.exp(s - m_new)
    alpha = jnp.exp(m - m_new)
    l_new = alpha * l + jnp.sum(p, axis=-1, keepdims=True)
    acc_new = alpha * acc + jnp.dot(p.astype(BF16), vb, preferred_element_type=F32)
    return m_new, l_new, acc_new


def _sub_ln(o, sw, lam_init):
    y = o * lax.rsqrt(jnp.mean(o * o, axis=-1, keepdims=True) + EPS)
    return y * sw * (1.0 - lam_init)


_ONES_ROWS = 16


def _attn_prompt_kernel(q_ref, k_ref, vt_ref, lq1_ref, lk1_ref, lq2_ref, lk2_ref, sw_ref,
                        o_ref, s_sc, *, blk, lam_init):
    qi = pl.program_id(2)
    q = q_ref[...]
    lane = lax.broadcasted_iota(jnp.int32, q.shape, 1)
    zero = jnp.zeros_like(q)
    qa = jnp.where(lane < DA_HEAD_DIM, q, zero)
    qb = jnp.where(lane >= DA_HEAD_DIM, q, zero)
    ones = jnp.ones((_ONES_ROWS, blk), BF16)

    def update(s, m, acc, vt1):
        m_new = jnp.maximum(m, jnp.max(s, axis=0, keepdims=True))
        p = jnp.exp(s - m_new).astype(BF16)
        acc_new = jnp.exp(m - m_new) * acc + jnp.dot(vt1, p, preferred_element_type=F32)
        return m_new, acc_new

    def scores(c):
        kb = k_ref[pl.ds(pl.multiple_of(c * blk, blk), blk), :]
        return (lax.dot_general(kb, qa, _NT, preferred_element_type=F32),
                lax.dot_general(kb, qb, _NT, preferred_element_type=F32))

    def stage_scores(c, slot):
        s_sc[slot, 0], s_sc[slot, 1] = scores(c)

    def consume(c, s1, s2, carry):
        m1, a1, m2, a2 = carry
        vt1 = jnp.concatenate([vt_ref[:, pl.ds(pl.multiple_of(c * blk, blk), blk)], ones], axis=0)
        m1, a1 = update(s1, m1, a1, vt1)
        m2, a2 = update(s2, m2, a2, vt1)
        return m1, a1, m2, a2

    s1, s2 = scores(qi)
    key = lax.broadcasted_iota(jnp.int32, s1.shape, 0)
    qry = lax.broadcasted_iota(jnp.int32, s1.shape, 1)
    m0 = jnp.full((1, blk), NEG_INF, F32)
    a0 = jnp.zeros((DA_HEAD_W + _ONES_ROWS, blk), F32)
    carry = consume(qi, jnp.where(key <= qry, s1, NEG_INF), jnp.where(key <= qry, s2, NEG_INF),
                    (m0, a0, m0, a0))

    stage_scores(0, 0)

    def chunk_pair(j, c):
        stage_scores(2 * j + 1, 1)
        c = consume(2 * j, s_sc[0, 0], s_sc[0, 1], c)
        stage_scores(2 * j + 2, 0)
        return consume(2 * j + 1, s_sc[1, 0], s_sc[1, 1], c)

    carry = lax.fori_loop(0, qi // 2, chunk_pair, carry)
    carry = lax.cond(qi % 2 == 1,
                     lambda c: consume(qi - 1, s_sc[0, 0], s_sc[0, 1], c),
                     lambda c: c, carry)
    _, a1, _, a2 = carry
    lam = _lambda(lq1_ref, lk1_ref, lq2_ref, lk2_ref, lam_init)
    o1 = a1[:DA_HEAD_W] * (1.0 / a1[DA_HEAD_W:DA_HEAD_W + 1])
    o2 = a2[:DA_HEAD_W] * (1.0 / a2[DA_HEAD_W:DA_HEAD_W + 1])
    o_ref[...] = _sub_ln((o1 - lam * o2).T, sw_ref[...], lam_init)


def _attn_prompt(q, kb, vt, lq1, lk1, lq2, lk2, sw, batch, seq, lam_init):
    n, da_w = q.shape
    heads = da_w // DA_HEAD_W
    blk = _row_tile(seq, 512)
    nq = seq // blk
    small = lambda a: pl.BlockSpec(a.shape, lambda b, h, i: (0, 0))
    return pl.pallas_call(
        functools.partial(_attn_prompt_kernel, blk=blk, lam_init=lam_init),
        grid=(batch, heads, nq),
        in_specs=[
            pl.BlockSpec((blk, DA_HEAD_W), lambda b, h, i: (b * nq + i, h)),
            pl.BlockSpec((seq, DA_HEAD_W), lambda b, h, i: (b, h)),
            pl.BlockSpec((DA_HEAD_W, seq), lambda b, h, i: (h, b)),
            small(lq1), small(lk1), small(lq2), small(lk2), small(sw),
        ],
        out_specs=pl.BlockSpec((blk, DA_HEAD_W), lambda b, h, i: (b * nq + i, h)),
        out_shape=jax.ShapeDtypeStruct((n, da_w), F32),
        scratch_shapes=[pltpu.VMEM((2, 2, blk, blk), F32)],
        compiler_params=_params("parallel", "parallel", "arbitrary"),
    )(q, kb, vt, lq1, lk1, lq2, lk2, sw)


def _attn_decode_kernel(pt_ref, q_ref, kn_ref, vn_ref, lq1_ref, lk1_ref, lq2_ref, lk2_ref,
                        sw_ref, *rest, pages_per_step, heads, page, lam_init):
    del pt_ref
    k_refs = rest[:pages_per_step]
    v_refs = rest[pages_per_step:2 * pages_per_step]
    o_ref = rest[2 * pages_per_step]
    m_sc, l_sc, acc_sc = rest[2 * pages_per_step + 1:]
    g = pl.program_id(1)
    q = q_ref[...]
    t_new = q.shape[0]
    lane = lax.broadcasted_iota(jnp.int32, (t_new, DA_HEAD_W), 1)
    head_cols = [slice(h * DA_HEAD_W, (h + 1) * DA_HEAD_W) for h in range(heads)]
    q2 = [jnp.concatenate([jnp.where(lane < DA_HEAD_DIM, q[:, hs], 0.0),
                           jnp.where(lane >= DA_HEAD_DIM, q[:, hs], 0.0)], axis=0).astype(BF16)
          for hs in head_cols]

    @pl.when(g == 0)
    def _new_tokens():
        pad = jnp.zeros((LANES - t_new, DA_HEAD_W), F32)
        for h, hs in enumerate(head_cols):
            kn = jnp.concatenate([kn_ref[:, hs], pad], axis=0).astype(BF16)
            vn = jnp.concatenate([vn_ref[:, hs], pad], axis=0).astype(BF16)
            s = lax.dot_general(q2[h], kn, _NT, preferred_element_type=F32)
            r = lax.broadcasted_iota(jnp.int32, s.shape, 0)
            c = lax.broadcasted_iota(jnp.int32, s.shape, 1)
            tok = jnp.where(r >= t_new, r - t_new, r)
            s = jnp.where(c <= tok, s, NEG_INF)
            m = jnp.max(s, axis=-1, keepdims=True)
            p = jnp.exp(s - m)
            m_sc[h] = m
            l_sc[h] = jnp.sum(p, axis=-1, keepdims=True)
            acc_sc[h] = jnp.dot(p.astype(BF16), vn, preferred_element_type=F32)

    for h in range(heads):
        kb = jnp.concatenate([r_[pl.ds(h, page, stride=heads), :] for r_ in k_refs],
                             axis=0).astype(BF16)
        vb = jnp.concatenate([r_[pl.ds(h, page, stride=heads), :] for r_ in v_refs],
                             axis=0).astype(BF16)
        s = lax.dot_general(q2[h], kb, _NT, preferred_element_type=F32)
        m, l, acc = _softmax_update(s, m_sc[h], l_sc[h], acc_sc[h], vb)
        m_sc[h] = m
        l_sc[h] = l
        acc_sc[h] = acc

    @pl.when(g == pl.num_programs(1) - 1)
    def _finish():
        lam = _lambda(lq1_ref, lk1_ref, lq2_ref, lk2_ref, lam_init)
        sw = sw_ref[...]
        for h in range(heads):
            on = acc_sc[h] * (1.0 / l_sc[h])
            o = on[:t_new] - lam * on[t_new:]
            o_ref[:, h * DA_HEAD_W:(h + 1) * DA_HEAD_W] = _sub_ln(o, sw, lam_init)


def _attn_decode(q3, kn3, vn3, cache_k3, cache_v3, page_table, lq1, lk1, lq2, lk2, sw,
                 heads, lam_init):
    nb, t_new, da_w = q3.shape
    n_pages = page_table.shape[1]
    page = cache_k3.shape[1] // heads
    pps = math.gcd(n_pages, 16)
    n_steps = n_pages // pps
    tok = pl.BlockSpec((None, t_new, da_w), lambda b, g, pt: (b, 0, 0))
    small = lambda a: pl.BlockSpec(a.shape, lambda b, g, pt: (0, 0))

    def page_spec(j):
        return pl.BlockSpec((None, page * heads, DA_HEAD_W),
                            lambda b, g, pt: (pt[b, g * pps + j], 0, 0))

    grid_spec = pltpu.PrefetchScalarGridSpec(
        num_scalar_prefetch=1,
        grid=(nb, n_steps),
        in_specs=[tok, tok, tok, small(lq1), small(lk1), small(lq2), small(lk2), small(sw)]
        + [page_spec(j) for j in range(pps)] * 2,
        out_specs=tok,
        scratch_shapes=[pltpu.VMEM((heads, 2 * t_new, 1), F32),
                        pltpu.VMEM((heads, 2 * t_new, 1), F32),
                        pltpu.VMEM((heads, 2 * t_new, DA_HEAD_W), F32)],
    )
    return pl.pallas_call(
        functools.partial(_attn_decode_kernel, pages_per_step=pps, heads=heads, page=page,
                          lam_init=lam_init),
        grid_spec=grid_spec,
        out_shape=jax.ShapeDtypeStruct((nb, t_new, da_w), F32),
        compiler_params=_params("parallel", "arbitrary"),
    )(page_table, q3, kn3, vn3, lq1, lk1, lq2, lk2, sw,
      *([cache_k3] * pps), *([cache_v3] * pps))


def _hgrn_kernel(q_ref, f_ref, v_ref, g_ref, s0_ref, nw_ref, o_ref, s_out_ref, st_sc,
                 *, heads, n_chunks):
    c_len = HG_CHUNK
    step = pl.program_id(1)

    @pl.when(step == 0)
    def _load_state():
        for h in range(heads):
            st_sc[h] = s0_ref[h].T

    r = lax.broadcasted_iota(jnp.int32, (c_len, c_len), 0)
    c = lax.broadcasted_iota(jnp.int32, (c_len, c_len), 1)
    tri = jnp.where(c <= r, 1.0, 0.0).astype(BF16)
    levels = [16, 32, 64, 128]
    level_masks = []
    for lv in levels:
        half = lv // 2
        same = (r // lv) == (c // lv)
        level_masks.append(same & ((r % lv) >= half) & ((c % lv) < half))
    sub = lax.broadcasted_iota(jnp.int32, (c_len // 8, 8, 1), 1)
    nw = nw_ref[...]

    def cumsum_rows(x):
        x1 = x.astype(BF16)
        r1 = x - x1.astype(F32)
        x2 = r1.astype(BF16)
        x3 = (r1 - x2.astype(F32)).astype(BF16)
        return (jnp.dot(tri, x1, preferred_element_type=F32)
                + jnp.dot(tri, x2, preferred_element_type=F32)
                + jnp.dot(tri, x3, preferred_element_type=F32))

    def one_chunk(ci, _):
        r0 = pl.multiple_of(ci * c_len, c_len)
        for h in range(heads):
            hs = slice(h * HG_DK, (h + 1) * HG_DK)
            q = q_ref[pl.ds(r0, c_len), hs]
            f = f_ref[pl.ds(r0, c_len), hs]
            v = v_ref[pl.ds(r0, c_len), hs]
            kk = 1.0 - f
            b = cumsum_rows(jnp.log(f))

            q3 = q.reshape(c_len // 8, 8, HG_DK)
            b3 = b.reshape(c_len // 8, 8, HG_DK)
            k3 = kk.reshape(c_len // 8, 8, HG_DK)
            v3 = v.reshape(c_len // 8, 8, HG_DV)
            o3 = jnp.zeros((c_len // 8, 8, HG_DV), F32)
            for s in range(8):
                dec = jnp.exp(jnp.minimum(b3 - b3[:, s:s + 1, :], 0.0))
                a = jnp.sum(q3 * k3[:, s:s + 1, :] * dec, axis=-1, keepdims=True)
                o3 = o3 + jnp.where(sub >= s, a, 0.0) * v3[:, s:s + 1, :]
            o = o3.reshape(c_len, HG_DV)

            a_mat = jnp.zeros((c_len, c_len), F32)
            for lv, msk in zip(levels, level_masks):
                bl = b.reshape(c_len // lv, lv, HG_DK)
                mid = bl[:, lv // 2 - 1:lv // 2, :]
                e = jnp.exp(-jnp.abs(bl - mid)).reshape(c_len, HG_DK)
                a_lv = lax.dot_general((q * e).astype(BF16), (kk * e).astype(BF16), _NT,
                                       preferred_element_type=F32)
                a_mat = a_mat + jnp.where(msk, a_lv, 0.0)
            o = o + jnp.dot(a_mat.astype(BF16), v.astype(BF16), preferred_element_type=F32)

            st = st_sc[h]
            o = o + lax.dot_general((q * jnp.exp(b)).astype(BF16), st.astype(BF16), _NT,
                                    preferred_element_type=F32)
            b_end = b[c_len - 1:c_len, :]
            k_end = kk * jnp.exp(b_end - b)
            st_sc[h] = st * jnp.exp(b_end) + jnp.dot(
                v.T.astype(BF16), k_end.astype(BF16), preferred_element_type=F32)

            y = o * lax.rsqrt(jnp.mean(o * o, axis=-1, keepdims=True) + EPS) * nw
            o_ref[pl.ds(r0, c_len), hs] = y * g_ref[pl.ds(r0, c_len), hs]
        return 0

    lax.fori_loop(0, n_chunks, one_chunk, 0)

    @pl.when(step == pl.num_programs(1) - 1)
    def _store_state():
        for h in range(heads):
            s_out_ref[h] = st_sc[h].T


def _hgrn(hq, f, hi, hg, s0, nw, batch, seq):
    n, hg_w = hq.shape
    heads = hg_w // HG_DK
    tb = HG_CHUNK * math.gcd(seq // HG_CHUNK, 4)
    steps = seq // tb
    row = pl.BlockSpec((tb, hg_w), lambda b, t: (b * steps + t, 0))
    st = pl.BlockSpec((None, heads, HG_DK, HG_DV), lambda b, t: (b, 0, 0, 0))
    return pl.pallas_call(
        functools.partial(_hgrn_kernel, heads=heads, n_chunks=tb // HG_CHUNK),
        grid=(batch, steps),
        in_specs=[row, row, row, row, st, pl.BlockSpec(nw.shape, lambda b, t: (0, 0))],
        out_specs=[row, st],
        out_shape=[jax.ShapeDtypeStruct((n, hg_w), F32),
                   jax.ShapeDtypeStruct(s0.shape, F32)],
        scratch_shapes=[pltpu.VMEM((heads, HG_DV, HG_DK), F32)],
        compiler_params=_params("parallel", "arbitrary"),
    )(hq, f, hi, hg, s0, nw)


def _outproj_kernel(oda_ref, ohg_ref, x_ref, wo_ref, n2_ref, wrh_ref, wrl_ref, br_ref,
                    h_ref, xn_ref, gates_ref, *, da_w):
    o = (jnp.dot(oda_ref[...].astype(BF16), wo_ref[:da_w, :], preferred_element_type=F32)
         + jnp.dot(ohg_ref[...].astype(BF16), wo_ref[da_w:, :], preferred_element_type=F32))
    h = x_ref[...] + o
    h_ref[...] = h
    xn = h * lax.rsqrt(jnp.mean(h * h, axis=-1, keepdims=True) + EPS) * n2_ref[...]
    xh = xn.astype(BF16)
    xn_ref[...] = xh
    xl = (xn - xh.astype(F32)).astype(BF16)
    wrh = wrh_ref[...]
    logits = (jnp.dot(xh, wrh, preferred_element_type=F32)
              + jnp.dot(xl, wrh, preferred_element_type=F32)
              + jnp.dot(xh, wrl_ref[...], preferred_element_type=F32)) + br_ref[...]
    lane = lax.broadcasted_iota(jnp.int32, logits.shape, 1)
    lane_f = lane.astype(F32)
    big = float(LANES)
    is_g = (lane >= N_EXPERTS) & (lane < N_EXPERTS + N_GROUPS)
    gl = jnp.where(is_g, logits, NEG_INF)
    g_max = jnp.max(gl, axis=-1, keepdims=True)
    g_idx = jnp.min(jnp.where(gl == g_max, lane_f, big), axis=-1, keepdims=True) - N_EXPERTS
    g_w = 1.0 / jnp.sum(jnp.exp(gl - g_max), axis=-1, keepdims=True)
    grp_of_lane = (lane // EXPERTS_PER_GROUP).astype(F32)
    sel = (lane < N_EXPERTS) & (grp_of_lane == g_idx)
    el = jnp.where(sel, logits, NEG_INF)
    v1 = jnp.max(el, axis=-1, keepdims=True)
    i1 = jnp.min(jnp.where(sel, jnp.where(el == v1, lane_f, big), big), axis=-1, keepdims=True)
    el2 = jnp.where(lane_f == i1, NEG_INF, el)
    v2 = jnp.max(el2, axis=-1, keepdims=True)
    sel2 = sel & (lane_f != i1)
    i2 = jnp.min(jnp.where(sel2, jnp.where(el2 == v2, lane_f, big), big), axis=-1, keepdims=True)
    t = jnp.exp(v2 - v1)
    p1 = 1.0 / (1.0 + t)
    p2 = t * p1
    gates_ref[...] = (jnp.where(lane_f == i1, p1 * g_w, 0.0)
                      + jnp.where(lane_f == i2, p2 * g_w, 0.0))


def _outproj(oda, ohg, x2d, wo_bf, n2, wr_hi, wr_lo, br):
    n, d = x2d.shape
    da_w = oda.shape[1]
    tm = _row_tile(n, 512)
    row = lambda w: pl.BlockSpec((tm, w), lambda i: (i, 0))
    full = lambda a: pl.BlockSpec(a.shape, lambda i: (0, 0))
    return pl.pallas_call(
        functools.partial(_outproj_kernel, da_w=da_w),
        grid=(n // tm,),
        in_specs=[row(da_w), row(ohg.shape[1]), row(d), full(wo_bf), full(n2), full(wr_hi),
                  full(wr_lo), full(br)],
        out_specs=[row(d), row(d), row(LANES)],
        out_shape=[jax.ShapeDtypeStruct((n, d), F32), jax.ShapeDtypeStruct((n, d), BF16),
                   jax.ShapeDtypeStruct((n, LANES), F32)],
        compiler_params=_params("parallel"),
    )(oda, ohg, x2d, wo_bf, n2, wr_hi, wr_lo, br)


def _moe_kernel(x_ref, h_ref, gates_ref, wg_ref, wu_ref, wd_ref, y_ref):
    e = pl.program_id(1)

    @pl.when(e == 0)
    def _init():
        y_ref[...] = h_ref[...]

    x = x_ref[...]
    a = jnp.dot(x, wg_ref[...], preferred_element_type=F32)
    u = jnp.dot(x, wu_ref[...], preferred_element_type=F32)
    gates = gates_ref[...]
    lane = lax.broadcasted_iota(jnp.int32, gates.shape, 1)
    gate = jnp.sum(jnp.where(lane == e, gates, 0.0), axis=-1, keepdims=True)
    hid = (a * _sigmoid(a)) * u * gate
    y_ref[...] += jnp.dot(hid.astype(BF16), wd_ref[...], preferred_element_type=F32)


def _moe(xn_bf, h, gates, wg_bf, wu_bf, wd_bf):
    n, d = h.shape
    n_exp, _, ff = wg_bf.shape
    tm = _row_tile(n, 1024)
    row = lambda w: pl.BlockSpec((tm, w), lambda i, e: (i, 0))
    return pl.pallas_call(
        _moe_kernel,
        grid=(n // tm, n_exp),
        in_specs=[row(d), row(d), row(LANES),
                  pl.BlockSpec((None, d, ff), lambda i, e: (e, 0, 0)),
                  pl.BlockSpec((None, d, ff), lambda i, e: (e, 0, 0)),
                  pl.BlockSpec((None, ff, d), lambda i, e: (e, 0, 0))],
        out_specs=row(d),
        out_shape=jax.ShapeDtypeStruct((n, d), F32),
        compiler_params=_params("parallel", "arbitrary"),
    )(xn_bf, h, gates, wg_bf, wu_bf, wd_bf)


def _pad_tokens(a, batch, seq, seq_pad, value):
    a3 = a.reshape(batch, seq, a.shape[-1])
    a3 = jnp.pad(a3, ((0, 0), (0, seq_pad - seq), (0, 0)), constant_values=value)
    return a3.reshape(batch * seq_pad, a.shape[-1])


def _layer(x, s0, past, layer_idx, w):
    batch, seq, d = x.shape
    n = batch * seq
    x2d = x.reshape(n, d)
    da_w = w["qw"].shape[1]
    heads = da_w // DA_HEAD_W
    lam_init = 0.8 - 0.6 * math.exp(-0.3 * layer_idx)
    lam_args = (w["lq1"], w["lk1"], w["lq2"], w["lk2"], w["sw"])

    q, k, v, hq, f, hi, hg, *attn_in = _inproj(x2d, w["n1"], w["w_in"], w["qw"], w["kw"], w["lb"],
                                               w["gm"], past is None)
    if past is None:
        oda = _attn_prompt(q, *attn_in, *lam_args, batch, seq, lam_init)
    else:
        cache_k3, cache_v3, page_table = past
        r3 = lambda a: a.reshape(batch, seq, da_w)
        oda = _attn_decode(r3(q), r3(k), r3(v), cache_k3, cache_v3, page_table, *lam_args,
                           heads, lam_init).reshape(n, da_w)

    seq_pad = -(-seq // HG_CHUNK) * HG_CHUNK
    if seq_pad != seq:
        hq_p, hi_p, hg_p = (_pad_tokens(a, batch, seq, seq_pad, 0.0) for a in (hq, hi, hg))
        f_p = _pad_tokens(f, batch, seq, seq_pad, 1.0)
        ohg, s_new = _hgrn(hq_p, f_p, hi_p, hg_p, s0, w["hnw"], batch, seq_pad)
        ohg = ohg.reshape(batch, seq_pad, -1)[:, :seq].reshape(n, -1)
    else:
        ohg, s_new = _hgrn(hq, f, hi, hg, s0, w["hnw"], batch, seq)

    h, xn_bf, gates = _outproj(oda, ohg, x2d, w["w_out"], w["n2"], w["wr_hi"], w["wr_lo"], w["br"])
    y = _moe(xn_bf, h, gates, w["wg"], w["wu"], w["wd"])
    kv_shape = (batch, seq, heads, DA_HEAD_W)
    return y.reshape(batch, seq, d), k.reshape(kv_shape), v.reshape(kv_shape), s_new


def _layer_weights(l, lb_all, norm1_w, w_in, q_norm_w, k_norm_w, lambda_q1, lambda_k1, lambda_q2,
                   lambda_k2, subln_w, hg_norm_w, w_out, norm2_w, w_router_grp, b_router_grp,
                   w_router_exp, b_router_exp, w_gate, w_up, w_down):
    d = w_in.shape[1]
    da_w = w_out.shape[1] // 2
    n_grp_norm = da_w // DA_HEAD_DIM
    row = lambda a: a.reshape(1, -1).astype(F32)
    g_id = jnp.arange(da_w) // DA_HEAD_DIM
    gm = jnp.where(g_id[:, None] == g_id[None, :], 1.0 / DA_HEAD_DIM, 0.0).astype(BF16)
    wr = jnp.zeros((d, LANES), F32)
    wr = wr.at[:, :N_EXPERTS].set(w_router_exp[l]).at[:, N_EXPERTS:N_EXPERTS + N_GROUPS].set(
        w_router_grp[l])
    wr_hi = wr.astype(BF16)
    wr_lo = (wr - wr_hi.astype(F32)).astype(BF16)
    br = jnp.zeros((1, LANES), F32)
    br = br.at[0, :N_EXPERTS].set(b_router_exp[l]).at[0, N_EXPERTS:N_EXPERTS + N_GROUPS].set(
        b_router_grp[l])
    return dict(
        n1=row(norm1_w[l]), w_in=w_in[l].astype(BF16),
        qw=row(jnp.tile(q_norm_w[l], n_grp_norm)), kw=row(jnp.tile(k_norm_w[l], n_grp_norm)),
        lb=row(lb_all[l]), gm=gm,
        lq1=row(lambda_q1[l]), lk1=row(lambda_k1[l]), lq2=row(lambda_q2[l]), lk2=row(lambda_k2[l]),
        sw=row(subln_w[l]), hnw=row(hg_norm_w[l]),
        w_out=w_out[l].astype(BF16), n2=row(norm2_w[l]), wr_hi=wr_hi, wr_lo=wr_lo, br=br,
        wg=w_gate[l].astype(BF16), wu=w_up[l].astype(BF16), wd=w_down[l].astype(BF16),
    )


def kernel(x_prompt, x_sample, cache_k, cache_v, state_hgrn, page_table, norm1_w, w_in, q_norm_w, k_norm_w, lambda_q1, lambda_k1, lambda_q2, lambda_k2, subln_w, hg_lb, hg_norm_w, w_out, norm2_w, w_router_grp, b_router_grp, w_router_exp, b_router_exp, w_gate, w_up, w_down):
    depth = norm1_w.shape[0]
    n_phys, page, heads, head_w = cache_k.shape[1:]
    lb_all = jnp.cumsum(jax.nn.softmax(hg_lb.astype(F32), axis=0), axis=0)
    yp, ys = x_prompt, x_sample
    outs = [[] for _ in range(6)]
    for l in range(depth):
        w = _layer_weights(l, lb_all, norm1_w, w_in, q_norm_w, k_norm_w, lambda_q1, lambda_k1,
                           lambda_q2, lambda_k2, subln_w, hg_norm_w, w_out, norm2_w,
                           w_router_grp, b_router_grp, w_router_exp, b_router_exp,
                           w_gate, w_up, w_down)
        s0p = jnp.zeros((x_prompt.shape[0],) + state_hgrn.shape[2:], F32)
        yp, k_r, v_r, s_r = _layer(yp, s0p, None, l, w)
        past = (cache_k[l].reshape(n_phys, page * heads, head_w),
                cache_v[l].reshape(n_phys, page * heads, head_w), page_table)
        ys, k_s, v_s, s_s = _layer(ys, state_hgrn[l], past, l, w)
        for lst, val in zip(outs, (k_r, v_r, s_r, k_s, v_s, s_s)):
            lst.append(val)
    kp, vp, sp, kss, vss, sss = (jnp.stack(o) for o in outs)
    return (yp, ys, kp, vp, sp, kss, vss, sss)
```

```python
import functools
import math

import jax
import jax.numpy as jnp
from jax import lax
from jax.experimental import pallas as pl
from jax.experimental.pallas import tpu as pltpu

F32 = jnp.float32
BF16 = jnp.bfloat16

DA_HEAD_DIM = 64
DA_HEAD_W = 2 * DA_HEAD_DIM
HG_DK = 128
HG_DV = 128
HG_CHUNK = 128
_HG_FAST_BLOCK = 16
_HG_FAST_LIMIT = 60.0
N_GROUPS = 4
EXPERTS_PER_GROUP = 4
N_EXPERTS = N_GROUPS * EXPERTS_PER_GROUP
EPS = 1e-6
NEG_INF = -1e30
_Q_SCALE = DA_HEAD_DIM ** -0.5 * math.log2(math.e)
LANES = 128
VMEM_LIMIT = 56 * 1024 * 1024

_NT = (((1,), (1,)), ((), ()))


def _row_tile(n, pref):
    for t in range(min(pref, n), 7, -1):
        if n % t == 0 and t % 8 == 0:
            return t
    return n


def _sigmoid(x):
    return 1.0 / (1.0 + jnp.exp(-x))


def _params(*sem):
    return pltpu.CompilerParams(dimension_semantics=sem, vmem_limit_bytes=VMEM_LIMIT)


def _inproj_kernel(x_ref, n1_ref, w_ref, qw_ref, kw_ref, lb_ref, gm_ref,
                   q_ref, k_ref, v_ref, hq_ref, f_ref, hi_ref, hg_ref, *attn_refs, da_w, hg_w):
    x = x_ref[...]
    xn = x * lax.rsqrt(jnp.mean(x * x, axis=-1, keepdims=True) + EPS) * n1_ref[...]
    xb = xn.astype(BF16)

    def proj(lo, width):
        return jnp.dot(xb, w_ref[:, lo:lo + width], preferred_element_type=F32)

    gm = gm_ref[...]

    def group_rms(t, w):
        sq = t * t
        hi = sq.astype(BF16)
        lo = (sq - hi.astype(F32)).astype(BF16)
        ms = (jnp.dot(hi, gm, preferred_element_type=F32)
              + jnp.dot(lo, gm, preferred_element_type=F32))
        return t * lax.rsqrt(ms + EPS) * w

    q_ref[...] = (group_rms(proj(0, da_w), qw_ref[...]) * _Q_SCALE).astype(q_ref.dtype)
    k = group_rms(proj(da_w, da_w), kw_ref[...])
    v = proj(2 * da_w, da_w)
    if attn_refs:
        heads = da_w // DA_HEAD_W
        rows = k.shape[0]
        for h in range(heads):
            hs = slice(h * DA_HEAD_W, (h + 1) * DA_HEAD_W)
            k_ref[pl.ds(h, rows, stride=heads), :] = k[:, hs]
            v_ref[pl.ds(h, rows, stride=heads), :] = v[:, hs]
        kb_ref, vt_ref = attn_refs
        kb_ref[...] = k.astype(BF16)
        vt_ref[...] = v.T.astype(BF16)
    else:
        k_ref[...] = k
        v_ref[...] = v
    base = 3 * da_w
    hq = proj(base, hg_w)
    hq_ref[...] = hq * _sigmoid(hq)
    lb = lb_ref[...]
    f_ref[...] = lb + (1.0 - lb) * _sigmoid(proj(base + hg_w, hg_w))
    hi_ref[...] = proj(base + 2 * hg_w, hg_w)
    hg = proj(base + 3 * hg_w, hg_w)
    hg_ref[...] = hg * _sigmoid(hg)


def _inproj(x2d, n1, w_in_bf, qw, kw, lb, gm, for_prompt):
    n, d = x2d.shape
    da_w = qw.shape[1]
    hg_w = lb.shape[1]
    tm = _row_tile(n, 512)
    row = lambda w: pl.BlockSpec((tm, w), lambda i: (i, 0))
    full = lambda a: pl.BlockSpec(a.shape, lambda i: (0, 0))
    heads = da_w // DA_HEAD_W
    if for_prompt:
        kv_shape = jax.ShapeDtypeStruct((n * heads, DA_HEAD_W), F32)
        kv_spec = pl.BlockSpec((tm * heads, DA_HEAD_W), lambda i: (i, 0))
    else:
        kv_shape = jax.ShapeDtypeStruct((n, da_w), F32)
        kv_spec = row(da_w)
    out_shapes = [jax.ShapeDtypeStruct((n, da_w), BF16 if for_prompt else F32), kv_shape, kv_shape]
    out_shapes += [jax.ShapeDtypeStruct((n, hg_w), F32)] * 4
    out_specs = [row(da_w), kv_spec, kv_spec] + [row(hg_w)] * 4
    if for_prompt:
        out_shapes += [jax.ShapeDtypeStruct((n, da_w), BF16), jax.ShapeDtypeStruct((da_w, n), BF16)]
        out_specs += [row(da_w), pl.BlockSpec((da_w, tm), lambda i: (0, i))]
    return pl.pallas_call(
        functools.partial(_inproj_kernel, da_w=da_w, hg_w=hg_w),
        grid=(n // tm,),
        in_specs=[row(d), full(n1), full(w_in_bf), full(qw), full(kw), full(lb), full(gm)],
        out_specs=out_specs,
        out_shape=out_shapes,
        compiler_params=_params("parallel"),
    )(x2d, n1, w_in_bf, qw, kw, lb, gm)


def _lambda(lq1_ref, lk1_ref, lq2_ref, lk2_ref, lam_init):
    s1 = jnp.sum(lq1_ref[...] * lk1_ref[...], axis=-1, keepdims=True)
    s2 = jnp.sum(lq2_ref[...] * lk2_ref[...], axis=-1, keepdims=True)
    return jnp.exp(s1) - jnp.exp(s2) + lam_init


def _softmax_update(s, m, l, acc, vb):
    m_new = jnp.maximum(m, jnp.max(s, axis=-1, keepdims=True))
    p = jnp.exp2(s - m_new)
    alpha = jnp.exp2(m - m_new)
    l_new = alpha * l + jnp.sum(p, axis=-1, keepdims=True)
    acc_new = alpha * acc + jnp.dot(p.astype(BF16), vb, preferred_element_type=F32)
    return m_new, l_new, acc_new


def _sub_ln(o, sw, lam_init):
    y = o * lax.rsqrt(jnp.mean(o * o, axis=-1, keepdims=True) + EPS)
    return y * sw * (1.0 - lam_init)


_ONES_ROWS = 16


def _attn_prompt_kernel(q_ref, k_ref, vt_ref, lq1_ref, lk1_ref, lq2_ref, lk2_ref, sw_ref,
                        o_ref, s_sc, *, blk, lam_init):
    qi = pl.program_id(2)
    q = q_ref[...]
    lane = lax.broadcasted_iota(jnp.int32, q.shape, 1)
    zero = jnp.zeros_like(q)
    qa = jnp.where(lane < DA_HEAD_DIM, q, zero)
    qb = jnp.where(lane >= DA_HEAD_DIM, q, zero)
    ones = jnp.ones((_ONES_ROWS, blk), BF16)

    def update(s, m, acc, vt1):
        m_new = jnp.maximum(m, jnp.max(s, axis=0, keepdims=True))
        p = jnp.exp2(s - m_new).astype(BF16)
        acc_new = jnp.exp2(m - m_new) * acc + jnp.dot(vt1, p, preferred_element_type=F32)
        return m_new, acc_new

    def scores(c):
        kb = k_ref[pl.ds(pl.multiple_of(c * blk, blk), blk), :]
        return (lax.dot_general(kb, qa, _NT, preferred_element_type=F32),
                lax.dot_general(kb, qb, _NT, preferred_element_type=F32))

    def stage_scores(c, slot):
        s_sc[slot, 0], s_sc[slot, 1] = scores(c)

    def consume(c, s1, s2, carry):
        m1, a1, m2, a2 = carry
        vt1 = jnp.concatenate([vt_ref[:, pl.ds(pl.multiple_of(c * blk, blk), blk)], ones], axis=0)
        m1, a1 = update(s1, m1, a1, vt1)
        m2, a2 = update(s2, m2, a2, vt1)
        return m1, a1, m2, a2

    s1, s2 = scores(qi)
    key = lax.broadcasted_iota(jnp.int32, s1.shape, 0)
    qry = lax.broadcasted_iota(jnp.int32, s1.shape, 1)
    m0 = jnp.full((1, blk), NEG_INF, F32)
    a0 = jnp.zeros((DA_HEAD_W + _ONES_ROWS, blk), F32)
    carry = consume(qi, jnp.where(key <= qry, s1, NEG_INF), jnp.where(key <= qry, s2, NEG_INF),
                    (m0, a0, m0, a0))

    stage_scores(0, 0)

    def chunk_pair(j, c):
        stage_scores(2 * j + 1, 1)
        c = consume(2 * j, s_sc[0, 0], s_sc[0, 1], c)
        stage_scores(2 * j + 2, 0)
        return consume(2 * j + 1, s_sc[1, 0], s_sc[1, 1], c)

    carry = lax.fori_loop(0, qi // 2, chunk_pair, carry)
    carry = lax.cond(qi % 2 == 1,
                     lambda c: consume(qi - 1, s_sc[0, 0], s_sc[0, 1], c),
                     lambda c: c, carry)
    _, a1, _, a2 = carry
    lam = _lambda(lq1_ref, lk1_ref, lq2_ref, lk2_ref, lam_init)
    o1 = a1[:DA_HEAD_W] * (1.0 / a1[DA_HEAD_W:DA_HEAD_W + 1])
    o2 = a2[:DA_HEAD_W] * (1.0 / a2[DA_HEAD_W:DA_HEAD_W + 1])
    o_ref[...] = _sub_ln((o1 - lam * o2).T, sw_ref[...], lam_init)


def _attn_prompt(q, kb, vt, lq1, lk1, lq2, lk2, sw, batch, seq, lam_init):
    n, da_w = q.shape
    heads = da_w // DA_HEAD_W
    blk = _row_tile(seq, 512)
    nq = seq // blk
    small = lambda a: pl.BlockSpec(a.shape, lambda b, h, i: (0, 0))
    return pl.pallas_call(
        functools.partial(_attn_prompt_kernel, blk=blk, lam_init=lam_init),
        grid=(batch, heads, nq),
        in_specs=[
            pl.BlockSpec((blk, DA_HEAD_W), lambda b, h, i: (b * nq + i, h)),
            pl.BlockSpec((seq, DA_HEAD_W), lambda b, h, i: (b, h)),
            pl.BlockSpec((DA_HEAD_W, seq), lambda b, h, i: (h, b)),
            small(lq1), small(lk1), small(lq2), small(lk2), small(sw),
        ],
        out_specs=pl.BlockSpec((blk, DA_HEAD_W), lambda b, h, i: (b * nq + i, h)),
        out_shape=jax.ShapeDtypeStruct((n, da_w), F32),
        scratch_shapes=[pltpu.VMEM((2, 2, blk, blk), F32)],
        compiler_params=_params("parallel", "parallel", "arbitrary"),
    )(q, kb, vt, lq1, lk1, lq2, lk2, sw)


def _attn_decode_kernel(pt_ref, q_ref, kn_ref, vn_ref, lq1_ref, lk1_ref, lq2_ref, lk2_ref,
                        sw_ref, *rest, pages_per_step, heads, page, lam_init):
    del pt_ref
    k_refs = rest[:pages_per_step]
    v_refs = rest[pages_per_step:2 * pages_per_step]
    o_ref = rest[2 * pages_per_step]
    m_sc, l_sc, acc_sc = rest[2 * pages_per_step + 1:]
    g = pl.program_id(1)
    q = q_ref[...]
    t_new = q.shape[0]
    lane = lax.broadcasted_iota(jnp.int32, (t_new, DA_HEAD_W), 1)
    head_cols = [slice(h * DA_HEAD_W, (h + 1) * DA_HEAD_W) for h in range(heads)]
    q2 = [jnp.concatenate([jnp.where(lane < DA_HEAD_DIM, q[:, hs], 0.0),
                           jnp.where(lane >= DA_HEAD_DIM, q[:, hs], 0.0)], axis=0).astype(BF16)
          for hs in head_cols]

    @pl.when(g == 0)
    def _new_tokens():
        pad = jnp.zeros((LANES - t_new, DA_HEAD_W), F32)
        for h, hs in enumerate(head_cols):
            kn = jnp.concatenate([kn_ref[:, hs], pad], axis=0).astype(BF16)
            vn = jnp.concatenate([vn_ref[:, hs], pad], axis=0).astype(BF16)
            s = lax.dot_general(q2[h], kn, _NT, preferred_element_type=F32)
            r = lax.broadcasted_iota(jnp.int32, s.shape, 0)
            c = lax.broadcasted_iota(jnp.int32, s.shape, 1)
            tok = jnp.where(r >= t_new, r - t_new, r)
            s = jnp.where(c <= tok, s, NEG_INF)
            m = jnp.max(s, axis=-1, keepdims=True)
            p = jnp.exp2(s - m)
            m_sc[h] = m
            l_sc[h] = jnp.sum(p, axis=-1, keepdims=True)
            acc_sc[h] = jnp.dot(p.astype(BF16), vn, preferred_element_type=F32)

    for h in range(heads):
        kb = jnp.concatenate([r_[pl.ds(h, page, stride=heads), :] for r_ in k_refs],
                             axis=0).astype(BF16)
        vb = jnp.concatenate([r_[pl.ds(h, page, stride=heads), :] for r_ in v_refs],
                             axis=0).astype(BF16)
        s = lax.dot_general(q2[h], kb, _NT, preferred_element_type=F32)
        m, l, acc = _softmax_update(s, m_sc[h], l_sc[h], acc_sc[h], vb)
        m_sc[h] = m
        l_sc[h] = l
        acc_sc[h] = acc

    @pl.when(g == pl.num_programs(1) - 1)
    def _finish():
        lam = _lambda(lq1_ref, lk1_ref, lq2_ref, lk2_ref, lam_init)
        sw = sw_ref[...]
        for h in range(heads):
            on = acc_sc[h] * (1.0 / l_sc[h])
            o = on[:t_new] - lam * on[t_new:]
            o_ref[:, h * DA_HEAD_W:(h + 1) * DA_HEAD_W] = _sub_ln(o, sw, lam_init)


def _attn_decode(q3, kn3, vn3, cache_k3, cache_v3, page_table, lq1, lk1, lq2, lk2, sw,
                 heads, lam_init):
    nb, t_new, da_w = q3.shape
    n_pages = page_table.shape[1]
    page = cache_k3.shape[1] // heads
    pps = math.gcd(n_pages, 16)
    n_steps = n_pages // pps
    tok = pl.BlockSpec((None, t_new, da_w), lambda b, g, pt: (b, 0, 0))
    small = lambda a: pl.BlockSpec(a.shape, lambda b, g, pt: (0, 0))

    def page_spec(j):
        return pl.BlockSpec((None, page * heads, DA_HEAD_W),
                            lambda b, g, pt: (pt[b, g * pps + j], 0, 0))

    grid_spec = pltpu.PrefetchScalarGridSpec(
        num_scalar_prefetch=1,
        grid=(nb, n_steps),
        in_specs=[tok, tok, tok, small(lq1), small(lk1), small(lq2), small(lk2), small(sw)]
        + [page_spec(j) for j in range(pps)] * 2,
        out_specs=tok,
        scratch_shapes=[pltpu.VMEM((heads, 2 * t_new, 1), F32),
                        pltpu.VMEM((heads, 2 * t_new, 1), F32),
                        pltpu.VMEM((heads, 2 * t_new, DA_HEAD_W), F32)],
    )
    return pl.pallas_call(
        functools.partial(_attn_decode_kernel, pages_per_step=pps, heads=heads, page=page,
                          lam_init=lam_init),
        grid_spec=grid_spec,
        out_shape=jax.ShapeDtypeStruct((nb, t_new, da_w), F32),
        compiler_params=_params("parallel", "arbitrary"),
    )(page_table, q3, kn3, vn3, lq1, lk1, lq2, lk2, sw,
      *([cache_k3] * pps), *([cache_v3] * pps))


def _hgrn_kernel(q_ref, f_ref, v_ref, g_ref, s0_ref, nw_ref, o_ref, s_out_ref, st_sc, b_sc,
                 *, heads, n_chunks):
    c_len = HG_CHUNK
    step = pl.program_id(1)

    @pl.when(step == 0)
    def _load_state():
        for h in range(heads):
            st_sc[h] = s0_ref[h].T

    r = lax.broadcasted_iota(jnp.int32, (c_len, c_len), 0)
    c = lax.broadcasted_iota(jnp.int32, (c_len, c_len), 1)
    tri = jnp.where(c <= r, 1.0, 0.0).astype(BF16)

    def midpoint_mask(lv):
        return ((r // lv) == (c // lv)) & ((r % lv) >= lv // 2) & ((c % lv) < lv // 2)

    level_masks = {lv: midpoint_mask(lv) for lv in (16, 32, 64, 128)}
    blk = _HG_FAST_BLOCK
    diag_mask = ((r // blk) == (c // blk)) & (c <= r)
    sub = lax.broadcasted_iota(jnp.int32, (c_len // 8, 8, 1), 1)
    nw = nw_ref[...]

    def cumsum_rows(x):
        x1 = x.astype(BF16)
        r1 = x - x1.astype(F32)
        x2 = r1.astype(BF16)
        x3 = (r1 - x2.astype(F32)).astype(BF16)
        return (jnp.dot(tri, x1, preferred_element_type=F32)
                + jnp.dot(tri, x2, preferred_element_type=F32)
                + jnp.dot(tri, x3, preferred_element_type=F32))

    def midpoint_terms(q, kk, b, levels):
        a_mat = jnp.zeros((c_len, c_len), F32)
        for lv in levels:
            bl = b.reshape(c_len // lv, lv, HG_DK)
            mid = bl[:, lv // 2 - 1:lv // 2, :]
            e = jnp.exp2(-jnp.abs(bl - mid)).reshape(c_len, HG_DK)
            a_lv = lax.dot_general((q * e).astype(BF16), (kk * e).astype(BF16), _NT,
                                   preferred_element_type=F32)
            a_mat = a_mat + jnp.where(level_masks[lv], a_lv, 0.0)
        return a_mat

    def intra_fast(q, kk, v, b, f):
        bl = b.reshape(c_len // blk, blk, HG_DK)
        lf0 = jnp.log2(f.reshape(c_len // blk, blk, HG_DK)[:, 0:1, :])
        d = (bl - (bl[:, 0:1, :] - lf0)).reshape(c_len, HG_DK)
        a_blk = lax.dot_general((q * jnp.exp2(d)).astype(BF16), (kk * jnp.exp2(-d)).astype(BF16),
                                _NT, preferred_element_type=F32)
        a_mat = jnp.where(diag_mask, a_blk, 0.0) + midpoint_terms(q, kk, b, (32, 64, 128))
        return jnp.dot(a_mat.astype(BF16), v.astype(BF16), preferred_element_type=F32)

    def intra_exact(q, kk, v, b, f):
        del f
        q3 = q.reshape(c_len // 8, 8, HG_DK)
        b3 = b.reshape(c_len // 8, 8, HG_DK)
        k3 = kk.reshape(c_len // 8, 8, HG_DK)
        v3 = v.reshape(c_len // 8, 8, HG_DV)
        o3 = jnp.zeros((c_len // 8, 8, HG_DV), F32)
        for s in range(8):
            dec = jnp.exp2(jnp.minimum(b3 - b3[:, s:s + 1, :], 0.0))
            a = jnp.sum(q3 * k3[:, s:s + 1, :] * dec, axis=-1, keepdims=True)
            o3 = o3 + jnp.where(sub >= s, a, 0.0) * v3[:, s:s + 1, :]
        a_mat = midpoint_terms(q, kk, b, (16, 32, 64, 128))
        return o3.reshape(c_len, HG_DV) + jnp.dot(a_mat.astype(BF16), v.astype(BF16),
                                                  preferred_element_type=F32)

    def head_cols(h):
        return slice(h * HG_DK, (h + 1) * HG_DK)

    def prefix(ci, growth):
        r0 = pl.multiple_of(ci * c_len, c_len)
        for h in range(heads):
            lf = jnp.log2(f_ref[pl.ds(r0, c_len), head_cols(h)])
            b = cumsum_rows(lf)
            b_sc[pl.ds(r0, c_len), head_cols(h)] = b
            bl = b.reshape(c_len // blk, blk, HG_DK)
            before = bl[:, 0:1, :] - lf.reshape(c_len // blk, blk, HG_DK)[:, 0:1, :]
            growth = jnp.maximum(growth, before - bl[:, blk - 1:blk, :])
        return growth

    growth = lax.fori_loop(0, n_chunks, prefix, jnp.zeros((c_len // blk, 1, HG_DK), F32))
    safe = jnp.max(growth) < _HG_FAST_LIMIT

    def run_chunks(intra):
        for ci in range(n_chunks):
            rows = pl.ds(ci * c_len, c_len)
            for h in range(heads):
                hs = head_cols(h)
                q = q_ref[rows, hs]
                f = f_ref[rows, hs]
                v = v_ref[rows, hs]
                b = b_sc[rows, hs]
                kk = 1.0 - f
                st = st_sc[h]
                o = intra(q, kk, v, b, f) + lax.dot_general(
                    (q * jnp.exp2(b)).astype(BF16), st.astype(BF16), _NT,
                    preferred_element_type=F32)
                b_end = b[c_len - 1:c_len, :]
                k_end = kk * jnp.exp2(b_end - b)
                st_sc[h] = st * jnp.exp2(b_end) + jnp.dot(
                    v.T.astype(BF16), k_end.astype(BF16), preferred_element_type=F32)
                y = o * lax.rsqrt(jnp.mean(o * o, axis=-1, keepdims=True) + EPS) * nw
                o_ref[rows, hs] = y * g_ref[rows, hs]

    pl.when(safe)(lambda: run_chunks(intra_fast))
    pl.when(jnp.logical_not(safe))(lambda: run_chunks(intra_exact))

    @pl.when(step == pl.num_programs(1) - 1)
    def _store_state():
        for h in range(heads):
            s_out_ref[h] = st_sc[h].T


def _hgrn(hq, f, hi, hg, s0, nw, batch, seq):
    n, hg_w = hq.shape
    heads = hg_w // HG_DK
    tb = HG_CHUNK * math.gcd(seq // HG_CHUNK, 4)
    steps = seq // tb
    row = pl.BlockSpec((tb, hg_w), lambda b, t: (b * steps + t, 0))
    st = pl.BlockSpec((None, heads, HG_DK, HG_DV), lambda b, t: (b, 0, 0, 0))
    return pl.pallas_call(
        functools.partial(_hgrn_kernel, heads=heads, n_chunks=tb // HG_CHUNK),
        grid=(batch, steps),
        in_specs=[row, row, row, row, st, pl.BlockSpec(nw.shape, lambda b, t: (0, 0))],
        out_specs=[row, st],
        out_shape=[jax.ShapeDtypeStruct((n, hg_w), F32),
                   jax.ShapeDtypeStruct(s0.shape, F32)],
        scratch_shapes=[pltpu.VMEM((heads, HG_DV, HG_DK), F32), pltpu.VMEM((tb, hg_w), F32)],
        compiler_params=_params("parallel", "arbitrary"),
    )(hq, f, hi, hg, s0, nw)


def _outproj_kernel(oda_ref, ohg_ref, x_ref, wo_ref, n2_ref, wrh_ref, wrl_ref, br_ref,
                    h_ref, xn_ref, gates_ref, *, da_w):
    o = (jnp.dot(oda_ref[...].astype(BF16), wo_ref[:da_w, :], preferred_element_type=F32)
         + jnp.dot(ohg_ref[...].astype(BF16), wo_ref[da_w:, :], preferred_element_type=F32))
    h = x_ref[...] + o
    h_ref[...] = h
    xn = h * lax.rsqrt(jnp.mean(h * h, axis=-1, keepdims=True) + EPS) * n2_ref[...]
    xh = xn.astype(BF16)
    xn_ref[...] = xh
    xl = (xn - xh.astype(F32)).astype(BF16)
    wrh = wrh_ref[...]
    logits = (jnp.dot(xh, wrh, preferred_element_type=F32)
              + jnp.dot(xl, wrh, preferred_element_type=F32)
              + jnp.dot(xh, wrl_ref[...], preferred_element_type=F32)) + br_ref[...]
    lane = lax.broadcasted_iota(jnp.int32, logits.shape, 1)
    lane_f = lane.astype(F32)
    big = float(LANES)
    is_g = (lane >= N_EXPERTS) & (lane < N_EXPERTS + N_GROUPS)
    gl = jnp.where(is_g, logits, NEG_INF)
    g_max = jnp.max(gl, axis=-1, keepdims=True)
    g_idx = jnp.min(jnp.where(gl == g_max, lane_f, big), axis=-1, keepdims=True) - N_EXPERTS
    g_w = 1.0 / jnp.sum(jnp.exp(gl - g_max), axis=-1, keepdims=True)
    grp_of_lane = (lane // EXPERTS_PER_GROUP).astype(F32)
    sel = (lane < N_EXPERTS) & (grp_of_lane == g_idx)
    el = jnp.where(sel, logits, NEG_INF)
    v1 = jnp.max(el, axis=-1, keepdims=True)
    i1 = jnp.min(jnp.where(sel, jnp.where(el == v1, lane_f, big), big), axis=-1, keepdims=True)
    el2 = jnp.where(lane_f == i1, NEG_INF, el)
    v2 = jnp.max(el2, axis=-1, keepdims=True)
    sel2 = sel & (lane_f != i1)
    i2 = jnp.min(jnp.where(sel2, jnp.where(el2 == v2, lane_f, big), big), axis=-1, keepdims=True)
    t = jnp.exp(v2 - v1)
    p1 = 1.0 / (1.0 + t)
    p2 = t * p1
    gates_ref[...] = (jnp.where(lane_f == i1, p1 * g_w, 0.0)
                      + jnp.where(lane_f == i2, p2 * g_w, 0.0))


def _outproj(oda, ohg, x2d, wo_bf, n2, wr_hi, wr_lo, br):
    n, d = x2d.shape
    da_w = oda.shape[1]
    tm = _row_tile(n, 512)
    row = lambda w: pl.BlockSpec((tm, w), lambda i: (i, 0))
    full = lambda a: pl.BlockSpec(a.shape, lambda i: (0, 0))
    return pl.pallas_call(
        functools.partial(_outproj_kernel, da_w=da_w),
        grid=(n // tm,),
        in_specs=[row(da_w), row(ohg.shape[1]), row(d), full(wo_bf), full(n2), full(wr_hi),
                  full(wr_lo), full(br)],
        out_specs=[row(d), row(d), row(LANES)],
        out_shape=[jax.ShapeDtypeStruct((n, d), F32), jax.ShapeDtypeStruct((n, d), BF16),
                   jax.ShapeDtypeStruct((n, LANES), F32)],
        compiler_params=_params("parallel"),
    )(oda, ohg, x2d, wo_bf, n2, wr_hi, wr_lo, br)


def _moe_kernel(x_ref, h_ref, gates_ref, wg_ref, wu_ref, wd_ref, y_ref):
    e = pl.program_id(1)

    @pl.when(e == 0)
    def _init():
        y_ref[...] = h_ref[...]

    x = x_ref[...]
    a = jnp.dot(x, wg_ref[...], preferred_element_type=F32)
    u = jnp.dot(x, wu_ref[...], preferred_element_type=F32)
    gates = gates_ref[...]
    lane = lax.broadcasted_iota(jnp.int32, gates.shape, 1)
    gate = jnp.sum(jnp.where(lane == e, gates, 0.0), axis=-1, keepdims=True)
    hid = (a * _sigmoid(a)) * u * gate
    y_ref[...] += jnp.dot(hid.astype(BF16), wd_ref[...], preferred_element_type=F32)


def _moe(xn_bf, h, gates, wg_bf, wu_bf, wd_bf):
    n, d = h.shape
    n_exp, _, ff = wg_bf.shape
    tm = _row_tile(n, 1024)
    row = lambda w: pl.BlockSpec((tm, w), lambda i, e: (i, 0))
    return pl.pallas_call(
        _moe_kernel,
        grid=(n // tm, n_exp),
        in_specs=[row(d), row(d), row(LANES),
                  pl.BlockSpec((None, d, ff), lambda i, e: (e, 0, 0)),
                  pl.BlockSpec((None, d, ff), lambda i, e: (e, 0, 0)),
                  pl.BlockSpec((None, ff, d), lambda i, e: (e, 0, 0))],
        out_specs=row(d),
        out_shape=jax.ShapeDtypeStruct((n, d), F32),
        compiler_params=_params("parallel", "arbitrary"),
    )(xn_bf, h, gates, wg_bf, wu_bf, wd_bf)


def _pad_tokens(a, batch, seq, seq_pad, value):
    a3 = a.reshape(batch, seq, a.shape[-1])
    a3 = jnp.pad(a3, ((0, 0), (0, seq_pad - seq), (0, 0)), constant_values=value)
    return a3.reshape(batch * seq_pad, a.shape[-1])


def _layer(x, s0, past, layer_idx, w):
    batch, seq, d = x.shape
    n = batch * seq
    x2d = x.reshape(n, d)
    da_w = w["qw"].shape[1]
    heads = da_w // DA_HEAD_W
    lam_init = 0.8 - 0.6 * math.exp(-0.3 * layer_idx)
    lam_args = (w["lq1"], w["lk1"], w["lq2"], w["lk2"], w["sw"])

    q, k, v, hq, f, hi, hg, *attn_in = _inproj(x2d, w["n1"], w["w_in"], w["qw"], w["kw"], w["lb"],
                                               w["gm"], past is None)
    if past is None:
        oda = _attn_prompt(q, *attn_in, *lam_args, batch, seq, lam_init)
    else:
        cache_k3, cache_v3, page_table = past
        r3 = lambda a: a.reshape(batch, seq, da_w)
        oda = _attn_decode(r3(q), r3(k), r3(v), cache_k3, cache_v3, page_table, *lam_args,
                           heads, lam_init).reshape(n, da_w)

    seq_pad = -(-seq // HG_CHUNK) * HG_CHUNK
    if seq_pad != seq:
        hq_p, hi_p, hg_p = (_pad_tokens(a, batch, seq, seq_pad, 0.0) for a in (hq, hi, hg))
        f_p = _pad_tokens(f, batch, seq, seq_pad, 1.0)
        ohg, s_new = _hgrn(hq_p, f_p, hi_p, hg_p, s0, w["hnw"], batch, seq_pad)
        ohg = ohg.reshape(batch, seq_pad, -1)[:, :seq].reshape(n, -1)
    else:
        ohg, s_new = _hgrn(hq, f, hi, hg, s0, w["hnw"], batch, seq)

    h, xn_bf, gates = _outproj(oda, ohg, x2d, w["w_out"], w["n2"], w["wr_hi"], w["wr_lo"], w["br"])
    y = _moe(xn_bf, h, gates, w["wg"], w["wu"], w["wd"])
    kv_shape = (batch, seq, heads, DA_HEAD_W)
    return y.reshape(batch, seq, d), k.reshape(kv_shape), v.reshape(kv_shape), s_new


def _layer_weights(l, lb_all, norm1_w, w_in, q_norm_w, k_norm_w, lambda_q1, lambda_k1, lambda_q2,
                   lambda_k2, subln_w, hg_norm_w, w_out, norm2_w, w_router_grp, b_router_grp,
                   w_router_exp, b_router_exp, w_gate, w_up, w_down):
    d = w_in.shape[1]
    da_w = w_out.shape[1] // 2
    n_grp_norm = da_w // DA_HEAD_DIM
    row = lambda a: a.reshape(1, -1).astype(F32)
    g_id = jnp.arange(da_w) // DA_HEAD_DIM
    gm = jnp.where(g_id[:, None] == g_id[None, :], 1.0 / DA_HEAD_DIM, 0.0).astype(BF16)
    wr = jnp.zeros((d, LANES), F32)
    wr = wr.at[:, :N_EXPERTS].set(w_router_exp[l]).at[:, N_EXPERTS:N_EXPERTS + N_GROUPS].set(
        w_router_grp[l])
    wr_hi = wr.astype(BF16)
    wr_lo = (wr - wr_hi.astype(F32)).astype(BF16)
    br = jnp.zeros((1, LANES), F32)
    br = br.at[0, :N_EXPERTS].set(b_router_exp[l]).at[0, N_EXPERTS:N_EXPERTS + N_GROUPS].set(
        b_router_grp[l])
    return dict(
        n1=row(norm1_w[l]), w_in=w_in[l].astype(BF16),
        qw=row(jnp.tile(q_norm_w[l], n_grp_norm)), kw=row(jnp.tile(k_norm_w[l], n_grp_norm)),
        lb=row(lb_all[l]), gm=gm,
        lq1=row(lambda_q1[l]), lk1=row(lambda_k1[l]), lq2=row(lambda_q2[l]), lk2=row(lambda_k2[l]),
        sw=row(subln_w[l]), hnw=row(hg_norm_w[l]),
        w_out=w_out[l].astype(BF16), n2=row(norm2_w[l]), wr_hi=wr_hi, wr_lo=wr_lo, br=br,
        wg=w_gate[l].astype(BF16), wu=w_up[l].astype(BF16), wd=w_down[l].astype(BF16),
    )


def kernel(x_prompt, x_sample, cache_k, cache_v, state_hgrn, page_table, norm1_w, w_in, q_norm_w, k_norm_w, lambda_q1, lambda_k1, lambda_q2, lambda_k2, subln_w, hg_lb, hg_norm_w, w_out, norm2_w, w_router_grp, b_router_grp, w_router_exp, b_router_exp, w_gate, w_up, w_down):
    depth = norm1_w.shape[0]
    n_phys, page, heads, head_w = cache_k.shape[1:]
    lb_all = jnp.cumsum(jax.nn.softmax(hg_lb.astype(F32), axis=0), axis=0)
    yp, ys = x_prompt, x_sample
    outs = [[] for _ in range(6)]
    for l in range(depth):
        w = _layer_weights(l, lb_all, norm1_w, w_in, q_norm_w, k_norm_w, lambda_q1, lambda_k1,
                           lambda_q2, lambda_k2, subln_w, hg_norm_w, w_out, norm2_w,
                           w_router_grp, b_router_grp, w_router_exp, b_router_exp,
                           w_gate, w_up, w_down)
        s0p = jnp.zeros((x_prompt.shape[0],) + state_hgrn.shape[2:], F32)
        yp, k_r, v_r, s_r = _layer(yp, s0p, None, l, w)
        past = (cache_k[l].reshape(n_phys, page * heads, head_w),
                cache_v[l].reshape(n_phys, page * heads, head_w), page_table)
        ys, k_s, v_s, s_s = _layer(ys, state_hgrn[l], past, l, w)
        for lst, val in zip(outs, (k_r, v_r, s_r, k_s, v_s, s_s)):
            lst.append(val)
    kp, vp, sp, kss, vss, sss = (jnp.stack(o) for o in outs)
    return (yp, ys, kp, vp, sp, kss, vss, sss)
```

```python
import functools
import math

import jax
import jax.numpy as jnp
from jax import lax
from jax.experimental import pallas as pl
from jax.experimental.pallas import tpu as pltpu

F32 = jnp.float32
BF16 = jnp.bfloat16

DA_HEAD_DIM = 64
DA_HEAD_W = 2 * DA_HEAD_DIM
HG_DK = 128
HG_DV = 128
HG_CHUNK = 128
_HG_FAST_BLOCK = 16
_HG_FAST_LIMIT = 60.0
N_GROUPS = 4
EXPERTS_PER_GROUP = 4
N_EXPERTS = N_GROUPS * EXPERTS_PER_GROUP
EPS = 1e-6
NEG_INF = -1e30
_Q_SCALE = DA_HEAD_DIM ** -0.5 * math.log2(math.e)
LANES = 128
VMEM_LIMIT = 56 * 1024 * 1024

_NT = (((1,), (1,)), ((), ()))


def _row_tile(n, pref):
    for t in range(min(pref, n), 7, -1):
        if n % t == 0 and t % 8 == 0:
            return t
    return n


def _sigmoid(x):
    return 1.0 / (1.0 + jnp.exp(-x))


def _params(*sem):
    return pltpu.CompilerParams(dimension_semantics=sem, vmem_limit_bytes=VMEM_LIMIT)


def _inproj_kernel(x_ref, n1_ref, w_ref, qw_ref, kw_ref, lb_ref, gm_ref,
                   q_ref, k_ref, v_ref, hq_ref, f_ref, hi_ref, hg_ref, *attn_refs, da_w, hg_w):
    x = x_ref[...]
    xn = x * lax.rsqrt(jnp.mean(x * x, axis=-1, keepdims=True) + EPS) * n1_ref[...]
    xb = xn.astype(BF16)

    def proj(lo, width):
        return jnp.dot(xb, w_ref[:, lo:lo + width], preferred_element_type=F32)

    gm = gm_ref[...]

    def group_rms(t, w):
        sq = t * t
        hi = sq.astype(BF16)
        lo = (sq - hi.astype(F32)).astype(BF16)
        ms = (jnp.dot(hi, gm, preferred_element_type=F32)
              + jnp.dot(lo, gm, preferred_element_type=F32))
        return t * lax.rsqrt(ms + EPS) * w

    q_ref[...] = (group_rms(proj(0, da_w), qw_ref[...]) * _Q_SCALE).astype(q_ref.dtype)
    k = group_rms(proj(da_w, da_w), kw_ref[...])
    v = proj(2 * da_w, da_w)
    if attn_refs:
        heads = da_w // DA_HEAD_W
        rows = k.shape[0]
        for h in range(heads):
            hs = slice(h * DA_HEAD_W, (h + 1) * DA_HEAD_W)
            k_ref[pl.ds(h, rows, stride=heads), :] = k[:, hs]
            v_ref[pl.ds(h, rows, stride=heads), :] = v[:, hs]
        kb_ref, vt_ref = attn_refs
        kb_ref[...] = k.astype(BF16)
        vt_ref[...] = v.T.astype(BF16)
    else:
        k_ref[...] = k
        v_ref[...] = v
    base = 3 * da_w
    hq = proj(base, hg_w)
    hq_ref[...] = hq * _sigmoid(hq)
    lb = lb_ref[...]
    f_ref[...] = lb + (1.0 - lb) * _sigmoid(proj(base + hg_w, hg_w))
    hi_ref[...] = proj(base + 2 * hg_w, hg_w)
    hg = proj(base + 3 * hg_w, hg_w)
    hg_ref[...] = hg * _sigmoid(hg)


def _inproj(x2d, n1, w_in_bf, qw, kw, lb, gm, for_prompt):
    n, d = x2d.shape
    da_w = qw.shape[1]
    hg_w = lb.shape[1]
    tm = _row_tile(n, 512)
    row = lambda w: pl.BlockSpec((tm, w), lambda i: (i, 0))
    full = lambda a: pl.BlockSpec(a.shape, lambda i: (0, 0))
    heads = da_w // DA_HEAD_W
    if for_prompt:
        kv_shape = jax.ShapeDtypeStruct((n * heads, DA_HEAD_W), F32)
        kv_spec = pl.BlockSpec((tm * heads, DA_HEAD_W), lambda i: (i, 0))
    else:
        kv_shape = jax.ShapeDtypeStruct((n, da_w), F32)
        kv_spec = row(da_w)
    out_shapes = [jax.ShapeDtypeStruct((n, da_w), BF16 if for_prompt else F32), kv_shape, kv_shape]
    out_shapes += [jax.ShapeDtypeStruct((n, hg_w), F32)] * 4
    out_specs = [row(da_w), kv_spec, kv_spec] + [row(hg_w)] * 4
    if for_prompt:
        out_shapes += [jax.ShapeDtypeStruct((n, da_w), BF16), jax.ShapeDtypeStruct((da_w, n), BF16)]
        out_specs += [row(da_w), pl.BlockSpec((da_w, tm), lambda i: (0, i))]
    return pl.pallas_call(
        functools.partial(_inproj_kernel, da_w=da_w, hg_w=hg_w),
        grid=(n // tm,),
        in_specs=[row(d), full(n1), full(w_in_bf), full(qw), full(kw), full(lb), full(gm)],
        out_specs=out_specs,
        out_shape=out_shapes,
        compiler_params=_params("parallel"),
    )(x2d, n1, w_in_bf, qw, kw, lb, gm)


def _lambda(lq1_ref, lk1_ref, lq2_ref, lk2_ref, lam_init):
    s1 = jnp.sum(lq1_ref[...] * lk1_ref[...], axis=-1, keepdims=True)
    s2 = jnp.sum(lq2_ref[...] * lk2_ref[...], axis=-1, keepdims=True)
    return jnp.exp(s1) - jnp.exp(s2) + lam_init


def _softmax_update(s, m, l, acc, vb):
    m_new = jnp.maximum(m, jnp.max(s, axis=-1, keepdims=True))
    p = jnp.exp2(s - m_new)
    alpha = jnp.exp2(m - m_new)
    l_new = alpha * l + jnp.sum(p, axis=-1, keepdims=True)
    acc_new = alpha * acc + jnp.dot(p.astype(BF16), vb, preferred_element_type=F32)
    return m_new, l_new, acc_new


def _sub_ln(o, sw, lam_init):
    y = o * lax.rsqrt(jnp.mean(o * o, axis=-1, keepdims=True) + EPS)
    return y * sw * (1.0 - lam_init)


_ONES_ROWS = 16


def _attn_prompt_kernel(q_ref, k_ref, vt_ref, lq1_ref, lk1_ref, lq2_ref, lk2_ref, sw_ref,
                        o_ref, s_sc, *, blk, lam_init):
    qi = pl.program_id(2)
    q = q_ref[...]
    lane = lax.broadcasted_iota(jnp.int32, q.shape, 1)
    zero = jnp.zeros_like(q)
    qa = jnp.where(lane < DA_HEAD_DIM, q, zero)
    qb = jnp.where(lane >= DA_HEAD_DIM, q, zero)
    ones = jnp.ones((_ONES_ROWS, blk), BF16)

    def update(s, m, acc, vt1):
        m_new = jnp.maximum(m, jnp.max(s, axis=0, keepdims=True))
        p = jnp.exp2(s - m_new).astype(BF16)
        acc_new = jnp.exp2(m - m_new) * acc + jnp.dot(vt1, p, preferred_element_type=F32)
        return m_new, acc_new

    def scores(c):
        kb = k_ref[pl.ds(pl.multiple_of(c * blk, blk), blk), :]
        return (lax.dot_general(kb, qa, _NT, preferred_element_type=F32),
                lax.dot_general(kb, qb, _NT, preferred_element_type=F32))

    def stage_scores(c, slot):
        s_sc[slot, 0], s_sc[slot, 1] = scores(c)

    def consume(c, s1, s2, carry):
        m1, a1, m2, a2 = carry
        vt1 = jnp.concatenate([vt_ref[:, pl.ds(pl.multiple_of(c * blk, blk), blk)], ones], axis=0)
        m1, a1 = update(s1, m1, a1, vt1)
        m2, a2 = update(s2, m2, a2, vt1)
        return m1, a1, m2, a2

    s1, s2 = scores(qi)
    key = lax.broadcasted_iota(jnp.int32, s1.shape, 0)
    qry = lax.broadcasted_iota(jnp.int32, s1.shape, 1)
    m0 = jnp.full((1, blk), NEG_INF, F32)
    a0 = jnp.zeros((DA_HEAD_W + _ONES_ROWS, blk), F32)
    carry = consume(qi, jnp.where(key <= qry, s1, NEG_INF), jnp.where(key <= qry, s2, NEG_INF),
                    (m0, a0, m0, a0))

    stage_scores(0, 0)

    def chunk_pair(j, c):
        stage_scores(2 * j + 1, 1)
        c = consume(2 * j, s_sc[0, 0], s_sc[0, 1], c)
        stage_scores(2 * j + 2, 0)
        return consume(2 * j + 1, s_sc[1, 0], s_sc[1, 1], c)

    carry = lax.fori_loop(0, qi // 2, chunk_pair, carry)
    carry = lax.cond(qi % 2 == 1,
                     lambda c: consume(qi - 1, s_sc[0, 0], s_sc[0, 1], c),
                     lambda c: c, carry)
    _, a1, _, a2 = carry
    lam = _lambda(lq1_ref, lk1_ref, lq2_ref, lk2_ref, lam_init)
    o1 = a1[:DA_HEAD_W] * (1.0 / a1[DA_HEAD_W:DA_HEAD_W + 1])
    o2 = a2[:DA_HEAD_W] * (1.0 / a2[DA_HEAD_W:DA_HEAD_W + 1])
    o_ref[...] = _sub_ln((o1 - lam * o2).T, sw_ref[...], lam_init)


def _attn_prompt(q, kb, vt, lq1, lk1, lq2, lk2, sw, batch, seq, lam_init):
    n, da_w = q.shape
    heads = da_w // DA_HEAD_W
    blk = _row_tile(seq, 512)
    nq = seq // blk
    small = lambda a: pl.BlockSpec(a.shape, lambda b, h, i: (0, 0))
    return pl.pallas_call(
        functools.partial(_attn_prompt_kernel, blk=blk, lam_init=lam_init),
        grid=(batch, heads, nq),
        in_specs=[
            pl.BlockSpec((blk, DA_HEAD_W), lambda b, h, i: (b * nq + i, h)),
            pl.BlockSpec((seq, DA_HEAD_W), lambda b, h, i: (b, h)),
            pl.BlockSpec((DA_HEAD_W, seq), lambda b, h, i: (h, b)),
            small(lq1), small(lk1), small(lq2), small(lk2), small(sw),
        ],
        out_specs=pl.BlockSpec((blk, DA_HEAD_W), lambda b, h, i: (b * nq + i, h)),
        out_shape=jax.ShapeDtypeStruct((n, da_w), F32),
        scratch_shapes=[pltpu.VMEM((2, 2, blk, blk), F32)],
        compiler_params=_params("parallel", "parallel", "arbitrary"),
    )(q, kb, vt, lq1, lk1, lq2, lk2, sw)


def _attn_decode_kernel(pt_ref, q_ref, kn_ref, vn_ref, lq1_ref, lk1_ref, lq2_ref, lk2_ref,
                        sw_ref, ck_hbm, cv_hbm, o_ref, kbuf, vbuf, sem, m_sc, l_sc, acc_sc,
                        *, pages_per_step, heads, page, lam_init):
    g = pl.program_id(1)
    n_steps = pl.num_programs(1)
    step = pl.program_id(0) * n_steps + g
    slot = step % 2

    def page_copies(b, gg, sl):
        copies = []
        for j in range(pages_per_step):
            pid = pt_ref[b, gg * pages_per_step + j]
            copies.append(pltpu.make_async_copy(ck_hbm.at[pid], kbuf.at[sl, j], sem.at[0, sl]))
            copies.append(pltpu.make_async_copy(cv_hbm.at[pid], vbuf.at[sl, j], sem.at[1, sl]))
        return copies

    @pl.when(step == 0)
    def _first_fetch():
        for cp in page_copies(0, 0, 0):
            cp.start()

    @pl.when(step + 1 < pl.num_programs(0) * n_steps)
    def _prefetch_next():
        nxt = step + 1
        for cp in page_copies(nxt // n_steps, nxt % n_steps, 1 - slot):
            cp.start()

    for cp in page_copies(pl.program_id(0), g, slot):
        cp.wait()
    k_refs = [kbuf.at[slot, j] for j in range(pages_per_step)]
    v_refs = [vbuf.at[slot, j] for j in range(pages_per_step)]
    q = q_ref[...]
    t_new = q.shape[0]
    lane = lax.broadcasted_iota(jnp.int32, (t_new, DA_HEAD_W), 1)
    head_cols = [slice(h * DA_HEAD_W, (h + 1) * DA_HEAD_W) for h in range(heads)]
    q2 = [jnp.concatenate([jnp.where(lane < DA_HEAD_DIM, q[:, hs], 0.0),
                           jnp.where(lane >= DA_HEAD_DIM, q[:, hs], 0.0)], axis=0).astype(BF16)
          for hs in head_cols]

    @pl.when(g == 0)
    def _new_tokens():
        pad = jnp.zeros((LANES - t_new, DA_HEAD_W), F32)
        for h, hs in enumerate(head_cols):
            kn = jnp.concatenate([kn_ref[:, hs], pad], axis=0).astype(BF16)
            vn = jnp.concatenate([vn_ref[:, hs], pad], axis=0).astype(BF16)
            s = lax.dot_general(q2[h], kn, _NT, preferred_element_type=F32)
            r = lax.broadcasted_iota(jnp.int32, s.shape, 0)
            c = lax.broadcasted_iota(jnp.int32, s.shape, 1)
            tok = jnp.where(r >= t_new, r - t_new, r)
            s = jnp.where(c <= tok, s, NEG_INF)
            m = jnp.max(s, axis=-1, keepdims=True)
            p = jnp.exp2(s - m)
            m_sc[h] = m
            l_sc[h] = jnp.sum(p, axis=-1, keepdims=True)
            acc_sc[h] = jnp.dot(p.astype(BF16), vn, preferred_element_type=F32)

    for h in range(heads):
        kb = jnp.concatenate([r_[pl.ds(h, page, stride=heads), :] for r_ in k_refs],
                             axis=0).astype(BF16)
        vb = jnp.concatenate([r_[pl.ds(h, page, stride=heads), :] for r_ in v_refs],
                             axis=0).astype(BF16)
        s = lax.dot_general(q2[h], kb, _NT, preferred_element_type=F32)
        m, l, acc = _softmax_update(s, m_sc[h], l_sc[h], acc_sc[h], vb)
        m_sc[h] = m
        l_sc[h] = l
        acc_sc[h] = acc

    @pl.when(g == pl.num_programs(1) - 1)
    def _finish():
        lam = _lambda(lq1_ref, lk1_ref, lq2_ref, lk2_ref, lam_init)
        sw = sw_ref[...]
        for h in range(heads):
            on = acc_sc[h] * (1.0 / l_sc[h])
            o = on[:t_new] - lam * on[t_new:]
            o_ref[:, h * DA_HEAD_W:(h + 1) * DA_HEAD_W] = _sub_ln(o, sw, lam_init)


def _attn_decode(q3, kn3, vn3, cache_k3, cache_v3, page_table, lq1, lk1, lq2, lk2, sw,
                 heads, lam_init):
    nb, t_new, da_w = q3.shape
    n_pages = page_table.shape[1]
    page = cache_k3.shape[1] // heads
    pps = math.gcd(n_pages, 16)
    n_steps = n_pages // pps
    tok = pl.BlockSpec((None, t_new, da_w), lambda b, g, pt: (b, 0, 0))
    small = lambda a: pl.BlockSpec(a.shape, lambda b, g, pt: (0, 0))
    hbm = pl.BlockSpec(memory_space=pl.ANY)
    page_buf = pltpu.VMEM((2, pps, page * heads, DA_HEAD_W), cache_k3.dtype)

    grid_spec = pltpu.PrefetchScalarGridSpec(
        num_scalar_prefetch=1,
        grid=(nb, n_steps),
        in_specs=[tok, tok, tok, small(lq1), small(lk1), small(lq2), small(lk2), small(sw), hbm, hbm],
        out_specs=tok,
        scratch_shapes=[page_buf, page_buf,
                        pltpu.SemaphoreType.DMA((2, 2)),
                        pltpu.VMEM((heads, 2 * t_new, 1), F32),
                        pltpu.VMEM((heads, 2 * t_new, 1), F32),
                        pltpu.VMEM((heads, 2 * t_new, DA_HEAD_W), F32)],
    )
    return pl.pallas_call(
        functools.partial(_attn_decode_kernel, pages_per_step=pps, heads=heads, page=page,
                          lam_init=lam_init),
        grid_spec=grid_spec,
        out_shape=jax.ShapeDtypeStruct((nb, t_new, da_w), F32),
        compiler_params=_params("arbitrary", "arbitrary"),
    )(page_table, q3, kn3, vn3, lq1, lk1, lq2, lk2, sw, cache_k3, cache_v3)


def _hgrn_kernel(q_ref, f_ref, v_ref, g_ref, s0_ref, nw_ref, o_ref, s_out_ref, st_sc, b_sc,
                 *, heads, n_chunks):
    c_len = HG_CHUNK
    step = pl.program_id(1)

    @pl.when(step == 0)
    def _load_state():
        for h in range(heads):
            st_sc[h] = s0_ref[h].T

    r = lax.broadcasted_iota(jnp.int32, (c_len, c_len), 0)
    c = lax.broadcasted_iota(jnp.int32, (c_len, c_len), 1)
    tri = jnp.where(c <= r, 1.0, 0.0).astype(BF16)

    def midpoint_mask(lv):
        return ((r // lv) == (c // lv)) & ((r % lv) >= lv // 2) & ((c % lv) < lv // 2)

    level_masks = {lv: midpoint_mask(lv) for lv in (16, 32, 64, 128)}
    blk = _HG_FAST_BLOCK
    diag_mask = ((r // blk) == (c // blk)) & (c <= r)
    sub = lax.broadcasted_iota(jnp.int32, (c_len // 8, 8, 1), 1)
    nw = nw_ref[...]

    def cumsum_rows(x):
        x1 = x.astype(BF16)
        r1 = x - x1.astype(F32)
        x2 = r1.astype(BF16)
        x3 = (r1 - x2.astype(F32)).astype(BF16)
        return (jnp.dot(tri, x1, preferred_element_type=F32)
                + jnp.dot(tri, x2, preferred_element_type=F32)
                + jnp.dot(tri, x3, preferred_element_type=F32))

    def midpoint_terms(q, kk, b, levels):
        a_mat = jnp.zeros((c_len, c_len), F32)
        for lv in levels:
            bl = b.reshape(c_len // lv, lv, HG_DK)
            mid = bl[:, lv // 2 - 1:lv // 2, :]
            e = jnp.exp2(-jnp.abs(bl - mid)).reshape(c_len, HG_DK)
            a_lv = lax.dot_general((q * e).astype(BF16), (kk * e).astype(BF16), _NT,
                                   preferred_element_type=F32)
            a_mat = a_mat + jnp.where(level_masks[lv], a_lv, 0.0)
        return a_mat

    def intra_fast(q, kk, v, b, f):
        bl = b.reshape(c_len // blk, blk, HG_DK)
        lf0 = jnp.log2(f.reshape(c_len // blk, blk, HG_DK)[:, 0:1, :])
        d = (bl - (bl[:, 0:1, :] - lf0)).reshape(c_len, HG_DK)
        a_blk = lax.dot_general((q * jnp.exp2(d)).astype(BF16), (kk * jnp.exp2(-d)).astype(BF16),
                                _NT, preferred_element_type=F32)
        a_mat = jnp.where(diag_mask, a_blk, 0.0) + midpoint_terms(q, kk, b, (32, 64, 128))
        return jnp.dot(a_mat.astype(BF16), v.astype(BF16), preferred_element_type=F32)

    def intra_exact(q, kk, v, b, f):
        del f
        q3 = q.reshape(c_len // 8, 8, HG_DK)
        b3 = b.reshape(c_len // 8, 8, HG_DK)
        k3 = kk.reshape(c_len // 8, 8, HG_DK)
        v3 = v.reshape(c_len // 8, 8, HG_DV)
        o3 = jnp.zeros((c_len // 8, 8, HG_DV), F32)
        for s in range(8):
            dec = jnp.exp2(jnp.minimum(b3 - b3[:, s:s + 1, :], 0.0))
            a = jnp.sum(q3 * k3[:, s:s + 1, :] * dec, axis=-1, keepdims=True)
            o3 = o3 + jnp.where(sub >= s, a, 0.0) * v3[:, s:s + 1, :]
        a_mat = midpoint_terms(q, kk, b, (16, 32, 64, 128))
        return o3.reshape(c_len, HG_DV) + jnp.dot(a_mat.astype(BF16), v.astype(BF16),
                                                  preferred_element_type=F32)

    def head_cols(h):
        return slice(h * HG_DK, (h + 1) * HG_DK)

    def prefix(ci, growth):
        r0 = pl.multiple_of(ci * c_len, c_len)
        for h in range(heads):
            lf = jnp.log2(f_ref[pl.ds(r0, c_len), head_cols(h)])
            b = cumsum_rows(lf)
            b_sc[pl.ds(r0, c_len), head_cols(h)] = b
            bl = b.reshape(c_len // blk, blk, HG_DK)
            before = bl[:, 0:1, :] - lf.reshape(c_len // blk, blk, HG_DK)[:, 0:1, :]
            growth = jnp.maximum(growth, before - bl[:, blk - 1:blk, :])
        return growth

    growth = lax.fori_loop(0, n_chunks, prefix, jnp.zeros((c_len // blk, 1, HG_DK), F32))
    safe = jnp.max(growth) < _HG_FAST_LIMIT

    def run_chunks(intra):
        for ci in range(n_chunks):
            rows = pl.ds(ci * c_len, c_len)
            for h in range(heads):
                hs = head_cols(h)
                q = q_ref[rows, hs]
                f = f_ref[rows, hs]
                v = v_ref[rows, hs]
                b = b_sc[rows, hs]
                kk = 1.0 - f
                st = st_sc[h]
                o = intra(q, kk, v, b, f) + lax.dot_general(
                    (q * jnp.exp2(b)).astype(BF16), st.astype(BF16), _NT,
                    preferred_element_type=F32)
                b_end = b[c_len - 1:c_len, :]
                k_end = kk * jnp.exp2(b_end - b)
                st_sc[h] = st * jnp.exp2(b_end) + jnp.dot(
                    v.T.astype(BF16), k_end.astype(BF16), preferred_element_type=F32)
                y = o * lax.rsqrt(jnp.mean(o * o, axis=-1, keepdims=True) + EPS) * nw
                o_ref[rows, hs] = y * g_ref[rows, hs]

    pl.when(safe)(lambda: run_chunks(intra_fast))
    pl.when(jnp.logical_not(safe))(lambda: run_chunks(intra_exact))

    @pl.when(step == pl.num_programs(1) - 1)
    def _store_state():
        for h in range(heads):
            s_out_ref[h] = st_sc[h].T


def _hgrn(hq, f, hi, hg, s0, nw, batch, seq):
    n, hg_w = hq.shape
    heads = hg_w // HG_DK
    tb = HG_CHUNK * math.gcd(seq // HG_CHUNK, 4)
    steps = seq // tb
    row = pl.BlockSpec((tb, hg_w), lambda b, t: (b * steps + t, 0))
    st = pl.BlockSpec((None, heads, HG_DK, HG_DV), lambda b, t: (b, 0, 0, 0))
    return pl.pallas_call(
        functools.partial(_hgrn_kernel, heads=heads, n_chunks=tb // HG_CHUNK),
        grid=(batch, steps),
        in_specs=[row, row, row, row, st, pl.BlockSpec(nw.shape, lambda b, t: (0, 0))],
        out_specs=[row, st],
        out_shape=[jax.ShapeDtypeStruct((n, hg_w), F32),
                   jax.ShapeDtypeStruct(s0.shape, F32)],
        scratch_shapes=[pltpu.VMEM((heads, HG_DV, HG_DK), F32), pltpu.VMEM((tb, hg_w), F32)],
        compiler_params=_params("parallel", "arbitrary"),
    )(hq, f, hi, hg, s0, nw)


def _outproj_kernel(oda_ref, ohg_ref, x_ref, wo_ref, n2_ref, wrh_ref, wrl_ref, br_ref,
                    h_ref, xn_ref, gates_ref, *, da_w):
    o = (jnp.dot(oda_ref[...].astype(BF16), wo_ref[:da_w, :], preferred_element_type=F32)
         + jnp.dot(ohg_ref[...].astype(BF16), wo_ref[da_w:, :], preferred_element_type=F32))
    h = x_ref[...] + o
    h_ref[...] = h
    xn = h * lax.rsqrt(jnp.mean(h * h, axis=-1, keepdims=True) + EPS) * n2_ref[...]
    xh = xn.astype(BF16)
    xn_ref[...] = xh
    xl = (xn - xh.astype(F32)).astype(BF16)
    wrh = wrh_ref[...]
    logits = (jnp.dot(xh, wrh, preferred_element_type=F32)
              + jnp.dot(xl, wrh, preferred_element_type=F32)
              + jnp.dot(xh, wrl_ref[...], preferred_element_type=F32)) + br_ref[...]
    lane = lax.broadcasted_iota(jnp.int32, logits.shape, 1)
    lane_f = lane.astype(F32)
    big = float(LANES)
    is_g = (lane >= N_EXPERTS) & (lane < N_EXPERTS + N_GROUPS)
    gl = jnp.where(is_g, logits, NEG_INF)
    g_max = jnp.max(gl, axis=-1, keepdims=True)
    g_idx = jnp.min(jnp.where(gl == g_max, lane_f, big), axis=-1, keepdims=True) - N_EXPERTS
    g_w = 1.0 / jnp.sum(jnp.exp(gl - g_max), axis=-1, keepdims=True)
    grp_of_lane = (lane // EXPERTS_PER_GROUP).astype(F32)
    sel = (lane < N_EXPERTS) & (grp_of_lane == g_idx)
    el = jnp.where(sel, logits, NEG_INF)
    v1 = jnp.max(el, axis=-1, keepdims=True)
    i1 = jnp.min(jnp.where(sel, jnp.where(el == v1, lane_f, big), big), axis=-1, keepdims=True)
    el2 = jnp.where(lane_f == i1, NEG_INF, el)
    v2 = jnp.max(el2, axis=-1, keepdims=True)
    sel2 = sel & (lane_f != i1)
    i2 = jnp.min(jnp.where(sel2, jnp.where(el2 == v2, lane_f, big), big), axis=-1, keepdims=True)
    t = jnp.exp(v2 - v1)
    p1 = 1.0 / (1.0 + t)
    p2 = t * p1
    gates_ref[...] = (jnp.where(lane_f == i1, p1 * g_w, 0.0)
                      + jnp.where(lane_f == i2, p2 * g_w, 0.0))


def _outproj(oda, ohg, x2d, wo_bf, n2, wr_hi, wr_lo, br):
    n, d = x2d.shape
    da_w = oda.shape[1]
    tm = _row_tile(n, 512)
    row = lambda w: pl.BlockSpec((tm, w), lambda i: (i, 0))
    full = lambda a: pl.BlockSpec(a.shape, lambda i: (0, 0))
    return pl.pallas_call(
        functools.partial(_outproj_kernel, da_w=da_w),
        grid=(n // tm,),
        in_specs=[row(da_w), row(ohg.shape[1]), row(d), full(wo_bf), full(n2), full(wr_hi),
                  full(wr_lo), full(br)],
        out_specs=[row(d), row(d), row(LANES)],
        out_shape=[jax.ShapeDtypeStruct((n, d), F32), jax.ShapeDtypeStruct((n, d), BF16),
                   jax.ShapeDtypeStruct((n, LANES), F32)],
        compiler_params=_params("parallel"),
    )(oda, ohg, x2d, wo_bf, n2, wr_hi, wr_lo, br)


_MOE_EXPERTS_PER_STEP = 4


def _moe_kernel(x_ref, h_ref, gates_ref, wg_ref, wu_ref, wd_ref, y_ref):
    eb = pl.program_id(1)
    per_step = wg_ref.shape[0]

    @pl.when(eb == 0)
    def _init():
        y_ref[...] = h_ref[...]

    x = x_ref[...]
    gates = gates_ref[...]
    lane = lax.broadcasted_iota(jnp.int32, gates.shape, 1)
    hid = []
    for j in range(per_step):
        a = jnp.dot(x, wg_ref[j], preferred_element_type=F32)
        u = jnp.dot(x, wu_ref[j], preferred_element_type=F32)
        gate = jnp.sum(jnp.where(lane == eb * per_step + j, gates, 0.0), axis=-1, keepdims=True)
        hid.append(((a * _sigmoid(a)) * u * gate).astype(BF16))
    y_ref[...] += jnp.dot(jnp.concatenate(hid, axis=-1), wd_ref[...], preferred_element_type=F32)


def _moe(xn_bf, h, gates, wg_bf, wu_bf, wd_bf):
    n, d = h.shape
    n_exp, _, ff = wg_bf.shape
    per_step = math.gcd(n_exp, _MOE_EXPERTS_PER_STEP)
    tm = _row_tile(n, 1024)
    row = lambda w: pl.BlockSpec((tm, w), lambda i, e: (i, 0))
    return pl.pallas_call(
        _moe_kernel,
        grid=(n // tm, n_exp // per_step),
        in_specs=[row(d), row(d), row(LANES),
                  pl.BlockSpec((per_step, d, ff), lambda i, e: (e, 0, 0)),
                  pl.BlockSpec((per_step, d, ff), lambda i, e: (e, 0, 0)),
                  pl.BlockSpec((per_step * ff, d), lambda i, e: (e, 0))],
        out_specs=row(d),
        out_shape=jax.ShapeDtypeStruct((n, d), F32),
        compiler_params=_params("parallel", "arbitrary"),
    )(xn_bf, h, gates, wg_bf, wu_bf, wd_bf.reshape(n_exp * ff, d))


def _pad_tokens(a, batch, seq, seq_pad, value):
    a3 = a.reshape(batch, seq, a.shape[-1])
    a3 = jnp.pad(a3, ((0, 0), (0, seq_pad - seq), (0, 0)), constant_values=value)
    return a3.reshape(batch * seq_pad, a.shape[-1])


def _layer(x, s0, past, layer_idx, w):
    batch, seq, d = x.shape
    n = batch * seq
    x2d = x.reshape(n, d)
    da_w = w["qw"].shape[1]
    heads = da_w // DA_HEAD_W
    lam_init = 0.8 - 0.6 * math.exp(-0.3 * layer_idx)
    lam_args = (w["lq1"], w["lk1"], w["lq2"], w["lk2"], w["sw"])

    q, k, v, hq, f, hi, hg, *attn_in = _inproj(x2d, w["n1"], w["w_in"], w["qw"], w["kw"], w["lb"],
                                               w["gm"], past is None)
    if past is None:
        oda = _attn_prompt(q, *attn_in, *lam_args, batch, seq, lam_init)
    else:
        cache_k3, cache_v3, page_table = past
        r3 = lambda a: a.reshape(batch, seq, da_w)
        oda = _attn_decode(r3(q), r3(k), r3(v), cache_k3, cache_v3, page_table, *lam_args,
                           heads, lam_init).reshape(n, da_w)

    seq_pad = -(-seq // HG_CHUNK) * HG_CHUNK
    if seq_pad != seq:
        hq_p, hi_p, hg_p = (_pad_tokens(a, batch, seq, seq_pad, 0.0) for a in (hq, hi, hg))
        f_p = _pad_tokens(f, batch, seq, seq_pad, 1.0)
        ohg, s_new = _hgrn(hq_p, f_p, hi_p, hg_p, s0, w["hnw"], batch, seq_pad)
        ohg = ohg.reshape(batch, seq_pad, -1)[:, :seq].reshape(n, -1)
    else:
        ohg, s_new = _hgrn(hq, f, hi, hg, s0, w["hnw"], batch, seq)

    h, xn_bf, gates = _outproj(oda, ohg, x2d, w["w_out"], w["n2"], w["wr_hi"], w["wr_lo"], w["br"])
    y = _moe(xn_bf, h, gates, w["wg"], w["wu"], w["wd"])
    kv_shape = (batch, seq, heads, DA_HEAD_W)
    return y.reshape(batch, seq, d), k.reshape(kv_shape), v.reshape(kv_shape), s_new


def _layer_weights(l, lb_all, norm1_w, w_in, q_norm_w, k_norm_w, lambda_q1, lambda_k1, lambda_q2,
                   lambda_k2, subln_w, hg_norm_w, w_out, norm2_w, w_router_grp, b_router_grp,
                   w_router_exp, b_router_exp, w_gate, w_up, w_down):
    d = w_in.shape[1]
    da_w = w_out.shape[1] // 2
    n_grp_norm = da_w // DA_HEAD_DIM
    row = lambda a: a.reshape(1, -1).astype(F32)
    g_id = jnp.arange(da_w) // DA_HEAD_DIM
    gm = jnp.where(g_id[:, None] == g_id[None, :], 1.0 / DA_HEAD_DIM, 0.0).astype(BF16)
    wr = jnp.zeros((d, LANES), F32)
    wr = wr.at[:, :N_EXPERTS].set(w_router_exp[l]).at[:, N_EXPERTS:N_EXPERTS + N_GROUPS].set(
        w_router_grp[l])
    wr_hi = wr.astype(BF16)
    wr_lo = (wr - wr_hi.astype(F32)).astype(BF16)
    br = jnp.zeros((1, LANES), F32)
    br = br.at[0, :N_EXPERTS].set(b_router_exp[l]).at[0, N_EXPERTS:N_EXPERTS + N_GROUPS].set(
        b_router_grp[l])
    return dict(
        n1=row(norm1_w[l]), w_in=w_in[l].astype(BF16),
        qw=row(jnp.tile(q_norm_w[l], n_grp_norm)), kw=row(jnp.tile(k_norm_w[l], n_grp_norm)),
        lb=row(lb_all[l]), gm=gm,
        lq1=row(lambda_q1[l]), lk1=row(lambda_k1[l]), lq2=row(lambda_q2[l]), lk2=row(lambda_k2[l]),
        sw=row(subln_w[l]), hnw=row(hg_norm_w[l]),
        w_out=w_out[l].astype(BF16), n2=row(norm2_w[l]), wr_hi=wr_hi, wr_lo=wr_lo, br=br,
        wg=w_gate[l].astype(BF16), wu=w_up[l].astype(BF16), wd=w_down[l].astype(BF16),
    )


def kernel(x_prompt, x_sample, cache_k, cache_v, state_hgrn, page_table, norm1_w, w_in, q_norm_w, k_norm_w, lambda_q1, lambda_k1, lambda_q2, lambda_k2, subln_w, hg_lb, hg_norm_w, w_out, norm2_w, w_router_grp, b_router_grp, w_router_exp, b_router_exp, w_gate, w_up, w_down):
    depth = norm1_w.shape[0]
    n_phys, page, heads, head_w = cache_k.shape[1:]
    lb_all = jnp.cumsum(jax.nn.softmax(hg_lb.astype(F32), axis=0), axis=0)
    yp, ys = x_prompt, x_sample
    outs = [[] for _ in range(6)]
    for l in range(depth):
        w = _layer_weights(l, lb_all, norm1_w, w_in, q_norm_w, k_norm_w, lambda_q1, lambda_k1,
                           lambda_q2, lambda_k2, subln_w, hg_norm_w, w_out, norm2_w,
                           w_router_grp, b_router_grp, w_router_exp, b_router_exp,
                           w_gate, w_up, w_down)
        s0p = jnp.zeros((x_prompt.shape[0],) + state_hgrn.shape[2:], F32)
        yp, k_r, v_r, s_r = _layer(yp, s0p, None, l, w)
        past = (cache_k[l].reshape(n_phys, page * heads, head_w),
                cache_v[l].reshape(n_phys, page * heads, head_w), page_table)
        ys, k_s, v_s, s_s = _layer(ys, state_hgrn[l], past, l, w)
        for lst, val in zip(outs, (k_r, v_r, s_r, k_s, v_s, s_s)):
            lst.append(val)
    kp, vp, sp, kss, vss, sss = (jnp.stack(o) for o in outs)
    return (yp, ys, kp, vp, sp, kss, vss, sss)
```

```python
import functools
import math

import jax
import jax.numpy as jnp
from jax import lax
from jax.experimental import pallas as pl
from jax.experimental.pallas import tpu as pltpu

F32 = jnp.float32
BF16 = jnp.bfloat16

DA_HEAD_DIM = 64
DA_HEAD_W = 2 * DA_HEAD_DIM
HG_DK = 128
HG_DV = 128
HG_CHUNK = 128
_HG_FAST_BLOCK = 16
_HG_FAST_LIMIT = 60.0
N_GROUPS = 4
EXPERTS_PER_GROUP = 4
N_EXPERTS = N_GROUPS * EXPERTS_PER_GROUP
EPS = 1e-6
NEG_INF = -1e30
_Q_SCALE = DA_HEAD_DIM ** -0.5 * math.log2(math.e)
LANES = 128
VMEM_LIMIT = 56 * 1024 * 1024

_NT = (((1,), (1,)), ((), ()))


def _row_tile(n, pref):
    for t in range(min(pref, n), 7, -1):
        if n % t == 0 and t % 8 == 0:
            return t
    return n


def _sigmoid(x):
    return 1.0 / (1.0 + jnp.exp(-x))


def _params(*sem):
    return pltpu.CompilerParams(dimension_semantics=sem, vmem_limit_bytes=VMEM_LIMIT)


def _inproj_kernel(x_ref, n1_ref, w_ref, qw_ref, kw_ref, lb_ref, gm_ref,
                   q_ref, k_ref, v_ref, hq_ref, f_ref, hi_ref, hg_ref, *attn_refs, da_w, hg_w):
    x = x_ref[...]
    xn = x * lax.rsqrt(jnp.mean(x * x, axis=-1, keepdims=True) + EPS) * n1_ref[...]
    xb = xn.astype(BF16)

    def proj(lo, width):
        return jnp.dot(xb, w_ref[:, lo:lo + width], preferred_element_type=F32)

    gm = gm_ref[...]

    def group_rms(t, w):
        sq = t * t
        hi = sq.astype(BF16)
        lo = (sq - hi.astype(F32)).astype(BF16)
        ms = (jnp.dot(hi, gm, preferred_element_type=F32)
              + jnp.dot(lo, gm, preferred_element_type=F32))
        return t * lax.rsqrt(ms + EPS) * w

    q_ref[...] = (group_rms(proj(0, da_w), qw_ref[...]) * _Q_SCALE).astype(q_ref.dtype)
    k = group_rms(proj(da_w, da_w), kw_ref[...])
    v = proj(2 * da_w, da_w)
    if attn_refs:
        heads = da_w // DA_HEAD_W
        rows = k.shape[0]
        for h in range(heads):
            hs = slice(h * DA_HEAD_W, (h + 1) * DA_HEAD_W)
            k_ref[pl.ds(h, rows, stride=heads), :] = k[:, hs]
            v_ref[pl.ds(h, rows, stride=heads), :] = v[:, hs]
        kb_ref, vt_ref = attn_refs
        kb_ref[...] = k.astype(BF16)
        vt_ref[...] = v.T.astype(BF16)
    else:
        k_ref[...] = k
        v_ref[...] = v
    base = 3 * da_w
    hq = proj(base, hg_w)
    hq_ref[...] = hq * _sigmoid(hq)
    lb = lb_ref[...]
    f_ref[...] = lb + (1.0 - lb) * _sigmoid(proj(base + hg_w, hg_w))
    hi_ref[...] = proj(base + 2 * hg_w, hg_w)
    hg = proj(base + 3 * hg_w, hg_w)
    hg_ref[...] = hg * _sigmoid(hg)


def _inproj(x2d, n1, w_in_bf, qw, kw, lb, gm, for_prompt):
    n, d = x2d.shape
    da_w = qw.shape[1]
    hg_w = lb.shape[1]
    tm = _row_tile(n, 512)
    row = lambda w: pl.BlockSpec((tm, w), lambda i: (i, 0))
    full = lambda a: pl.BlockSpec(a.shape, lambda i: (0, 0))
    heads = da_w // DA_HEAD_W
    if for_prompt:
        kv_shape = jax.ShapeDtypeStruct((n * heads, DA_HEAD_W), F32)
        kv_spec = pl.BlockSpec((tm * heads, DA_HEAD_W), lambda i: (i, 0))
    else:
        kv_shape = jax.ShapeDtypeStruct((n, da_w), F32)
        kv_spec = row(da_w)
    out_shapes = [jax.ShapeDtypeStruct((n, da_w), BF16 if for_prompt else F32), kv_shape, kv_shape]
    out_shapes += [jax.ShapeDtypeStruct((n, hg_w), F32)] * 4
    out_specs = [row(da_w), kv_spec, kv_spec] + [row(hg_w)] * 4
    if for_prompt:
        out_shapes += [jax.ShapeDtypeStruct((n, da_w), BF16), jax.ShapeDtypeStruct((da_w, n), BF16)]
        out_specs += [row(da_w), pl.BlockSpec((da_w, tm), lambda i: (0, i))]
    return pl.pallas_call(
        functools.partial(_inproj_kernel, da_w=da_w, hg_w=hg_w),
        grid=(n // tm,),
        in_specs=[row(d), full(n1), full(w_in_bf), full(qw), full(kw), full(lb), full(gm)],
        out_specs=out_specs,
        out_shape=out_shapes,
        compiler_params=_params("parallel"),
    )(x2d, n1, w_in_bf, qw, kw, lb, gm)


def _lambda(lq1_ref, lk1_ref, lq2_ref, lk2_ref, lam_init):
    s1 = jnp.sum(lq1_ref[...] * lk1_ref[...], axis=-1, keepdims=True)
    s2 = jnp.sum(lq2_ref[...] * lk2_ref[...], axis=-1, keepdims=True)
    return jnp.exp(s1) - jnp.exp(s2) + lam_init


def _softmax_update(s, m, l, acc, vb):
    m_new = jnp.maximum(m, jnp.max(s, axis=-1, keepdims=True))
    p = jnp.exp2(s - m_new)
    alpha = jnp.exp2(m - m_new)
    l_new = alpha * l + jnp.sum(p, axis=-1, keepdims=True)
    acc_new = alpha * acc + jnp.dot(p.astype(BF16), vb, preferred_element_type=F32)
    return m_new, l_new, acc_new


def _sub_ln(o, sw, lam_init):
    y = o * lax.rsqrt(jnp.mean(o * o, axis=-1, keepdims=True) + EPS)
    return y * sw * (1.0 - lam_init)


_ONES_ROWS = 16


def _attn_prompt_kernel(q_ref, k_ref, vt_ref, lq1_ref, lk1_ref, lq2_ref, lk2_ref, swc_ref,
                        o_ref, s_sc, *, blk, lam_init):
    qi = pl.program_id(2)
    q = q_ref[...]
    lane = lax.broadcasted_iota(jnp.int32, q.shape, 1)
    zero = jnp.zeros_like(q)
    qa = jnp.where(lane < DA_HEAD_DIM, q, zero)
    qb = jnp.where(lane >= DA_HEAD_DIM, q, zero)
    ones = jnp.ones((_ONES_ROWS, blk), BF16)

    def update(s, m, acc, vt1):
        m_new = jnp.maximum(m, jnp.max(s, axis=0, keepdims=True))
        p = jnp.exp2(s - m_new).astype(BF16)
        acc_new = jnp.exp2(m - m_new) * acc + jnp.dot(vt1, p, preferred_element_type=F32)
        return m_new, acc_new

    def scores(c):
        kb = k_ref[pl.ds(pl.multiple_of(c * blk, blk), blk), :]
        return (lax.dot_general(kb, qa, _NT, preferred_element_type=F32),
                lax.dot_general(kb, qb, _NT, preferred_element_type=F32))

    def stage_scores(c, slot):
        s_sc[slot, 0], s_sc[slot, 1] = scores(c)

    def consume(c, s1, s2, carry):
        m1, a1, m2, a2 = carry
        vt1 = jnp.concatenate([vt_ref[:, pl.ds(pl.multiple_of(c * blk, blk), blk)], ones], axis=0)
        m1, a1 = update(s1, m1, a1, vt1)
        m2, a2 = update(s2, m2, a2, vt1)
        return m1, a1, m2, a2

    def consume_diagonal(slot, carry):
        key = lax.broadcasted_iota(jnp.int32, (blk, blk), 0)
        qry = lax.broadcasted_iota(jnp.int32, (blk, blk), 1)
        return consume(qi, jnp.where(key <= qry, s_sc[slot, 0], NEG_INF),
                       jnp.where(key <= qry, s_sc[slot, 1], NEG_INF), carry)

    stage_scores(0, 0)

    def chunk_pair(j, c):
        stage_scores(2 * j + 1, 1)
        c = consume(2 * j, s_sc[0, 0], s_sc[0, 1], c)
        stage_scores(2 * j + 2, 0)
        return consume(2 * j + 1, s_sc[1, 0], s_sc[1, 1], c)

    m0 = jnp.full((1, blk), NEG_INF, F32)
    a0 = jnp.zeros((DA_HEAD_W + _ONES_ROWS, blk), F32)
    carry = lax.fori_loop(0, qi // 2, chunk_pair, (m0, a0, m0, a0))

    def tail_odd(c):
        stage_scores(qi, 1)
        c = consume(qi - 1, s_sc[0, 0], s_sc[0, 1], c)
        return consume_diagonal(1, c)

    carry = lax.cond(qi % 2 == 1, tail_odd, lambda c: consume_diagonal(0, c), carry)
    _, a1, _, a2 = carry
    lam = _lambda(lq1_ref, lk1_ref, lq2_ref, lk2_ref, lam_init)
    o1 = a1[:DA_HEAD_W] * (1.0 / a1[DA_HEAD_W:DA_HEAD_W + 1])
    o2 = a2[:DA_HEAD_W] * (1.0 / a2[DA_HEAD_W:DA_HEAD_W + 1])
    o = o1 - lam * o2
    y = o * lax.rsqrt(jnp.mean(o * o, axis=0, keepdims=True) + EPS) * swc_ref[...]
    o_ref[...] = (y * (1.0 - lam_init)).T


def _attn_prompt(q, kb, vt, lq1, lk1, lq2, lk2, sw, batch, seq, lam_init):
    n, da_w = q.shape
    heads = da_w // DA_HEAD_W
    blk = _row_tile(seq, 512)
    nq = seq // blk
    swc = sw.reshape(-1, 1)
    small = lambda a: pl.BlockSpec(a.shape, lambda b, h, i: (0, 0))
    return pl.pallas_call(
        functools.partial(_attn_prompt_kernel, blk=blk, lam_init=lam_init),
        grid=(batch, heads, nq),
        in_specs=[
            pl.BlockSpec((blk, DA_HEAD_W), lambda b, h, i: (b * nq + i, h)),
            pl.BlockSpec((seq, DA_HEAD_W), lambda b, h, i: (b, h)),
            pl.BlockSpec((DA_HEAD_W, seq), lambda b, h, i: (h, b)),
            small(lq1), small(lk1), small(lq2), small(lk2), small(swc),
        ],
        out_specs=pl.BlockSpec((blk, DA_HEAD_W), lambda b, h, i: (b * nq + i, h)),
        out_shape=jax.ShapeDtypeStruct((n, da_w), F32),
        scratch_shapes=[pltpu.VMEM((2, 2, blk, blk), F32)],
        compiler_params=_params("parallel", "parallel", "arbitrary"),
    )(q, kb, vt, lq1, lk1, lq2, lk2, swc)


def _attn_decode_kernel(pt_ref, q_ref, kn_ref, vn_ref, lq1_ref, lk1_ref, lq2_ref, lk2_ref,
                        sw_ref, ck_hbm, cv_hbm, o_ref, kbuf, vbuf, sem, m_sc, l_sc, acc_sc,
                        *, pages_per_step, heads, page, lam_init):
    g = pl.program_id(1)
    n_steps = pl.num_programs(1)
    step = pl.program_id(0) * n_steps + g
    slot = step % 2

    def page_copies(b, gg, sl):
        copies = []
        for j in range(pages_per_step):
            pid = pt_ref[b, gg * pages_per_step + j]
            copies.append(pltpu.make_async_copy(ck_hbm.at[pid], kbuf.at[sl, j], sem.at[0, sl]))
            copies.append(pltpu.make_async_copy(cv_hbm.at[pid], vbuf.at[sl, j], sem.at[1, sl]))
        return copies

    @pl.when(step == 0)
    def _first_fetch():
        for cp in page_copies(0, 0, 0):
            cp.start()

    @pl.when(step + 1 < pl.num_programs(0) * n_steps)
    def _prefetch_next():
        nxt = step + 1
        for cp in page_copies(nxt // n_steps, nxt % n_steps, 1 - slot):
            cp.start()

    for cp in page_copies(pl.program_id(0), g, slot):
        cp.wait()
    k_refs = [kbuf.at[slot, j] for j in range(pages_per_step)]
    v_refs = [vbuf.at[slot, j] for j in range(pages_per_step)]
    q = q_ref[...]
    t_new = q.shape[0]
    lane = lax.broadcasted_iota(jnp.int32, (t_new, DA_HEAD_W), 1)
    head_cols = [slice(h * DA_HEAD_W, (h + 1) * DA_HEAD_W) for h in range(heads)]
    q2 = [jnp.concatenate([jnp.where(lane < DA_HEAD_DIM, q[:, hs], 0.0),
                           jnp.where(lane >= DA_HEAD_DIM, q[:, hs], 0.0)], axis=0).astype(BF16)
          for hs in head_cols]

    @pl.when(g == 0)
    def _new_tokens():
        pad = jnp.zeros((LANES - t_new, DA_HEAD_W), F32)
        for h, hs in enumerate(head_cols):
            kn = jnp.concatenate([kn_ref[:, hs], pad], axis=0).astype(BF16)
            vn = jnp.concatenate([vn_ref[:, hs], pad], axis=0).astype(BF16)
            s = lax.dot_general(q2[h], kn, _NT, preferred_element_type=F32)
            r = lax.broadcasted_iota(jnp.int32, s.shape, 0)
            c = lax.broadcasted_iota(jnp.int32, s.shape, 1)
            tok = jnp.where(r >= t_new, r - t_new, r)
            s = jnp.where(c <= tok, s, NEG_INF)
            m = jnp.max(s, axis=-1, keepdims=True)
            p = jnp.exp2(s - m)
            m_sc[h] = m
            l_sc[h] = jnp.sum(p, axis=-1, keepdims=True)
            acc_sc[h] = jnp.dot(p.astype(BF16), vn, preferred_element_type=F32)

    for h in range(heads):
        kb = jnp.concatenate([r_[pl.ds(h, page, stride=heads), :] for r_ in k_refs],
                             axis=0).astype(BF16)
        vb = jnp.concatenate([r_[pl.ds(h, page, stride=heads), :] for r_ in v_refs],
                             axis=0).astype(BF16)
        s = lax.dot_general(q2[h], kb, _NT, preferred_element_type=F32)
        m, l, acc = _softmax_update(s, m_sc[h], l_sc[h], acc_sc[h], vb)
        m_sc[h] = m
        l_sc[h] = l
        acc_sc[h] = acc

    @pl.when(g == pl.num_programs(1) - 1)
    def _finish():
        lam = _lambda(lq1_ref, lk1_ref, lq2_ref, lk2_ref, lam_init)
        sw = sw_ref[...]
        for h in range(heads):
            on = acc_sc[h] * (1.0 / l_sc[h])
            o = on[:t_new] - lam * on[t_new:]
            o_ref[:, h * DA_HEAD_W:(h + 1) * DA_HEAD_W] = _sub_ln(o, sw, lam_init)


def _attn_decode(q3, kn3, vn3, cache_k3, cache_v3, page_table, lq1, lk1, lq2, lk2, sw,
                 heads, lam_init):
    nb, t_new, da_w = q3.shape
    n_pages = page_table.shape[1]
    page = cache_k3.shape[1] // heads
    pps = math.gcd(n_pages, 16)
    n_steps = n_pages // pps
    tok = pl.BlockSpec((None, t_new, da_w), lambda b, g, pt: (b, 0, 0))
    small = lambda a: pl.BlockSpec(a.shape, lambda b, g, pt: (0, 0))
    hbm = pl.BlockSpec(memory_space=pl.ANY)
    page_buf = pltpu.VMEM((2, pps, page * heads, DA_HEAD_W), cache_k3.dtype)

    grid_spec = pltpu.PrefetchScalarGridSpec(
        num_scalar_prefetch=1,
        grid=(nb, n_steps),
        in_specs=[tok, tok, tok, small(lq1), small(lk1), small(lq2), small(lk2), small(sw), hbm, hbm],
        out_specs=tok,
        scratch_shapes=[page_buf, page_buf,
                        pltpu.SemaphoreType.DMA((2, 2)),
                        pltpu.VMEM((heads, 2 * t_new, 1), F32),
                        pltpu.VMEM((heads, 2 * t_new, 1), F32),
                        pltpu.VMEM((heads, 2 * t_new, DA_HEAD_W), F32)],
    )
    return pl.pallas_call(
        functools.partial(_attn_decode_kernel, pages_per_step=pps, heads=heads, page=page,
                          lam_init=lam_init),
        grid_spec=grid_spec,
        out_shape=jax.ShapeDtypeStruct((nb, t_new, da_w), F32),
        compiler_params=_params("arbitrary", "arbitrary"),
    )(page_table, q3, kn3, vn3, lq1, lk1, lq2, lk2, sw, cache_k3, cache_v3)


def _hgrn_kernel(q_ref, f_ref, v_ref, g_ref, s0_ref, nw_ref, o_ref, s_out_ref, st_sc, b_sc,
                 *, heads, n_chunks):
    c_len = HG_CHUNK
    step = pl.program_id(1)

    @pl.when(step == 0)
    def _load_state():
        for h in range(heads):
            st_sc[h] = s0_ref[h].T

    r = lax.broadcasted_iota(jnp.int32, (c_len, c_len), 0)
    c = lax.broadcasted_iota(jnp.int32, (c_len, c_len), 1)
    tri = jnp.where(c <= r, 1.0, 0.0).astype(BF16)

    def midpoint_mask(lv):
        return ((r // lv) == (c // lv)) & ((r % lv) >= lv // 2) & ((c % lv) < lv // 2)

    level_masks = {lv: midpoint_mask(lv) for lv in (16, 32, 64, 128)}
    blk = _HG_FAST_BLOCK
    diag_mask = ((r // blk) == (c // blk)) & (c <= r)
    sub = lax.broadcasted_iota(jnp.int32, (c_len // 8, 8, 1), 1)
    nw = nw_ref[...]

    def cumsum_rows(x):
        x1 = x.astype(BF16)
        r1 = x - x1.astype(F32)
        x2 = r1.astype(BF16)
        x3 = (r1 - x2.astype(F32)).astype(BF16)
        return (jnp.dot(tri, x1, preferred_element_type=F32)
                + jnp.dot(tri, x2, preferred_element_type=F32)
                + jnp.dot(tri, x3, preferred_element_type=F32))

    def midpoint_terms(q, kk, b, levels):
        a_mat = jnp.zeros((c_len, c_len), F32)
        for lv in levels:
            bl = b.reshape(c_len // lv, lv, HG_DK)
            mid = bl[:, lv // 2 - 1:lv // 2, :]
            e = jnp.exp2(-jnp.abs(bl - mid)).reshape(c_len, HG_DK)
            a_lv = lax.dot_general((q * e).astype(BF16), (kk * e).astype(BF16), _NT,
                                   preferred_element_type=F32)
            a_mat = a_mat + jnp.where(level_masks[lv], a_lv, 0.0)
        return a_mat

    def intra_fast(q, kk, v, b, f):
        bl = b.reshape(c_len // blk, blk, HG_DK)
        lf0 = jnp.log2(f.reshape(c_len // blk, blk, HG_DK)[:, 0:1, :])
        d = (bl - (bl[:, 0:1, :] - lf0)).reshape(c_len, HG_DK)
        a_blk = lax.dot_general((q * jnp.exp2(d)).astype(BF16), (kk * jnp.exp2(-d)).astype(BF16),
                                _NT, preferred_element_type=F32)
        a_mat = jnp.where(diag_mask, a_blk, 0.0) + midpoint_terms(q, kk, b, (32, 64, 128))
        return jnp.dot(a_mat.astype(BF16), v.astype(BF16), preferred_element_type=F32)

    def intra_exact(q, kk, v, b, f):
        del f
        q3 = q.reshape(c_len // 8, 8, HG_DK)
        b3 = b.reshape(c_len // 8, 8, HG_DK)
        k3 = kk.reshape(c_len // 8, 8, HG_DK)
        v3 = v.reshape(c_len // 8, 8, HG_DV)
        o3 = jnp.zeros((c_len // 8, 8, HG_DV), F32)
        for s in range(8):
            dec = jnp.exp2(jnp.minimum(b3 - b3[:, s:s + 1, :], 0.0))
            a = jnp.sum(q3 * k3[:, s:s + 1, :] * dec, axis=-1, keepdims=True)
            o3 = o3 + jnp.where(sub >= s, a, 0.0) * v3[:, s:s + 1, :]
        a_mat = midpoint_terms(q, kk, b, (16, 32, 64, 128))
        return o3.reshape(c_len, HG_DV) + jnp.dot(a_mat.astype(BF16), v.astype(BF16),
                                                  preferred_element_type=F32)

    def head_cols(h):
        return slice(h * HG_DK, (h + 1) * HG_DK)

    def prefix(ci, growth):
        r0 = pl.multiple_of(ci * c_len, c_len)
        for h in range(heads):
            lf = jnp.log2(f_ref[pl.ds(r0, c_len), head_cols(h)])
            b = cumsum_rows(lf)
            b_sc[pl.ds(r0, c_len), head_cols(h)] = b
            bl = b.reshape(c_len // blk, blk, HG_DK)
            before = bl[:, 0:1, :] - lf.reshape(c_len // blk, blk, HG_DK)[:, 0:1, :]
            growth = jnp.maximum(growth, before - bl[:, blk - 1:blk, :])
        return growth

    growth = lax.fori_loop(0, n_chunks, prefix, jnp.zeros((c_len // blk, 1, HG_DK), F32))
    safe = jnp.max(growth) < _HG_FAST_LIMIT

    def run_chunks(intra):
        for ci in range(n_chunks):
            rows = pl.ds(ci * c_len, c_len)
            for h in range(heads):
                hs = head_cols(h)
                q = q_ref[rows, hs]
                f = f_ref[rows, hs]
                v = v_ref[rows, hs]
                b = b_sc[rows, hs]
                kk = 1.0 - f
                st = st_sc[h]
                o = intra(q, kk, v, b, f) + lax.dot_general(
                    (q * jnp.exp2(b)).astype(BF16), st.astype(BF16), _NT,
                    preferred_element_type=F32)
                b_end = b[c_len - 1:c_len, :]
                k_end = kk * jnp.exp2(b_end - b)
                st_sc[h] = st * jnp.exp2(b_end) + jnp.dot(
                    v.T.astype(BF16), k_end.astype(BF16), preferred_element_type=F32)
                y = o * lax.rsqrt(jnp.mean(o * o, axis=-1, keepdims=True) + EPS) * nw
                o_ref[rows, hs] = y * g_ref[rows, hs]

    pl.when(safe)(lambda: run_chunks(intra_fast))
    pl.when(jnp.logical_not(safe))(lambda: run_chunks(intra_exact))

    @pl.when(step == pl.num_programs(1) - 1)
    def _store_state():
        for h in range(heads):
            s_out_ref[h] = st_sc[h].T


def _hgrn(hq, f, hi, hg, s0, nw, batch, seq):
    n, hg_w = hq.shape
    heads = hg_w // HG_DK
    tb = HG_CHUNK * math.gcd(seq // HG_CHUNK, 4)
    steps = seq // tb
    row = pl.BlockSpec((tb, hg_w), lambda b, t: (b * steps + t, 0))
    st = pl.BlockSpec((None, heads, HG_DK, HG_DV), lambda b, t: (b, 0, 0, 0))
    return pl.pallas_call(
        functools.partial(_hgrn_kernel, heads=heads, n_chunks=tb // HG_CHUNK),
        grid=(batch, steps),
        in_specs=[row, row, row, row, st, pl.BlockSpec(nw.shape, lambda b, t: (0, 0))],
        out_specs=[row, st],
        out_shape=[jax.ShapeDtypeStruct((n, hg_w), F32),
                   jax.ShapeDtypeStruct(s0.shape, F32)],
        scratch_shapes=[pltpu.VMEM((heads, HG_DV, HG_DK), F32), pltpu.VMEM((tb, hg_w), F32)],
        compiler_params=_params("parallel", "arbitrary"),
    )(hq, f, hi, hg, s0, nw)


_HG_SHORT = 8


def _hgrn_short_kernel(q_ref, f_ref, v_ref, g_ref, s0_ref, nw_ref, o_ref, s_out_ref, *, heads):
    t = _HG_SHORT
    row = lax.broadcasted_iota(jnp.int32, (t, 1), 0)
    pad8 = jnp.zeros((t, HG_DK), F32)
    pad = jnp.zeros((LANES - t, HG_DK), F32)
    nw = nw_ref[...]
    for h in range(heads):
        hs = slice(h * HG_DK, (h + 1) * HG_DK)
        q = q_ref[:, hs]
        f = f_ref[:, hs]
        v = v_ref[:, hs]
        kk = 1.0 - f
        lf = jnp.log2(f)
        b = jnp.zeros_like(lf)
        for s in range(t):
            b = b + jnp.where(row >= s, lf[s:s + 1, :], 0.0)
        o = jnp.zeros((t, HG_DV), F32)
        for s in range(t):
            dec = jnp.exp2(jnp.minimum(b - b[s:s + 1, :], 0.0))
            a = jnp.sum(q * kk[s:s + 1, :] * dec, axis=-1, keepdims=True)
            o = o + jnp.where(row >= s, a, 0.0) * v[s:s + 1, :]
        st = s0_ref[h].T
        q_dec = jnp.concatenate([q * jnp.exp2(b), pad8], axis=0).astype(BF16)
        o = o + lax.dot_general(q_dec, st.astype(BF16), _NT, preferred_element_type=F32)[:t]
        b_end = b[t - 1:t, :]
        k_end = jnp.concatenate([kk * jnp.exp2(b_end - b), pad], axis=0)
        v_pad = jnp.concatenate([v, pad], axis=0)
        st_new = st * jnp.exp2(b_end) + jnp.dot(v_pad.T.astype(BF16), k_end.astype(BF16),
                                                preferred_element_type=F32)
        s_out_ref[h] = st_new.T
        y = o * lax.rsqrt(jnp.mean(o * o, axis=-1, keepdims=True) + EPS) * nw
        o_ref[:, hs] = y * g_ref[:, hs]


def _hgrn_short(hq, f, hi, hg, s0, nw, batch):
    n, hg_w = hq.shape
    heads = hg_w // HG_DK
    r3 = lambda a: a.reshape(batch, _HG_SHORT, hg_w)
    tok = pl.BlockSpec((None, _HG_SHORT, hg_w), lambda b: (b, 0, 0))
    st = pl.BlockSpec((None, heads, HG_DK, HG_DV), lambda b: (b, 0, 0, 0))
    o, s_new = pl.pallas_call(
        functools.partial(_hgrn_short_kernel, heads=heads),
        grid=(batch,),
        in_specs=[tok, tok, tok, tok, st, pl.BlockSpec(nw.shape, lambda b: (0, 0))],
        out_specs=[tok, st],
        out_shape=[jax.ShapeDtypeStruct((batch, _HG_SHORT, hg_w), F32),
                   jax.ShapeDtypeStruct(s0.shape, F32)],
        compiler_params=_params("parallel"),
    )(r3(hq), r3(f), r3(hi), r3(hg), s0, nw)
    return o.reshape(n, hg_w), s_new


def _outproj_kernel(oda_ref, ohg_ref, x_ref, wo_ref, n2_ref, wrh_ref, wrl_ref, br_ref,
                    h_ref, xn_ref, gates_ref, *, da_w):
    o = (jnp.dot(oda_ref[...].astype(BF16), wo_ref[:da_w, :], preferred_element_type=F32)
         + jnp.dot(ohg_ref[...].astype(BF16), wo_ref[da_w:, :], preferred_element_type=F32))
    h = x_ref[...] + o
    h_ref[...] = h
    xn = h * lax.rsqrt(jnp.mean(h * h, axis=-1, keepdims=True) + EPS) * n2_ref[...]
    xh = xn.astype(BF16)
    xn_ref[...] = xh
    xl = (xn - xh.astype(F32)).astype(BF16)
    wrh = wrh_ref[...]
    logits = (jnp.dot(xh, wrh, preferred_element_type=F32)
              + jnp.dot(xl, wrh, preferred_element_type=F32)
              + jnp.dot(xh, wrl_ref[...], preferred_element_type=F32)) + br_ref[...]
    lane = lax.broadcasted_iota(jnp.int32, logits.shape, 1)
    lane_f = lane.astype(F32)
    big = float(LANES)
    is_g = (lane >= N_EXPERTS) & (lane < N_EXPERTS + N_GROUPS)
    gl = jnp.where(is_g, logits, NEG_INF)
    g_max = jnp.max(gl, axis=-1, keepdims=True)
    g_idx = jnp.min(jnp.where(gl == g_max, lane_f, big), axis=-1, keepdims=True) - N_EXPERTS
    g_w = 1.0 / jnp.sum(jnp.exp(gl - g_max), axis=-1, keepdims=True)
    grp_of_lane = (lane // EXPERTS_PER_GROUP).astype(F32)
    sel = (lane < N_EXPERTS) & (grp_of_lane == g_idx)
    el = jnp.where(sel, logits, NEG_INF)
    v1 = jnp.max(el, axis=-1, keepdims=True)
    i1 = jnp.min(jnp.where(sel, jnp.where(el == v1, lane_f, big), big), axis=-1, keepdims=True)
    el2 = jnp.where(lane_f == i1, NEG_INF, el)
    v2 = jnp.max(el2, axis=-1, keepdims=True)
    sel2 = sel & (lane_f != i1)
    i2 = jnp.min(jnp.where(sel2, jnp.where(el2 == v2, lane_f, big), big), axis=-1, keepdims=True)
    t = jnp.exp(v2 - v1)
    p1 = 1.0 / (1.0 + t)
    p2 = t * p1
    gates_ref[...] = (jnp.where(lane_f == i1, p1 * g_w, 0.0)
                      + jnp.where(lane_f == i2, p2 * g_w, 0.0))


def _outproj(oda, ohg, x2d, wo_bf, n2, wr_hi, wr_lo, br):
    n, d = x2d.shape
    da_w = oda.shape[1]
    tm = _row_tile(n, 512)
    row = lambda w: pl.BlockSpec((tm, w), lambda i: (i, 0))
    full = lambda a: pl.BlockSpec(a.shape, lambda i: (0, 0))
    return pl.pallas_call(
        functools.partial(_outproj_kernel, da_w=da_w),
        grid=(n // tm,),
        in_specs=[row(da_w), row(ohg.shape[1]), row(d), full(wo_bf), full(n2), full(wr_hi),
                  full(wr_lo), full(br)],
        out_specs=[row(d), row(d), row(LANES)],
        out_shape=[jax.ShapeDtypeStruct((n, d), F32), jax.ShapeDtypeStruct((n, d), BF16),
                   jax.ShapeDtypeStruct((n, LANES), F32)],
        compiler_params=_params("parallel"),
    )(oda, ohg, x2d, wo_bf, n2, wr_hi, wr_lo, br)


_MOE_EXPERTS_PER_STEP = 4


def _moe_kernel(x_ref, h_ref, gates_ref, wg_ref, wu_ref, wd_ref, y_ref):
    eb = pl.program_id(1)
    per_step = wg_ref.shape[0]

    @pl.when(eb == 0)
    def _init():
        y_ref[...] = h_ref[...]

    x = x_ref[...]
    gates = gates_ref[...]
    lane = lax.broadcasted_iota(jnp.int32, gates.shape, 1)
    hid = []
    for j in range(per_step):
        a = jnp.dot(x, wg_ref[j], preferred_element_type=F32)
        u = jnp.dot(x, wu_ref[j], preferred_element_type=F32)
        gate = jnp.sum(jnp.where(lane == eb * per_step + j, gates, 0.0), axis=-1, keepdims=True)
        hid.append(((a * _sigmoid(a)) * u * gate).astype(BF16))
    y_ref[...] += jnp.dot(jnp.concatenate(hid, axis=-1), wd_ref[...], preferred_element_type=F32)


def _moe(xn_bf, h, gates, wg_bf, wu_bf, wd_bf):
    n, d = h.shape
    n_exp, _, ff = wg_bf.shape
    per_step = math.gcd(n_exp, _MOE_EXPERTS_PER_STEP)
    tm = _row_tile(n, 1024)
    row = lambda w: pl.BlockSpec((tm, w), lambda i, e: (i, 0))
    return pl.pallas_call(
        _moe_kernel,
        grid=(n // tm, n_exp // per_step),
        in_specs=[row(d), row(d), row(LANES),
                  pl.BlockSpec((per_step, d, ff), lambda i, e: (e, 0, 0)),
                  pl.BlockSpec((per_step, d, ff), lambda i, e: (e, 0, 0)),
                  pl.BlockSpec((per_step * ff, d), lambda i, e: (e, 0))],
        out_specs=row(d),
        out_shape=jax.ShapeDtypeStruct((n, d), F32),
        compiler_params=_params("parallel", "arbitrary"),
    )(xn_bf, h, gates, wg_bf, wu_bf, wd_bf.reshape(n_exp * ff, d))


def _pad_tokens(a, batch, seq, seq_pad, value):
    a3 = a.reshape(batch, seq, a.shape[-1])
    a3 = jnp.pad(a3, ((0, 0), (0, seq_pad - seq), (0, 0)), constant_values=value)
    return a3.reshape(batch * seq_pad, a.shape[-1])


def _layer(x, s0, past, layer_idx, w):
    batch, seq, d = x.shape
    n = batch * seq
    x2d = x.reshape(n, d)
    da_w = w["qw"].shape[1]
    heads = da_w // DA_HEAD_W
    lam_init = 0.8 - 0.6 * math.exp(-0.3 * layer_idx)
    lam_args = (w["lq1"], w["lk1"], w["lq2"], w["lk2"], w["sw"])

    q, k, v, hq, f, hi, hg, *attn_in = _inproj(x2d, w["n1"], w["w_in"], w["qw"], w["kw"], w["lb"],
                                               w["gm"], past is None)
    if past is None:
        oda = _attn_prompt(q, *attn_in, *lam_args, batch, seq, lam_init)
    else:
        cache_k3, cache_v3, page_table = past
        r3 = lambda a: a.reshape(batch, seq, da_w)
        oda = _attn_decode(r3(q), r3(k), r3(v), cache_k3, cache_v3, page_table, *lam_args,
                           heads, lam_init).reshape(n, da_w)

    seq_pad = -(-seq // HG_CHUNK) * HG_CHUNK
    if seq == _HG_SHORT:
        ohg, s_new = _hgrn_short(hq, f, hi, hg, s0, w["hnw"], batch)
    elif seq_pad != seq:
        hq_p, hi_p, hg_p = (_pad_tokens(a, batch, seq, seq_pad, 0.0) for a in (hq, hi, hg))
        f_p = _pad_tokens(f, batch, seq, seq_pad, 1.0)
        ohg, s_new = _hgrn(hq_p, f_p, hi_p, hg_p, s0, w["hnw"], batch, seq_pad)
        ohg = ohg.reshape(batch, seq_pad, -1)[:, :seq].reshape(n, -1)
    else:
        ohg, s_new = _hgrn(hq, f, hi, hg, s0, w["hnw"], batch, seq)

    h, xn_bf, gates = _outproj(oda, ohg, x2d, w["w_out"], w["n2"], w["wr_hi"], w["wr_lo"], w["br"])
    y = _moe(xn_bf, h, gates, w["wg"], w["wu"], w["wd"])
    kv_shape = (batch, seq, heads, DA_HEAD_W)
    return y.reshape(batch, seq, d), k.reshape(kv_shape), v.reshape(kv_shape), s_new


def _layer_weights(l, lb_all, norm1_w, w_in, q_norm_w, k_norm_w, lambda_q1, lambda_k1, lambda_q2,
                   lambda_k2, subln_w, hg_norm_w, w_out, norm2_w, w_router_grp, b_router_grp,
                   w_router_exp, b_router_exp, w_gate, w_up, w_down):
    d = w_in.shape[1]
    da_w = w_out.shape[1] // 2
    n_grp_norm = da_w // DA_HEAD_DIM
    row = lambda a: a.reshape(1, -1).astype(F32)
    g_id = jnp.arange(da_w) // DA_HEAD_DIM
    gm = jnp.where(g_id[:, None] == g_id[None, :], 1.0 / DA_HEAD_DIM, 0.0).astype(BF16)
    wr = jnp.zeros((d, LANES), F32)
    wr = wr.at[:, :N_EXPERTS].set(w_router_exp[l]).at[:, N_EXPERTS:N_EXPERTS + N_GROUPS].set(
        w_router_grp[l])
    wr_hi = wr.astype(BF16)
    wr_lo = (wr - wr_hi.astype(F32)).astype(BF16)
    br = jnp.zeros((1, LANES), F32)
    br = br.at[0, :N_EXPERTS].set(b_router_exp[l]).at[0, N_EXPERTS:N_EXPERTS + N_GROUPS].set(
        b_router_grp[l])
    return dict(
        n1=row(norm1_w[l]), w_in=w_in[l].astype(BF16),
        qw=row(jnp.tile(q_norm_w[l], n_grp_norm)), kw=row(jnp.tile(k_norm_w[l], n_grp_norm)),
        lb=row(lb_all[l]), gm=gm,
        lq1=row(lambda_q1[l]), lk1=row(lambda_k1[l]), lq2=row(lambda_q2[l]), lk2=row(lambda_k2[l]),
        sw=row(subln_w[l]), hnw=row(hg_norm_w[l]),
        w_out=w_out[l].astype(BF16), n2=row(norm2_w[l]), wr_hi=wr_hi, wr_lo=wr_lo, br=br,
        wg=w_gate[l].astype(BF16), wu=w_up[l].astype(BF16), wd=w_down[l].astype(BF16),
    )


def kernel(x_prompt, x_sample, cache_k, cache_v, state_hgrn, page_table, norm1_w, w_in, q_norm_w, k_norm_w, lambda_q1, lambda_k1, lambda_q2, lambda_k2, subln_w, hg_lb, hg_norm_w, w_out, norm2_w, w_router_grp, b_router_grp, w_router_exp, b_router_exp, w_gate, w_up, w_down):
    depth = norm1_w.shape[0]
    n_phys, page, heads, head_w = cache_k.shape[1:]
    lb_all = jnp.cumsum(jax.nn.softmax(hg_lb.astype(F32), axis=0), axis=0)
    yp, ys = x_prompt, x_sample
    outs = [[] for _ in range(6)]
    for l in range(depth):
        w = _layer_weights(l, lb_all, norm1_w, w_in, q_norm_w, k_norm_w, lambda_q1, lambda_k1,
                           lambda_q2, lambda_k2, subln_w, hg_norm_w, w_out, norm2_w,
                           w_router_grp, b_router_grp, w_router_exp, b_router_exp,
                           w_gate, w_up, w_down)
        s0p = jnp.zeros((x_prompt.shape[0],) + state_hgrn.shape[2:], F32)
        yp, k_r, v_r, s_r = _layer(yp, s0p, None, l, w)
        past = (cache_k[l].reshape(n_phys, page * heads, head_w),
                cache_v[l].reshape(n_phys, page * heads, head_w), page_table)
        ys, k_s, v_s, s_s = _layer(ys, state_hgrn[l], past, l, w)
        for lst, val in zip(outs, (k_r, v_r, s_r, k_s, v_s, s_s)):
            lst.append(val)
    kp, vp, sp, kss, vss, sss = (jnp.stack(o) for o in outs)
    return (yp, ys, kp, vp, sp, kss, vss, sss)
```

```python
import functools
import math

import jax
import jax.numpy as jnp
from jax import lax
from jax.experimental import pallas as pl
from jax.experimental.pallas import tpu as pltpu

F32 = jnp.float32
BF16 = jnp.bfloat16

DA_HEAD_DIM = 64
DA_HEAD_W = 2 * DA_HEAD_DIM
HG_DK = 128
HG_DV = 128
HG_CHUNK = 128
_HG_FAST_BLOCK = 16
_HG_FAST_LIMIT = 60.0
N_GROUPS = 4
EXPERTS_PER_GROUP = 4
N_EXPERTS = N_GROUPS * EXPERTS_PER_GROUP
EPS = 1e-6
NEG_INF = -1e30
_Q_SCALE = DA_HEAD_DIM ** -0.5 * math.log2(math.e)
LANES = 128
VMEM_LIMIT = 56 * 1024 * 1024

_NT = (((1,), (1,)), ((), ()))


def _row_tile(n, pref):
    for t in range(min(pref, n), 7, -1):
        if n % t == 0 and t % 8 == 0:
            return t
    return n


def _sigmoid(x):
    return 1.0 / (1.0 + jnp.exp(-x))


def _params(*sem):
    return pltpu.CompilerParams(dimension_semantics=sem, vmem_limit_bytes=VMEM_LIMIT)


def _inproj_kernel(x_ref, n1_ref, w_ref, qw_ref, kw_ref, lb_ref, gm_ref,
                   q_ref, k_ref, v_ref, hq_ref, f_ref, hi_ref, hg_ref, *attn_refs, da_w, hg_w):
    x = x_ref[...]
    xn = x * lax.rsqrt(jnp.mean(x * x, axis=-1, keepdims=True) + EPS) * n1_ref[...]
    xb = xn.astype(BF16)

    def proj(lo, width):
        return jnp.dot(xb, w_ref[:, lo:lo + width], preferred_element_type=F32)

    gm = gm_ref[...]

    def group_rms(t, w):
        sq = t * t
        hi = sq.astype(BF16)
        lo = (sq - hi.astype(F32)).astype(BF16)
        ms = (jnp.dot(hi, gm, preferred_element_type=F32)
              + jnp.dot(lo, gm, preferred_element_type=F32))
        return t * lax.rsqrt(ms + EPS) * w

    q_ref[...] = (group_rms(proj(0, da_w), qw_ref[...]) * _Q_SCALE).astype(q_ref.dtype)
    k = group_rms(proj(da_w, da_w), kw_ref[...])
    v = proj(2 * da_w, da_w)
    if attn_refs:
        heads = da_w // DA_HEAD_W
        rows = k.shape[0]
        for h in range(heads):
            hs = slice(h * DA_HEAD_W, (h + 1) * DA_HEAD_W)
            k_ref[pl.ds(h, rows, stride=heads), :] = k[:, hs]
            v_ref[pl.ds(h, rows, stride=heads), :] = v[:, hs]
        kb_ref, vt_ref = attn_refs
        kb_ref[...] = k.astype(BF16)
        vt_ref[...] = v.T.astype(BF16)
    else:
        k_ref[...] = k
        v_ref[...] = v
    base = 3 * da_w
    hq = proj(base, hg_w)
    hq_ref[...] = hq * _sigmoid(hq)
    lb = lb_ref[...]
    f_ref[...] = lb + (1.0 - lb) * _sigmoid(proj(base + hg_w, hg_w))
    hi_ref[...] = proj(base + 2 * hg_w, hg_w)
    hg = proj(base + 3 * hg_w, hg_w)
    hg_ref[...] = hg * _sigmoid(hg)


def _inproj(x2d, n1, w_in_bf, qw, kw, lb, gm, for_prompt):
    n, d = x2d.shape
    da_w = qw.shape[1]
    hg_w = lb.shape[1]
    tm = _row_tile(n, 512)
    row = lambda w: pl.BlockSpec((tm, w), lambda i: (i, 0))
    full = lambda a: pl.BlockSpec(a.shape, lambda i: (0, 0))
    heads = da_w // DA_HEAD_W
    if for_prompt:
        kv_shape = jax.ShapeDtypeStruct((n * heads, DA_HEAD_W), F32)
        kv_spec = pl.BlockSpec((tm * heads, DA_HEAD_W), lambda i: (i, 0))
    else:
        kv_shape = jax.ShapeDtypeStruct((n, da_w), F32)
        kv_spec = row(da_w)
    out_shapes = [jax.ShapeDtypeStruct((n, da_w), BF16 if for_prompt else F32), kv_shape, kv_shape]
    out_shapes += [jax.ShapeDtypeStruct((n, hg_w), F32)] * 4
    out_specs = [row(da_w), kv_spec, kv_spec] + [row(hg_w)] * 4
    if for_prompt:
        out_shapes += [jax.ShapeDtypeStruct((n, da_w), BF16), jax.ShapeDtypeStruct((da_w, n), BF16)]
        out_specs += [row(da_w), pl.BlockSpec((da_w, tm), lambda i: (0, i))]
    return pl.pallas_call(
        functools.partial(_inproj_kernel, da_w=da_w, hg_w=hg_w),
        grid=(n // tm,),
        in_specs=[row(d), full(n1), full(w_in_bf), full(qw), full(kw), full(lb), full(gm)],
        out_specs=out_specs,
        out_shape=out_shapes,
        compiler_params=_params("parallel"),
    )(x2d, n1, w_in_bf, qw, kw, lb, gm)


def _lambda(lq1_ref, lk1_ref, lq2_ref, lk2_ref, lam_init):
    s1 = jnp.sum(lq1_ref[...] * lk1_ref[...], axis=-1, keepdims=True)
    s2 = jnp.sum(lq2_ref[...] * lk2_ref[...], axis=-1, keepdims=True)
    return jnp.exp(s1) - jnp.exp(s2) + lam_init


def _softmax_update(s, m, l, acc, vb):
    m_new = jnp.maximum(m, jnp.max(s, axis=-1, keepdims=True))
    p = jnp.exp2(s - m_new)
    alpha = jnp.exp2(m - m_new)
    l_new = alpha * l + jnp.sum(p, axis=-1, keepdims=True)
    acc_new = alpha * acc + jnp.dot(p.astype(BF16), vb, preferred_element_type=F32)
    return m_new, l_new, acc_new


def _sub_ln(o, sw, lam_init):
    y = o * lax.rsqrt(jnp.mean(o * o, axis=-1, keepdims=True) + EPS)
    return y * sw * (1.0 - lam_init)


_ONES_ROWS = 16


def _attn_prompt_kernel(q_ref, k_ref, vt_ref, lq1_ref, lk1_ref, lq2_ref, lk2_ref, swc_ref,
                        o_ref, s_sc, acc_sc, *, blk, lam_init):
    qi = pl.program_id(2)
    q = q_ref[...]
    lane = lax.broadcasted_iota(jnp.int32, q.shape, 1)
    zero = jnp.zeros_like(q)
    qa = jnp.where(lane < DA_HEAD_DIM, q, zero)
    qb = jnp.where(lane >= DA_HEAD_DIM, q, zero)
    ones = jnp.ones((_ONES_ROWS, blk), BF16)

    def update(s, m, mp, vt1):
        m_new = jnp.maximum(m, jnp.max(s, axis=0, keepdims=True))
        p = jnp.exp2(s - m_new)
        alpha = jnp.exp2(m.astype(F32) - m_new.astype(F32))
        acc_sc[mp] = alpha * acc_sc[mp] + jnp.dot(vt1, p, preferred_element_type=F32)
        return m_new

    def stage_scores(c, slot):
        kb = k_ref[pl.ds(pl.multiple_of(c * blk, blk), blk), :]
        s_sc[slot, 0] = lax.dot_general(kb, qa, _NT, preferred_element_type=F32).astype(BF16)
        s_sc[slot, 1] = lax.dot_general(kb, qb, _NT, preferred_element_type=F32).astype(BF16)

    def consume(c, s1, s2, carry):
        m1, m2 = carry
        vt1 = jnp.concatenate([vt_ref[:, pl.ds(pl.multiple_of(c * blk, blk), blk)], ones], axis=0)
        return update(s1, m1, 0, vt1), update(s2, m2, 1, vt1)

    def consume_diagonal(slot, carry):
        key = lax.broadcasted_iota(jnp.int32, (blk, blk), 0)
        qry = lax.broadcasted_iota(jnp.int32, (blk, blk), 1)
        neg = jnp.full((blk, blk), NEG_INF, BF16)
        return consume(qi, jnp.where(key <= qry, s_sc[slot, 0], neg),
                       jnp.where(key <= qry, s_sc[slot, 1], neg), carry)

    stage_scores(0, 0)

    def chunk_pair(j, c):
        stage_scores(2 * j + 1, 1)
        c = consume(2 * j, s_sc[0, 0], s_sc[0, 1], c)
        stage_scores(2 * j + 2, 0)
        return consume(2 * j + 1, s_sc[1, 0], s_sc[1, 1], c)

    m0 = jnp.full((1, blk), NEG_INF, BF16)
    acc_sc[...] = jnp.zeros(acc_sc.shape, F32)
    carry = lax.fori_loop(0, qi // 2, chunk_pair, (m0, m0))

    def tail_odd(c):
        stage_scores(qi, 1)
        c = consume(qi - 1, s_sc[0, 0], s_sc[0, 1], c)
        return consume_diagonal(1, c)

    lax.cond(qi % 2 == 1, tail_odd, lambda c: consume_diagonal(0, c), carry)
    a1 = acc_sc[0]
    a2 = acc_sc[1]
    lam = _lambda(lq1_ref, lk1_ref, lq2_ref, lk2_ref, lam_init)
    o1 = a1[:DA_HEAD_W] * (1.0 / a1[DA_HEAD_W:DA_HEAD_W + 1])
    o2 = a2[:DA_HEAD_W] * (1.0 / a2[DA_HEAD_W:DA_HEAD_W + 1])
    o = o1 - lam * o2
    y = o * lax.rsqrt(jnp.mean(o * o, axis=0, keepdims=True) + EPS) * swc_ref[...]
    o_ref[...] = (y * (1.0 - lam_init)).T


def _attn_prompt(q, kb, vt, lq1, lk1, lq2, lk2, sw, batch, seq, lam_init):
    n, da_w = q.shape
    heads = da_w // DA_HEAD_W
    blk = _row_tile(seq, 512)
    nq = seq // blk
    swc = sw.reshape(-1, 1)
    small = lambda a: pl.BlockSpec(a.shape, lambda b, h, i: (0, 0))
    return pl.pallas_call(
        functools.partial(_attn_prompt_kernel, blk=blk, lam_init=lam_init),
        grid=(batch, heads, nq),
        in_specs=[
            pl.BlockSpec((blk, DA_HEAD_W), lambda b, h, i: (b * nq + i, h)),
            pl.BlockSpec((seq, DA_HEAD_W), lambda b, h, i: (b, h)),
            pl.BlockSpec((DA_HEAD_W, seq), lambda b, h, i: (h, b)),
            small(lq1), small(lk1), small(lq2), small(lk2), small(swc),
        ],
        out_specs=pl.BlockSpec((blk, DA_HEAD_W), lambda b, h, i: (b * nq + i, h)),
        out_shape=jax.ShapeDtypeStruct((n, da_w), F32),
        scratch_shapes=[pltpu.VMEM((2, 2, blk, blk), BF16),
                        pltpu.VMEM((2, DA_HEAD_W + _ONES_ROWS, blk), F32)],
        compiler_params=_params("parallel", "parallel", "arbitrary"),
    )(q, kb, vt, lq1, lk1, lq2, lk2, swc)


def _attn_decode_kernel(pt_ref, q_ref, kn_ref, vn_ref, lq1_ref, lk1_ref, lq2_ref, lk2_ref,
                        sw_ref, ck_hbm, cv_hbm, o_ref, kbuf, vbuf, sem, m_sc, l_sc, acc_sc,
                        *, pages_per_step, heads, page, lam_init):
    g = pl.program_id(1)
    n_steps = pl.num_programs(1)
    step = pl.program_id(0) * n_steps + g
    slot = step % 2

    def page_copies(b, gg, sl):
        copies = []
        for j in range(pages_per_step):
            pid = pt_ref[b, gg * pages_per_step + j]
            copies.append((pltpu.make_async_copy(ck_hbm.at[pid], kbuf.at[sl, j], sem.at[0, sl]), 0))
            copies.append((pltpu.make_async_copy(cv_hbm.at[pid], vbuf.at[sl, j], sem.at[1, sl]), 1))
        return copies

    @pl.when(step == 0)
    def _first_fetch():
        for cp, prio in page_copies(0, 0, 0):
            cp.start(priority=prio)

    @pl.when(step + 1 < pl.num_programs(0) * n_steps)
    def _prefetch_next():
        nxt = step + 1
        for cp, prio in page_copies(nxt // n_steps, nxt % n_steps, 1 - slot):
            cp.start(priority=prio)

    for cp, _ in page_copies(pl.program_id(0), g, slot):
        cp.wait()
    k_refs = [kbuf.at[slot, j] for j in range(pages_per_step)]
    v_refs = [vbuf.at[slot, j] for j in range(pages_per_step)]
    q = q_ref[...]
    t_new = q.shape[0]
    lane = lax.broadcasted_iota(jnp.int32, (t_new, DA_HEAD_W), 1)
    head_cols = [slice(h * DA_HEAD_W, (h + 1) * DA_HEAD_W) for h in range(heads)]
    q2 = [jnp.concatenate([jnp.where(lane < DA_HEAD_DIM, q[:, hs], 0.0),
                           jnp.where(lane >= DA_HEAD_DIM, q[:, hs], 0.0)], axis=0).astype(BF16)
          for hs in head_cols]

    @pl.when(g == 0)
    def _new_tokens():
        pad = jnp.zeros((LANES - t_new, DA_HEAD_W), F32)
        for h, hs in enumerate(head_cols):
            kn = jnp.concatenate([kn_ref[:, hs], pad], axis=0).astype(BF16)
            vn = jnp.concatenate([vn_ref[:, hs], pad], axis=0).astype(BF16)
            s = lax.dot_general(q2[h], kn, _NT, preferred_element_type=F32)
            r = lax.broadcasted_iota(jnp.int32, s.shape, 0)
            c = lax.broadcasted_iota(jnp.int32, s.shape, 1)
            tok = jnp.where(r >= t_new, r - t_new, r)
            s = jnp.where(c <= tok, s, NEG_INF)
            m = jnp.max(s, axis=-1, keepdims=True)
            p = jnp.exp2(s - m)
            m_sc[h] = m
            l_sc[h] = jnp.sum(p, axis=-1, keepdims=True)
            acc_sc[h] = jnp.dot(p.astype(BF16), vn, preferred_element_type=F32)

    for h in range(heads):
        kb = jnp.concatenate([r_[pl.ds(h, page, stride=heads), :] for r_ in k_refs],
                             axis=0).astype(BF16)
        vb = jnp.concatenate([r_[pl.ds(h, page, stride=heads), :] for r_ in v_refs],
                             axis=0).astype(BF16)
        s = lax.dot_general(q2[h], kb, _NT, preferred_element_type=F32)
        m, l, acc = _softmax_update(s, m_sc[h], l_sc[h], acc_sc[h], vb)
        m_sc[h] = m
        l_sc[h] = l
        acc_sc[h] = acc

    @pl.when(g == pl.num_programs(1) - 1)
    def _finish():
        lam = _lambda(lq1_ref, lk1_ref, lq2_ref, lk2_ref, lam_init)
        sw = sw_ref[...]
        for h in range(heads):
            on = acc_sc[h] * (1.0 / l_sc[h])
            o = on[:t_new] - lam * on[t_new:]
            o_ref[:, h * DA_HEAD_W:(h + 1) * DA_HEAD_W] = _sub_ln(o, sw, lam_init)


def _attn_decode(q3, kn3, vn3, cache_k3, cache_v3, page_table, lq1, lk1, lq2, lk2, sw,
                 heads, lam_init):
    nb, t_new, da_w = q3.shape
    n_pages = page_table.shape[1]
    page = cache_k3.shape[1] // heads
    pps = math.gcd(n_pages, 16)
    n_steps = n_pages // pps
    tok = pl.BlockSpec((None, t_new, da_w), lambda b, g, pt: (b, 0, 0))
    small = lambda a: pl.BlockSpec(a.shape, lambda b, g, pt: (0, 0))
    hbm = pl.BlockSpec(memory_space=pl.ANY)
    page_buf = pltpu.VMEM((2, pps, page * heads, DA_HEAD_W), cache_k3.dtype)

    grid_spec = pltpu.PrefetchScalarGridSpec(
        num_scalar_prefetch=1,
        grid=(nb, n_steps),
        in_specs=[tok, tok, tok, small(lq1), small(lk1), small(lq2), small(lk2), small(sw), hbm, hbm],
        out_specs=tok,
        scratch_shapes=[page_buf, page_buf,
                        pltpu.SemaphoreType.DMA((2, 2)),
                        pltpu.VMEM((heads, 2 * t_new, 1), F32),
                        pltpu.VMEM((heads, 2 * t_new, 1), F32),
                        pltpu.VMEM((heads, 2 * t_new, DA_HEAD_W), F32)],
    )
    return pl.pallas_call(
        functools.partial(_attn_decode_kernel, pages_per_step=pps, heads=heads, page=page,
                          lam_init=lam_init),
        grid_spec=grid_spec,
        out_shape=jax.ShapeDtypeStruct((nb, t_new, da_w), F32),
        compiler_params=_params("arbitrary", "arbitrary"),
    )(page_table, q3, kn3, vn3, lq1, lk1, lq2, lk2, sw, cache_k3, cache_v3)


def _hgrn_kernel(q_ref, f_ref, v_ref, g_ref, s0_ref, nw_ref, o_ref, s_out_ref, st_sc, b_sc,
                 *, heads, n_chunks):
    c_len = HG_CHUNK
    step = pl.program_id(1)

    @pl.when(step == 0)
    def _load_state():
        for h in range(heads):
            st_sc[h] = s0_ref[h].T

    r = lax.broadcasted_iota(jnp.int32, (c_len, c_len), 0)
    c = lax.broadcasted_iota(jnp.int32, (c_len, c_len), 1)
    tri = jnp.where(c <= r, 1.0, 0.0).astype(BF16)

    def midpoint_mask(lv):
        return ((r // lv) == (c // lv)) & ((r % lv) >= lv // 2) & ((c % lv) < lv // 2)

    level_masks = {lv: midpoint_mask(lv) for lv in (16, 32, 64, 128)}
    blk = _HG_FAST_BLOCK
    diag_mask = ((r // blk) == (c // blk)) & (c <= r)
    sub = lax.broadcasted_iota(jnp.int32, (c_len // 8, 8, 1), 1)
    nw = nw_ref[...]

    def cumsum_rows(x):
        x1 = x.astype(BF16)
        r1 = x - x1.astype(F32)
        x2 = r1.astype(BF16)
        x3 = (r1 - x2.astype(F32)).astype(BF16)
        return (jnp.dot(tri, x1, preferred_element_type=F32)
                + jnp.dot(tri, x2, preferred_element_type=F32)
                + jnp.dot(tri, x3, preferred_element_type=F32))

    def midpoint_terms(q, kk, b, levels):
        a_mat = jnp.zeros((c_len, c_len), F32)
        for lv in levels:
            bl = b.reshape(c_len // lv, lv, HG_DK)
            mid = bl[:, lv // 2 - 1:lv // 2, :]
            e = jnp.exp2(-jnp.abs(bl - mid)).reshape(c_len, HG_DK)
            a_lv = lax.dot_general((q * e).astype(BF16), (kk * e).astype(BF16), _NT,
                                   preferred_element_type=F32)
            a_mat = a_mat + jnp.where(level_masks[lv], a_lv, 0.0)
        return a_mat

    def intra_fast(q, kk, v, b, f):
        bl = b.reshape(c_len // blk, blk, HG_DK)
        lf0 = jnp.log2(f.reshape(c_len // blk, blk, HG_DK)[:, 0:1, :])
        d = (bl - (bl[:, 0:1, :] - lf0)).reshape(c_len, HG_DK)
        a_blk = lax.dot_general((q * jnp.exp2(d)).astype(BF16), (kk * jnp.exp2(-d)).astype(BF16),
                                _NT, preferred_element_type=F32)
        a_mat = jnp.where(diag_mask, a_blk, 0.0) + midpoint_terms(q, kk, b, (32, 64, 128))
        return jnp.dot(a_mat.astype(BF16), v.astype(BF16), preferred_element_type=F32)

    def intra_exact(q, kk, v, b, f):
        del f
        q3 = q.reshape(c_len // 8, 8, HG_DK)
        b3 = b.reshape(c_len // 8, 8, HG_DK)
        k3 = kk.reshape(c_len // 8, 8, HG_DK)
        v3 = v.reshape(c_len // 8, 8, HG_DV)
        o3 = jnp.zeros((c_len // 8, 8, HG_DV), F32)
        for s in range(8):
            dec = jnp.exp2(jnp.minimum(b3 - b3[:, s:s + 1, :], 0.0))
            a = jnp.sum(q3 * k3[:, s:s + 1, :] * dec, axis=-1, keepdims=True)
            o3 = o3 + jnp.where(sub >= s, a, 0.0) * v3[:, s:s + 1, :]
        a_mat = midpoint_terms(q, kk, b, (16, 32, 64, 128))
        return o3.reshape(c_len, HG_DV) + jnp.dot(a_mat.astype(BF16), v.astype(BF16),
                                                  preferred_element_type=F32)

    def head_cols(h):
        return slice(h * HG_DK, (h + 1) * HG_DK)

    def prefix(ci, growth):
        r0 = pl.multiple_of(ci * c_len, c_len)
        for h in range(heads):
            lf = jnp.log2(f_ref[pl.ds(r0, c_len), head_cols(h)])
            b = cumsum_rows(lf)
            b_sc[pl.ds(r0, c_len), head_cols(h)] = b
            bl = b.reshape(c_len // blk, blk, HG_DK)
            before = bl[:, 0:1, :] - lf.reshape(c_len // blk, blk, HG_DK)[:, 0:1, :]
            growth = jnp.maximum(growth, before - bl[:, blk - 1:blk, :])
        return growth

    growth = lax.fori_loop(0, n_chunks, prefix, jnp.zeros((c_len // blk, 1, HG_DK), F32))
    safe = jnp.max(growth) < _HG_FAST_LIMIT

    def run_chunks(intra):
        for ci in range(n_chunks):
            rows = pl.ds(ci * c_len, c_len)
            for h in range(heads):
                hs = head_cols(h)
                q = q_ref[rows, hs]
                f = f_ref[rows, hs]
                v = v_ref[rows, hs]
                b = b_sc[rows, hs]
                kk = 1.0 - f
                st = st_sc[h]
                o = intra(q, kk, v, b, f) + lax.dot_general(
                    (q * jnp.exp2(b)).astype(BF16), st.astype(BF16), _NT,
                    preferred_element_type=F32)
                b_end = b[c_len - 1:c_len, :]
                k_end = kk * jnp.exp2(b_end - b)
                st_sc[h] = st * jnp.exp2(b_end) + jnp.dot(
                    v.T.astype(BF16), k_end.astype(BF16), preferred_element_type=F32)
                y = o * lax.rsqrt(jnp.mean(o * o, axis=-1, keepdims=True) + EPS) * nw
                o_ref[rows, hs] = y * g_ref[rows, hs]

    pl.when(safe)(lambda: run_chunks(intra_fast))
    pl.when(jnp.logical_not(safe))(lambda: run_chunks(intra_exact))

    @pl.when(step == pl.num_programs(1) - 1)
    def _store_state():
        for h in range(heads):
            s_out_ref[h] = st_sc[h].T


def _hgrn(hq, f, hi, hg, s0, nw, batch, seq):
    n, hg_w = hq.shape
    heads = hg_w // HG_DK
    tb = HG_CHUNK * math.gcd(seq // HG_CHUNK, 4)
    steps = seq // tb
    row = pl.BlockSpec((tb, hg_w), lambda b, t: (b * steps + t, 0))
    st = pl.BlockSpec((None, heads, HG_DK, HG_DV), lambda b, t: (b, 0, 0, 0))
    return pl.pallas_call(
        functools.partial(_hgrn_kernel, heads=heads, n_chunks=tb // HG_CHUNK),
        grid=(batch, steps),
        in_specs=[row, row, row, row, st, pl.BlockSpec(nw.shape, lambda b, t: (0, 0))],
        out_specs=[row, st],
        out_shape=[jax.ShapeDtypeStruct((n, hg_w), F32),
                   jax.ShapeDtypeStruct(s0.shape, F32)],
        scratch_shapes=[pltpu.VMEM((heads, HG_DV, HG_DK), F32), pltpu.VMEM((tb, hg_w), F32)],
        compiler_params=_params("parallel", "arbitrary"),
    )(hq, f, hi, hg, s0, nw)


_HG_SHORT = 8


def _hgrn_short_kernel(q_ref, f_ref, v_ref, g_ref, s0_ref, nw_ref, o_ref, s_out_ref, *, heads):
    t = _HG_SHORT
    row = lax.broadcasted_iota(jnp.int32, (t, 1), 0)
    pad8 = jnp.zeros((t, HG_DK), F32)
    pad = jnp.zeros((LANES - t, HG_DK), F32)
    nw = nw_ref[...]
    for h in range(heads):
        hs = slice(h * HG_DK, (h + 1) * HG_DK)
        q = q_ref[:, hs]
        f = f_ref[:, hs]
        v = v_ref[:, hs]
        kk = 1.0 - f
        lf = jnp.log2(f)
        b = jnp.zeros_like(lf)
        for s in range(t):
            b = b + jnp.where(row >= s, lf[s:s + 1, :], 0.0)
        o = jnp.zeros((t, HG_DV), F32)
        for s in range(t):
            dec = jnp.exp2(jnp.minimum(b - b[s:s + 1, :], 0.0))
            a = jnp.sum(q * kk[s:s + 1, :] * dec, axis=-1, keepdims=True)
            o = o + jnp.where(row >= s, a, 0.0) * v[s:s + 1, :]
        st = s0_ref[h].T
        q_dec = jnp.concatenate([q * jnp.exp2(b), pad8], axis=0).astype(BF16)
        o = o + lax.dot_general(q_dec, st.astype(BF16), _NT, preferred_element_type=F32)[:t]
        b_end = b[t - 1:t, :]
        k_end = jnp.concatenate([kk * jnp.exp2(b_end - b), pad], axis=0)
        v_pad = jnp.concatenate([v, pad], axis=0)
        st_new = st * jnp.exp2(b_end) + jnp.dot(v_pad.T.astype(BF16), k_end.astype(BF16),
                                                preferred_element_type=F32)
        s_out_ref[h] = st_new.T
        y = o * lax.rsqrt(jnp.mean(o * o, axis=-1, keepdims=True) + EPS) * nw
        o_ref[:, hs] = y * g_ref[:, hs]


def _hgrn_short(hq, f, hi, hg, s0, nw, batch):
    n, hg_w = hq.shape
    heads = hg_w // HG_DK
    r3 = lambda a: a.reshape(batch, _HG_SHORT, hg_w)
    tok = pl.BlockSpec((None, _HG_SHORT, hg_w), lambda b: (b, 0, 0))
    st = pl.BlockSpec((None, heads, HG_DK, HG_DV), lambda b: (b, 0, 0, 0))
    o, s_new = pl.pallas_call(
        functools.partial(_hgrn_short_kernel, heads=heads),
        grid=(batch,),
        in_specs=[tok, tok, tok, tok, st, pl.BlockSpec(nw.shape, lambda b: (0, 0))],
        out_specs=[tok, st],
        out_shape=[jax.ShapeDtypeStruct((batch, _HG_SHORT, hg_w), F32),
                   jax.ShapeDtypeStruct(s0.shape, F32)],
        compiler_params=_params("parallel"),
    )(r3(hq), r3(f), r3(hi), r3(hg), s0, nw)
    return o.reshape(n, hg_w), s_new


def _outproj_kernel(oda_ref, ohg_ref, x_ref, wo_ref, n2_ref, wrh_ref, wrl_ref, br_ref,
                    h_ref, xn_ref, gates_ref, *, da_w):
    o = (jnp.dot(oda_ref[...].astype(BF16), wo_ref[:da_w, :], preferred_element_type=F32)
         + jnp.dot(ohg_ref[...].astype(BF16), wo_ref[da_w:, :], preferred_element_type=F32))
    h = x_ref[...] + o
    h_ref[...] = h
    xn = h * lax.rsqrt(jnp.mean(h * h, axis=-1, keepdims=True) + EPS) * n2_ref[...]
    xh = xn.astype(BF16)
    xn_ref[...] = xh
    xl = (xn - xh.astype(F32)).astype(BF16)
    wrh = wrh_ref[...]
    logits = (jnp.dot(xh, wrh, preferred_element_type=F32)
              + jnp.dot(xl, wrh, preferred_element_type=F32)
              + jnp.dot(xh, wrl_ref[...], preferred_element_type=F32)) + br_ref[...]
    lane = lax.broadcasted_iota(jnp.int32, logits.shape, 1)
    lane_f = lane.astype(F32)
    big = float(LANES)
    is_g = (lane >= N_EXPERTS) & (lane < N_EXPERTS + N_GROUPS)
    gl = jnp.where(is_g, logits, NEG_INF)
    g_max = jnp.max(gl, axis=-1, keepdims=True)
    g_idx = jnp.min(jnp.where(gl == g_max, lane_f, big), axis=-1, keepdims=True) - N_EXPERTS
    g_w = 1.0 / jnp.sum(jnp.exp(gl - g_max), axis=-1, keepdims=True)
    grp_of_lane = (lane // EXPERTS_PER_GROUP).astype(F32)
    sel = (lane < N_EXPERTS) & (grp_of_lane == g_idx)
    el = jnp.where(sel, logits, NEG_INF)
    v1 = jnp.max(el, axis=-1, keepdims=True)
    i1 = jnp.min(jnp.where(sel, jnp.where(el == v1, lane_f, big), big), axis=-1, keepdims=True)
    el2 = jnp.where(lane_f == i1, NEG_INF, el)
    v2 = jnp.max(el2, axis=-1, keepdims=True)
    sel2 = sel & (lane_f != i1)
    i2 = jnp.min(jnp.where(sel2, jnp.where(el2 == v2, lane_f, big), big), axis=-1, keepdims=True)
    t = jnp.exp(v2 - v1)
    p1 = 1.0 / (1.0 + t)
    p2 = t * p1
    gates_ref[...] = (jnp.where(lane_f == i1, p1 * g_w, 0.0)
                      + jnp.where(lane_f == i2, p2 * g_w, 0.0))


def _outproj(oda, ohg, x2d, wo_bf, n2, wr_hi, wr_lo, br):
    n, d = x2d.shape
    da_w = oda.shape[1]
    tm = _row_tile(n, 512)
    row = lambda w: pl.BlockSpec((tm, w), lambda i: (i, 0))
    full = lambda a: pl.BlockSpec(a.shape, lambda i: (0, 0))
    return pl.pallas_call(
        functools.partial(_outproj_kernel, da_w=da_w),
        grid=(n // tm,),
        in_specs=[row(da_w), row(ohg.shape[1]), row(d), full(wo_bf), full(n2), full(wr_hi),
                  full(wr_lo), full(br)],
        out_specs=[row(d), row(d), row(LANES)],
        out_shape=[jax.ShapeDtypeStruct((n, d), F32), jax.ShapeDtypeStruct((n, d), BF16),
                   jax.ShapeDtypeStruct((n, LANES), F32)],
        compiler_params=_params("parallel"),
    )(oda, ohg, x2d, wo_bf, n2, wr_hi, wr_lo, br)


_MOE_EXPERTS_PER_STEP = 4


def _moe_kernel(x_ref, h_ref, gates_ref, wg_ref, wu_ref, wd_ref, y_ref):
    eb = pl.program_id(1)
    per_step = wg_ref.shape[0]

    @pl.when(eb == 0)
    def _init():
        y_ref[...] = h_ref[...]

    x = x_ref[...]
    gates = gates_ref[...]
    lane = lax.broadcasted_iota(jnp.int32, gates.shape, 1)
    hid = []
    for j in range(per_step):
        a = jnp.dot(x, wg_ref[j], preferred_element_type=F32)
        u = jnp.dot(x, wu_ref[j], preferred_element_type=F32)
        gate = jnp.sum(jnp.where(lane == eb * per_step + j, gates, 0.0), axis=-1, keepdims=True)
        hid.append(((a * _sigmoid(a)) * u * gate).astype(BF16))
    y_ref[...] += jnp.dot(jnp.concatenate(hid, axis=-1), wd_ref[...], preferred_element_type=F32)


def _moe(xn_bf, h, gates, wg_bf, wu_bf, wd_bf):
    n, d = h.shape
    n_exp, _, ff = wg_bf.shape
    per_step = math.gcd(n_exp, _MOE_EXPERTS_PER_STEP)
    tm = _row_tile(n, 1024)
    row = lambda w: pl.BlockSpec((tm, w), lambda i, e: (i, 0))
    return pl.pallas_call(
        _moe_kernel,
        grid=(n // tm, n_exp // per_step),
        in_specs=[row(d), row(d), row(LANES),
                  pl.BlockSpec((per_step, d, ff), lambda i, e: (e, 0, 0)),
                  pl.BlockSpec((per_step, d, ff), lambda i, e: (e, 0, 0)),
                  pl.BlockSpec((per_step * ff, d), lambda i, e: (e, 0))],
        out_specs=row(d),
        out_shape=jax.ShapeDtypeStruct((n, d), F32),
        compiler_params=_params("parallel", "arbitrary"),
    )(xn_bf, h, gates, wg_bf, wu_bf, wd_bf.reshape(n_exp * ff, d))


def _pad_tokens(a, batch, seq, seq_pad, value):
    a3 = a.reshape(batch, seq, a.shape[-1])
    a3 = jnp.pad(a3, ((0, 0), (0, seq_pad - seq), (0, 0)), constant_values=value)
    return a3.reshape(batch * seq_pad, a.shape[-1])


def _layer(x, s0, past, layer_idx, w):
    batch, seq, d = x.shape
    n = batch * seq
    x2d = x.reshape(n, d)
    da_w = w["qw"].shape[1]
    heads = da_w // DA_HEAD_W
    lam_init = 0.8 - 0.6 * math.exp(-0.3 * layer_idx)
    lam_args = (w["lq1"], w["lk1"], w["lq2"], w["lk2"], w["sw"])

    q, k, v, hq, f, hi, hg, *attn_in = _inproj(x2d, w["n1"], w["w_in"], w["qw"], w["kw"], w["lb"],
                                               w["gm"], past is None)
    if past is None:
        oda = _attn_prompt(q, *attn_in, *lam_args, batch, seq, lam_init)
    else:
        cache_k3, cache_v3, page_table = past
        r3 = lambda a: a.reshape(batch, seq, da_w)
        oda = _attn_decode(r3(q), r3(k), r3(v), cache_k3, cache_v3, page_table, *lam_args,
                           heads, lam_init).reshape(n, da_w)

    seq_pad = -(-seq // HG_CHUNK) * HG_CHUNK
    if seq == _HG_SHORT:
        ohg, s_new = _hgrn_short(hq, f, hi, hg, s0, w["hnw"], batch)
    elif seq_pad != seq:
        hq_p, hi_p, hg_p = (_pad_tokens(a, batch, seq, seq_pad, 0.0) for a in (hq, hi, hg))
        f_p = _pad_tokens(f, batch, seq, seq_pad, 1.0)
        ohg, s_new = _hgrn(hq_p, f_p, hi_p, hg_p, s0, w["hnw"], batch, seq_pad)
        ohg = ohg.reshape(batch, seq_pad, -1)[:, :seq].reshape(n, -1)
    else:
        ohg, s_new = _hgrn(hq, f, hi, hg, s0, w["hnw"], batch, seq)

    h, xn_bf, gates = _outproj(oda, ohg, x2d, w["w_out"], w["n2"], w["wr_hi"], w["wr_lo"], w["br"])
    y = _moe(xn_bf, h, gates, w["wg"], w["wu"], w["wd"])
    kv_shape = (batch, seq, heads, DA_HEAD_W)
    return y.reshape(batch, seq, d), k.reshape(kv_shape), v.reshape(kv_shape), s_new


def _layer_weights(l, lb_all, norm1_w, w_in, q_norm_w, k_norm_w, lambda_q1, lambda_k1, lambda_q2,
                   lambda_k2, subln_w, hg_norm_w, w_out, norm2_w, w_router_grp, b_router_grp,
                   w_router_exp, b_router_exp, w_gate, w_up, w_down):
    d = w_in.shape[1]
    da_w = w_out.shape[1] // 2
    n_grp_norm = da_w // DA_HEAD_DIM
    row = lambda a: a.reshape(1, -1).astype(F32)
    g_id = jnp.arange(da_w) // DA_HEAD_DIM
    gm = jnp.where(g_id[:, None] == g_id[None, :], 1.0 / DA_HEAD_DIM, 0.0).astype(BF16)
    wr = jnp.zeros((d, LANES), F32)
    wr = wr.at[:, :N_EXPERTS].set(w_router_exp[l]).at[:, N_EXPERTS:N_EXPERTS + N_GROUPS].set(
        w_router_grp[l])
    wr_hi = wr.astype(BF16)
    wr_lo = (wr - wr_hi.astype(F32)).astype(BF16)
    br = jnp.zeros((1, LANES), F32)
    br = br.at[0, :N_EXPERTS].set(b_router_exp[l]).at[0, N_EXPERTS:N_EXPERTS + N_GROUPS].set(
        b_router_grp[l])
    return dict(
        n1=row(norm1_w[l]), w_in=w_in[l].astype(BF16),
        qw=row(jnp.tile(q_norm_w[l], n_grp_norm)), kw=row(jnp.tile(k_norm_w[l], n_grp_norm)),
        lb=row(lb_all[l]), gm=gm,
        lq1=row(lambda_q1[l]), lk1=row(lambda_k1[l]), lq2=row(lambda_q2[l]), lk2=row(lambda_k2[l]),
        sw=row(subln_w[l]), hnw=row(hg_norm_w[l]),
        w_out=w_out[l].astype(BF16), n2=row(norm2_w[l]), wr_hi=wr_hi, wr_lo=wr_lo, br=br,
        wg=w_gate[l].astype(BF16), wu=w_up[l].astype(BF16), wd=w_down[l].astype(BF16),
    )


def kernel(x_prompt, x_sample, cache_k, cache_v, state_hgrn, page_table, norm1_w, w_in, q_norm_w, k_norm_w, lambda_q1, lambda_k1, lambda_q2, lambda_k2, subln_w, hg_lb, hg_norm_w, w_out, norm2_w, w_router_grp, b_router_grp, w_router_exp, b_router_exp, w_gate, w_up, w_down):
    depth = norm1_w.shape[0]
    n_phys, page, heads, head_w = cache_k.shape[1:]
    lb_all = jnp.cumsum(jax.nn.softmax(hg_lb.astype(F32), axis=0), axis=0)
    yp, ys = x_prompt, x_sample
    outs = [[] for _ in range(6)]
    for l in range(depth):
        w = _layer_weights(l, lb_all, norm1_w, w_in, q_norm_w, k_norm_w, lambda_q1, lambda_k1,
                           lambda_q2, lambda_k2, subln_w, hg_norm_w, w_out, norm2_w,
                           w_router_grp, b_router_grp, w_router_exp, b_router_exp,
                           w_gate, w_up, w_down)
        s0p = jnp.zeros((x_prompt.shape[0],) + state_hgrn.shape[2:], F32)
        yp, k_r, v_r, s_r = _layer(yp, s0p, None, l, w)
        past = (cache_k[l].reshape(n_phys, page * heads, head_w),
                cache_v[l].reshape(n_phys, page * heads, head_w), page_table)
        ys, k_s, v_s, s_s = _layer(ys, state_hgrn[l], past, l, w)
        for lst, val in zip(outs, (k_r, v_r, s_r, k_s, v_s, s_s)):
            lst.append(val)
    kp, vp, sp, kss, vss, sss = (jnp.stack(o) for o in outs)
    return (yp, ys, kp, vp, sp, kss, vss, sss)
```

```python
import functools
import math

import jax
import jax.numpy as jnp
from jax import lax
from jax.experimental import pallas as pl
from jax.experimental.pallas import tpu as pltpu

F32 = jnp.float32
BF16 = jnp.bfloat16

DA_HEAD_DIM = 64
DA_HEAD_W = 2 * DA_HEAD_DIM
HG_DK = 128
HG_DV = 128
HG_CHUNK = 128
_HG_FAST_BLOCK = 16
_HG_FAST_LIMIT = 60.0
N_GROUPS = 4
EXPERTS_PER_GROUP = 4
N_EXPERTS = N_GROUPS * EXPERTS_PER_GROUP
EPS = 1e-6
NEG_INF = -1e30
_Q_SCALE = DA_HEAD_DIM ** -0.5 * math.log2(math.e)
LANES = 128
VMEM_LIMIT = 56 * 1024 * 1024

_NT = (((1,), (1,)), ((), ()))


def _row_tile(n, pref):
    for t in range(min(pref, n), 7, -1):
        if n % t == 0 and t % 8 == 0:
            return t
    return n


def _sigmoid(x):
    return 1.0 / (1.0 + jnp.exp(-x))


def _params(*sem):
    return pltpu.CompilerParams(dimension_semantics=sem, vmem_limit_bytes=VMEM_LIMIT)


def _inproj_kernel(x_ref, n1_ref, w_ref, qw_ref, kw_ref, lb_ref, gm_ref,
                   q_ref, k_ref, v_ref, hq_ref, f_ref, hi_ref, hg_ref, *attn_refs, da_w, hg_w):
    x = x_ref[...]
    xn = x * lax.rsqrt(jnp.mean(x * x, axis=-1, keepdims=True) + EPS) * n1_ref[...]
    xb = xn.astype(BF16)

    def proj(lo, width):
        return jnp.dot(xb, w_ref[:, lo:lo + width], preferred_element_type=F32)

    gm = gm_ref[...]

    def group_rms(t, w):
        sq = t * t
        hi = sq.astype(BF16)
        lo = (sq - hi.astype(F32)).astype(BF16)
        ms = (jnp.dot(hi, gm, preferred_element_type=F32)
              + jnp.dot(lo, gm, preferred_element_type=F32))
        return t * lax.rsqrt(ms + EPS) * w

    q_ref[...] = (group_rms(proj(0, da_w), qw_ref[...]) * _Q_SCALE).astype(q_ref.dtype)
    k = group_rms(proj(da_w, da_w), kw_ref[...])
    v = proj(2 * da_w, da_w)
    if attn_refs:
        heads = da_w // DA_HEAD_W
        rows = k.shape[0]
        for h in range(heads):
            hs = slice(h * DA_HEAD_W, (h + 1) * DA_HEAD_W)
            k_ref[pl.ds(h, rows, stride=heads), :] = k[:, hs]
            v_ref[pl.ds(h, rows, stride=heads), :] = v[:, hs]
        kb_ref, vt_ref = attn_refs
        kb_ref[...] = k.astype(BF16)
        vt_ref[...] = v.T.astype(BF16)
    else:
        k_ref[...] = k
        v_ref[...] = v
    base = 3 * da_w
    hq = proj(base, hg_w)
    hq_ref[...] = hq * _sigmoid(hq)
    lb = lb_ref[...]
    f_ref[...] = lb + (1.0 - lb) * _sigmoid(proj(base + hg_w, hg_w))
    hi_ref[...] = proj(base + 2 * hg_w, hg_w)
    hg = proj(base + 3 * hg_w, hg_w)
    hg_ref[...] = hg * _sigmoid(hg)


def _inproj(x2d, n1, w_in_bf, qw, kw, lb, gm, for_prompt):
    n, d = x2d.shape
    da_w = qw.shape[1]
    hg_w = lb.shape[1]
    tm = _row_tile(n, 512)
    row = lambda w: pl.BlockSpec((tm, w), lambda i: (i, 0))
    full = lambda a: pl.BlockSpec(a.shape, lambda i: (0, 0))
    heads = da_w // DA_HEAD_W
    if for_prompt:
        kv_shape = jax.ShapeDtypeStruct((n * heads, DA_HEAD_W), F32)
        kv_spec = pl.BlockSpec((tm * heads, DA_HEAD_W), lambda i: (i, 0))
    else:
        kv_shape = jax.ShapeDtypeStruct((n, da_w), F32)
        kv_spec = row(da_w)
    out_shapes = [jax.ShapeDtypeStruct((n, da_w), BF16 if for_prompt else F32), kv_shape, kv_shape]
    out_shapes += [jax.ShapeDtypeStruct((n, hg_w), F32)] * 4
    out_specs = [row(da_w), kv_spec, kv_spec] + [row(hg_w)] * 4
    if for_prompt:
        out_shapes += [jax.ShapeDtypeStruct((n, da_w), BF16), jax.ShapeDtypeStruct((da_w, n), BF16)]
        out_specs += [row(da_w), pl.BlockSpec((da_w, tm), lambda i: (0, i))]
    return pl.pallas_call(
        functools.partial(_inproj_kernel, da_w=da_w, hg_w=hg_w),
        grid=(n // tm,),
        in_specs=[row(d), full(n1), full(w_in_bf), full(qw), full(kw), full(lb), full(gm)],
        out_specs=out_specs,
        out_shape=out_shapes,
        compiler_params=_params("parallel"),
    )(x2d, n1, w_in_bf, qw, kw, lb, gm)


def _lambda(lq1_ref, lk1_ref, lq2_ref, lk2_ref, lam_init):
    s1 = jnp.sum(lq1_ref[...] * lk1_ref[...], axis=-1, keepdims=True)
    s2 = jnp.sum(lq2_ref[...] * lk2_ref[...], axis=-1, keepdims=True)
    return jnp.exp(s1) - jnp.exp(s2) + lam_init


def _softmax_update(s, m, l, acc, vb):
    m_new = jnp.maximum(m, jnp.max(s, axis=-1, keepdims=True))
    p = jnp.exp2(s - m_new)
    alpha = jnp.exp2(m - m_new)
    l_new = alpha * l + jnp.sum(p, axis=-1, keepdims=True)
    acc_new = alpha * acc + jnp.dot(p.astype(BF16), vb, preferred_element_type=F32)
    return m_new, l_new, acc_new


def _sub_ln(o, sw, lam_init):
    y = o * lax.rsqrt(jnp.mean(o * o, axis=-1, keepdims=True) + EPS)
    return y * sw * (1.0 - lam_init)


_ONES_ROWS = 16


def _attn_prompt_kernel(q_ref, k_ref, vt_ref, lq1_ref, lk1_ref, lq2_ref, lk2_ref, swc_ref,
                        o_ref, s_sc, acc_sc, *, blk, lam_init):
    qi = pl.program_id(2)
    q = q_ref[...]
    lane = lax.broadcasted_iota(jnp.int32, q.shape, 1)
    zero = jnp.zeros_like(q)
    qa = jnp.where(lane < DA_HEAD_DIM, q, zero)
    qb = jnp.where(lane >= DA_HEAD_DIM, q, zero)
    ones = jnp.ones((_ONES_ROWS, blk), BF16)

    def update(s, m, mp, vt1):
        m_new = jnp.maximum(m, jnp.max(s, axis=0, keepdims=True))
        p = jnp.exp2(s - m_new)
        alpha = jnp.exp2(m.astype(F32) - m_new.astype(F32))
        acc_sc[mp] = alpha * acc_sc[mp] + jnp.dot(vt1, p, preferred_element_type=F32)
        return m_new

    def stage_scores(c, slot):
        kb = k_ref[pl.ds(pl.multiple_of(c * blk, blk), blk), :]
        s_sc[slot, 0] = lax.dot_general(kb, qa, _NT, preferred_element_type=F32).astype(BF16)
        s_sc[slot, 1] = lax.dot_general(kb, qb, _NT, preferred_element_type=F32).astype(BF16)

    def consume(c, s1, s2, carry):
        m1, m2 = carry
        vt1 = jnp.concatenate([vt_ref[:, pl.ds(pl.multiple_of(c * blk, blk), blk)], ones], axis=0)
        return update(s1, m1, 0, vt1), update(s2, m2, 1, vt1)

    def consume_diagonal(slot, carry):
        key = lax.broadcasted_iota(jnp.int32, (blk, blk), 0)
        qry = lax.broadcasted_iota(jnp.int32, (blk, blk), 1)
        neg = jnp.full((blk, blk), NEG_INF, BF16)
        return consume(qi, jnp.where(key <= qry, s_sc[slot, 0], neg),
                       jnp.where(key <= qry, s_sc[slot, 1], neg), carry)

    stage_scores(0, 0)

    def chunk_pair(j, c):
        stage_scores(2 * j + 1, 1)
        c = consume(2 * j, s_sc[0, 0], s_sc[0, 1], c)
        stage_scores(2 * j + 2, 0)
        return consume(2 * j + 1, s_sc[1, 0], s_sc[1, 1], c)

    m0 = jnp.full((1, blk), NEG_INF, BF16)
    acc_sc[...] = jnp.zeros(acc_sc.shape, F32)
    carry = lax.fori_loop(0, qi // 2, chunk_pair, (m0, m0))

    def tail_odd(c):
        stage_scores(qi, 1)
        c = consume(qi - 1, s_sc[0, 0], s_sc[0, 1], c)
        return consume_diagonal(1, c)

    lax.cond(qi % 2 == 1, tail_odd, lambda c: consume_diagonal(0, c), carry)
    a1 = acc_sc[0]
    a2 = acc_sc[1]
    lam = _lambda(lq1_ref, lk1_ref, lq2_ref, lk2_ref, lam_init)
    o1 = a1[:DA_HEAD_W] * (1.0 / a1[DA_HEAD_W:DA_HEAD_W + 1])
    o2 = a2[:DA_HEAD_W] * (1.0 / a2[DA_HEAD_W:DA_HEAD_W + 1])
    o = o1 - lam * o2
    y = o * lax.rsqrt(jnp.mean(o * o, axis=0, keepdims=True) + EPS) * swc_ref[...]
    o_ref[...] = (y * (1.0 - lam_init)).T


def _attn_prompt(q, kb, vt, lq1, lk1, lq2, lk2, sw, batch, seq, lam_init):
    n, da_w = q.shape
    heads = da_w // DA_HEAD_W
    blk = _row_tile(seq, 512)
    nq = seq // blk
    swc = sw.reshape(-1, 1)
    small = lambda a: pl.BlockSpec(a.shape, lambda b, h, i: (0, 0))
    return pl.pallas_call(
        functools.partial(_attn_prompt_kernel, blk=blk, lam_init=lam_init),
        grid=(batch, heads, nq),
        in_specs=[
            pl.BlockSpec((blk, DA_HEAD_W), lambda b, h, i: (b * nq + i, h)),
            pl.BlockSpec((seq, DA_HEAD_W), lambda b, h, i: (b, h)),
            pl.BlockSpec((DA_HEAD_W, seq), lambda b, h, i: (h, b)),
            small(lq1), small(lk1), small(lq2), small(lk2), small(swc),
        ],
        out_specs=pl.BlockSpec((blk, DA_HEAD_W), lambda b, h, i: (b * nq + i, h)),
        out_shape=jax.ShapeDtypeStruct((n, da_w), F32),
        scratch_shapes=[pltpu.VMEM((2, 2, blk, blk), BF16),
                        pltpu.VMEM((2, DA_HEAD_W + _ONES_ROWS, blk), F32)],
        compiler_params=_params("parallel", "parallel", "arbitrary"),
    )(q, kb, vt, lq1, lk1, lq2, lk2, swc)


_PAGE_SLOTS = 3


def _attn_decode_kernel(pt_ref, q_ref, kn_ref, vn_ref, lq1_ref, lk1_ref, lq2_ref, lk2_ref,
                        sw_ref, ck_hbm, cv_hbm, o_ref, kbuf, vbuf, sem, m_sc, l_sc, acc_sc,
                        *, pages_per_step, heads, page, lam_init):
    g = pl.program_id(1)
    n_steps = pl.num_programs(1)
    step = pl.program_id(0) * n_steps + g
    total = pl.num_programs(0) * n_steps
    slot = step % _PAGE_SLOTS

    def page_copies(b, gg, sl):
        copies = []
        for j in range(pages_per_step):
            pid = pt_ref[b, gg * pages_per_step + j]
            copies.append((pltpu.make_async_copy(ck_hbm.at[pid], kbuf.at[sl, j], sem.at[0, sl]), 0))
            copies.append((pltpu.make_async_copy(cv_hbm.at[pid], vbuf.at[sl, j], sem.at[1, sl]), 1))
        return copies

    def fetch(st):
        for cp, prio in page_copies(st // n_steps, st % n_steps, st % _PAGE_SLOTS):
            cp.start(priority=prio)

    @pl.when(step == 0)
    def _first_fetches():
        for ahead in range(_PAGE_SLOTS - 1):
            pl.when(ahead < total)(lambda ahead=ahead: fetch(ahead))

    @pl.when(step + (_PAGE_SLOTS - 1) < total)
    def _prefetch():
        fetch(step + (_PAGE_SLOTS - 1))

    for cp, _ in page_copies(pl.program_id(0), g, slot):
        cp.wait()
    k_refs = [kbuf.at[slot, j] for j in range(pages_per_step)]
    v_refs = [vbuf.at[slot, j] for j in range(pages_per_step)]
    q = q_ref[...]
    t_new = q.shape[0]
    lane = lax.broadcasted_iota(jnp.int32, (t_new, DA_HEAD_W), 1)
    head_cols = [slice(h * DA_HEAD_W, (h + 1) * DA_HEAD_W) for h in range(heads)]
    q2 = [jnp.concatenate([jnp.where(lane < DA_HEAD_DIM, q[:, hs], 0.0),
                           jnp.where(lane >= DA_HEAD_DIM, q[:, hs], 0.0)], axis=0).astype(BF16)
          for hs in head_cols]

    @pl.when(g == 0)
    def _new_tokens():
        pad = jnp.zeros((LANES - t_new, DA_HEAD_W), F32)
        for h, hs in enumerate(head_cols):
            kn = jnp.concatenate([kn_ref[:, hs], pad], axis=0).astype(BF16)
            vn = jnp.concatenate([vn_ref[:, hs], pad], axis=0).astype(BF16)
            s = lax.dot_general(q2[h], kn, _NT, preferred_element_type=F32)
            r = lax.broadcasted_iota(jnp.int32, s.shape, 0)
            c = lax.broadcasted_iota(jnp.int32, s.shape, 1)
            tok = jnp.where(r >= t_new, r - t_new, r)
            s = jnp.where(c <= tok, s, NEG_INF)
            m = jnp.max(s, axis=-1, keepdims=True)
            p = jnp.exp2(s - m)
            m_sc[h] = m
            l_sc[h] = jnp.sum(p, axis=-1, keepdims=True)
            acc_sc[h] = jnp.dot(p.astype(BF16), vn, preferred_element_type=F32)

    for h in range(heads):
        kb = jnp.concatenate([r_[pl.ds(h, page, stride=heads), :] for r_ in k_refs],
                             axis=0).astype(BF16)
        vb = jnp.concatenate([r_[pl.ds(h, page, stride=heads), :] for r_ in v_refs],
                             axis=0).astype(BF16)
        s = lax.dot_general(q2[h], kb, _NT, preferred_element_type=F32)
        m, l, acc = _softmax_update(s, m_sc[h], l_sc[h], acc_sc[h], vb)
        m_sc[h] = m
        l_sc[h] = l
        acc_sc[h] = acc

    @pl.when(g == pl.num_programs(1) - 1)
    def _finish():
        lam = _lambda(lq1_ref, lk1_ref, lq2_ref, lk2_ref, lam_init)
        sw = sw_ref[...]
        for h in range(heads):
            on = acc_sc[h] * (1.0 / l_sc[h])
            o = on[:t_new] - lam * on[t_new:]
            o_ref[:, h * DA_HEAD_W:(h + 1) * DA_HEAD_W] = _sub_ln(o, sw, lam_init)


def _attn_decode(q3, kn3, vn3, cache_k3, cache_v3, page_table, lq1, lk1, lq2, lk2, sw,
                 heads, lam_init):
    nb, t_new, da_w = q3.shape
    n_pages = page_table.shape[1]
    page = cache_k3.shape[1] // heads
    pps = math.gcd(n_pages, 16)
    n_steps = n_pages // pps
    tok = pl.BlockSpec((None, t_new, da_w), lambda b, g, pt: (b, 0, 0))
    small = lambda a: pl.BlockSpec(a.shape, lambda b, g, pt: (0, 0))
    hbm = pl.BlockSpec(memory_space=pl.ANY)
    page_buf = pltpu.VMEM((_PAGE_SLOTS, pps, page * heads, DA_HEAD_W), cache_k3.dtype)

    grid_spec = pltpu.PrefetchScalarGridSpec(
        num_scalar_prefetch=1,
        grid=(nb, n_steps),
        in_specs=[tok, tok, tok, small(lq1), small(lk1), small(lq2), small(lk2), small(sw), hbm, hbm],
        out_specs=tok,
        scratch_shapes=[page_buf, page_buf,
                        pltpu.SemaphoreType.DMA((2, _PAGE_SLOTS)),
                        pltpu.VMEM((heads, 2 * t_new, 1), F32),
                        pltpu.VMEM((heads, 2 * t_new, 1), F32),
                        pltpu.VMEM((heads, 2 * t_new, DA_HEAD_W), F32)],
    )
    return pl.pallas_call(
        functools.partial(_attn_decode_kernel, pages_per_step=pps, heads=heads, page=page,
                          lam_init=lam_init),
        grid_spec=grid_spec,
        out_shape=jax.ShapeDtypeStruct((nb, t_new, da_w), F32),
        compiler_params=_params("arbitrary", "arbitrary"),
    )(page_table, q3, kn3, vn3, lq1, lk1, lq2, lk2, sw, cache_k3, cache_v3)


def _hgrn_kernel(q_ref, f_ref, v_ref, g_ref, s0_ref, nw_ref, o_ref, s_out_ref, st_sc, b_sc,
                 *, heads, n_chunks):
    c_len = HG_CHUNK
    step = pl.program_id(1)

    @pl.when(step == 0)
    def _load_state():
        for h in range(heads):
            st_sc[h] = s0_ref[h].T

    r = lax.broadcasted_iota(jnp.int32, (c_len, c_len), 0)
    c = lax.broadcasted_iota(jnp.int32, (c_len, c_len), 1)
    tri = jnp.where(c <= r, 1.0, 0.0).astype(BF16)

    def midpoint_mask(lv):
        return ((r // lv) == (c // lv)) & ((r % lv) >= lv // 2) & ((c % lv) < lv // 2)

    level_masks = {lv: midpoint_mask(lv) for lv in (16, 32, 64, 128)}
    blk = _HG_FAST_BLOCK
    diag_mask = ((r // blk) == (c // blk)) & (c <= r)
    sub = lax.broadcasted_iota(jnp.int32, (c_len // 8, 8, 1), 1)
    nw = nw_ref[...]

    def cumsum_rows(x):
        x1 = x.astype(BF16)
        r1 = x - x1.astype(F32)
        x2 = r1.astype(BF16)
        x3 = (r1 - x2.astype(F32)).astype(BF16)
        return (jnp.dot(tri, x1, preferred_element_type=F32)
                + jnp.dot(tri, x2, preferred_element_type=F32)
                + jnp.dot(tri, x3, preferred_element_type=F32))

    def midpoint_terms(q, kk, b, levels):
        a_mat = jnp.zeros((c_len, c_len), F32)
        for lv in levels:
            bl = b.reshape(c_len // lv, lv, HG_DK)
            mid = bl[:, lv // 2 - 1:lv // 2, :]
            e = jnp.exp2(-jnp.abs(bl - mid)).reshape(c_len, HG_DK)
            a_lv = lax.dot_general((q * e).astype(BF16), (kk * e).astype(BF16), _NT,
                                   preferred_element_type=F32)
            a_mat = a_mat + jnp.where(level_masks[lv], a_lv, 0.0)
        return a_mat

    def intra_fast(q, kk, v, b, f):
        bl = b.reshape(c_len // blk, blk, HG_DK)
        lf0 = jnp.log2(f.reshape(c_len // blk, blk, HG_DK)[:, 0:1, :])
        d = (bl - (bl[:, 0:1, :] - lf0)).reshape(c_len, HG_DK)
        a_blk = lax.dot_general((q * jnp.exp2(d)).astype(BF16), (kk * jnp.exp2(-d)).astype(BF16),
                                _NT, preferred_element_type=F32)
        a_mat = jnp.where(diag_mask, a_blk, 0.0) + midpoint_terms(q, kk, b, (32, 64, 128))
        return jnp.dot(a_mat.astype(BF16), v.astype(BF16), preferred_element_type=F32)

    def intra_exact(q, kk, v, b, f):
        del f
        q3 = q.reshape(c_len // 8, 8, HG_DK)
        b3 = b.reshape(c_len // 8, 8, HG_DK)
        k3 = kk.reshape(c_len // 8, 8, HG_DK)
        v3 = v.reshape(c_len // 8, 8, HG_DV)
        o3 = jnp.zeros((c_len // 8, 8, HG_DV), F32)
        for s in range(8):
            dec = jnp.exp2(jnp.minimum(b3 - b3[:, s:s + 1, :], 0.0))
            a = jnp.sum(q3 * k3[:, s:s + 1, :] * dec, axis=-1, keepdims=True)
            o3 = o3 + jnp.where(sub >= s, a, 0.0) * v3[:, s:s + 1, :]
        a_mat = midpoint_terms(q, kk, b, (16, 32, 64, 128))
        return o3.reshape(c_len, HG_DV) + jnp.dot(a_mat.astype(BF16), v.astype(BF16),
                                                  preferred_element_type=F32)

    def head_cols(h):
        return slice(h * HG_DK, (h + 1) * HG_DK)

    def prefix(ci, growth):
        r0 = pl.multiple_of(ci * c_len, c_len)
        for h in range(heads):
            lf = jnp.log2(f_ref[pl.ds(r0, c_len), head_cols(h)])
            b = cumsum_rows(lf)
            b_sc[pl.ds(r0, c_len), head_cols(h)] = b
            bl = b.reshape(c_len // blk, blk, HG_DK)
            before = bl[:, 0:1, :] - lf.reshape(c_len // blk, blk, HG_DK)[:, 0:1, :]
            growth = jnp.maximum(growth, before - bl[:, blk - 1:blk, :])
        return growth

    growth = lax.fori_loop(0, n_chunks, prefix, jnp.zeros((c_len // blk, 1, HG_DK), F32))
    safe = jnp.max(growth) < _HG_FAST_LIMIT

    def run_chunks(intra):
        for ci in range(n_chunks):
            rows = pl.ds(ci * c_len, c_len)
            for h in range(heads):
                hs = head_cols(h)
                q = q_ref[rows, hs]
                f = f_ref[rows, hs]
                v = v_ref[rows, hs]
                b = b_sc[rows, hs]
                kk = 1.0 - f
                st = st_sc[h]
                o = intra(q, kk, v, b, f) + lax.dot_general(
                    (q * jnp.exp2(b)).astype(BF16), st.astype(BF16), _NT,
                    preferred_element_type=F32)
                b_end = b[c_len - 1:c_len, :]
                k_end = kk * jnp.exp2(b_end - b)
                st_sc[h] = st * jnp.exp2(b_end) + jnp.dot(
                    v.T.astype(BF16), k_end.astype(BF16), preferred_element_type=F32)
                y = o * lax.rsqrt(jnp.mean(o * o, axis=-1, keepdims=True) + EPS) * nw
                o_ref[rows, hs] = y * g_ref[rows, hs]

    pl.when(safe)(lambda: run_chunks(intra_fast))
    pl.when(jnp.logical_not(safe))(lambda: run_chunks(intra_exact))

    @pl.when(step == pl.num_programs(1) - 1)
    def _store_state():
        for h in range(heads):
            s_out_ref[h] = st_sc[h].T


def _hgrn(hq, f, hi, hg, s0, nw, batch, seq):
    n, hg_w = hq.shape
    heads = hg_w // HG_DK
    tb = HG_CHUNK * math.gcd(seq // HG_CHUNK, 4)
    steps = seq // tb
    row = pl.BlockSpec((tb, hg_w), lambda b, t: (b * steps + t, 0))
    st = pl.BlockSpec((None, heads, HG_DK, HG_DV), lambda b, t: (b, 0, 0, 0))
    return pl.pallas_call(
        functools.partial(_hgrn_kernel, heads=heads, n_chunks=tb // HG_CHUNK),
        grid=(batch, steps),
        in_specs=[row, row, row, row, st, pl.BlockSpec(nw.shape, lambda b, t: (0, 0))],
        out_specs=[row, st],
        out_shape=[jax.ShapeDtypeStruct((n, hg_w), F32),
                   jax.ShapeDtypeStruct(s0.shape, F32)],
        scratch_shapes=[pltpu.VMEM((heads, HG_DV, HG_DK), F32), pltpu.VMEM((tb, hg_w), F32)],
        compiler_params=_params("parallel", "arbitrary"),
    )(hq, f, hi, hg, s0, nw)


_HG_SHORT = 8


def _hgrn_short_kernel(q_ref, f_ref, v_ref, g_ref, s0_ref, nw_ref, o_ref, s_out_ref, *, heads):
    t = _HG_SHORT
    row = lax.broadcasted_iota(jnp.int32, (t, 1), 0)
    pad8 = jnp.zeros((t, HG_DK), F32)
    pad = jnp.zeros((LANES - t, HG_DK), F32)
    nw = nw_ref[...]
    for h in range(heads):
        hs = slice(h * HG_DK, (h + 1) * HG_DK)
        q = q_ref[:, hs]
        f = f_ref[:, hs]
        v = v_ref[:, hs]
        kk = 1.0 - f
        lf = jnp.log2(f)
        b = jnp.zeros_like(lf)
        for s in range(t):
            b = b + jnp.where(row >= s, lf[s:s + 1, :], 0.0)
        o = jnp.zeros((t, HG_DV), F32)
        for s in range(t):
            dec = jnp.exp2(jnp.minimum(b - b[s:s + 1, :], 0.0))
            a = jnp.sum(q * kk[s:s + 1, :] * dec, axis=-1, keepdims=True)
            o = o + jnp.where(row >= s, a, 0.0) * v[s:s + 1, :]
        st = s0_ref[h].T
        q_dec = jnp.concatenate([q * jnp.exp2(b), pad8], axis=0).astype(BF16)
        o = o + lax.dot_general(q_dec, st.astype(BF16), _NT, preferred_element_type=F32)[:t]
        b_end = b[t - 1:t, :]
        k_end = jnp.concatenate([kk * jnp.exp2(b_end - b), pad], axis=0)
        v_pad = jnp.concatenate([v, pad], axis=0)
        st_new = st * jnp.exp2(b_end) + jnp.dot(v_pad.T.astype(BF16), k_end.astype(BF16),
                                                preferred_element_type=F32)
        s_out_ref[h] = st_new.T
        y = o * lax.rsqrt(jnp.mean(o * o, axis=-1, keepdims=True) + EPS) * nw
        o_ref[:, hs] = y * g_ref[:, hs]


def _hgrn_short(hq, f, hi, hg, s0, nw, batch):
    n, hg_w = hq.shape
    heads = hg_w // HG_DK
    r3 = lambda a: a.reshape(batch, _HG_SHORT, hg_w)
    tok = pl.BlockSpec((None, _HG_SHORT, hg_w), lambda b: (b, 0, 0))
    st = pl.BlockSpec((None, heads, HG_DK, HG_DV), lambda b: (b, 0, 0, 0))
    o, s_new = pl.pallas_call(
        functools.partial(_hgrn_short_kernel, heads=heads),
        grid=(batch,),
        in_specs=[tok, tok, tok, tok, st, pl.BlockSpec(nw.shape, lambda b: (0, 0))],
        out_specs=[tok, st],
        out_shape=[jax.ShapeDtypeStruct((batch, _HG_SHORT, hg_w), F32),
                   jax.ShapeDtypeStruct(s0.shape, F32)],
        compiler_params=_params("parallel"),
    )(r3(hq), r3(f), r3(hi), r3(hg), s0, nw)
    return o.reshape(n, hg_w), s_new


def _outproj_kernel(oda_ref, ohg_ref, x_ref, wo_ref, n2_ref, wrh_ref, wrl_ref, br_ref,
                    h_ref, xn_ref, gates_ref, *, da_w):
    o = (jnp.dot(oda_ref[...].astype(BF16), wo_ref[:da_w, :], preferred_element_type=F32)
         + jnp.dot(ohg_ref[...].astype(BF16), wo_ref[da_w:, :], preferred_element_type=F32))
    h = x_ref[...] + o
    h_ref[...] = h
    xn = h * lax.rsqrt(jnp.mean(h * h, axis=-1, keepdims=True) + EPS) * n2_ref[...]
    xh = xn.astype(BF16)
    xn_ref[...] = xh
    xl = (xn - xh.astype(F32)).astype(BF16)
    wrh = wrh_ref[...]
    both = jnp.dot(xh, jnp.concatenate([wrh, wrl_ref[...]], axis=-1), preferred_element_type=F32)
    logits = (both[:, :LANES] + jnp.dot(xl, wrh, preferred_element_type=F32)
              + both[:, LANES:]) + br_ref[...]
    lane = lax.broadcasted_iota(jnp.int32, logits.shape, 1)
    lane_f = lane.astype(F32)
    big = float(LANES)
    is_g = (lane >= N_EXPERTS) & (lane < N_EXPERTS + N_GROUPS)
    gl = jnp.where(is_g, logits, NEG_INF)
    g_max = jnp.max(gl, axis=-1, keepdims=True)
    g_idx = jnp.min(jnp.where(gl == g_max, lane_f, big), axis=-1, keepdims=True) - N_EXPERTS
    g_w = 1.0 / jnp.sum(jnp.exp(gl - g_max), axis=-1, keepdims=True)
    grp_of_lane = (lane // EXPERTS_PER_GROUP).astype(F32)
    sel = (lane < N_EXPERTS) & (grp_of_lane == g_idx)
    el = jnp.where(sel, logits, NEG_INF)
    v1 = jnp.max(el, axis=-1, keepdims=True)
    i1 = jnp.min(jnp.where(sel, jnp.where(el == v1, lane_f, big), big), axis=-1, keepdims=True)
    el2 = jnp.where(lane_f == i1, NEG_INF, el)
    v2 = jnp.max(el2, axis=-1, keepdims=True)
    sel2 = sel & (lane_f != i1)
    i2 = jnp.min(jnp.where(sel2, jnp.where(el2 == v2, lane_f, big), big), axis=-1, keepdims=True)
    t = jnp.exp(v2 - v1)
    p1 = 1.0 / (1.0 + t)
    p2 = t * p1
    gates_ref[...] = (jnp.where(lane_f == i1, p1 * g_w, 0.0)
                      + jnp.where(lane_f == i2, p2 * g_w, 0.0))


def _outproj(oda, ohg, x2d, wo_bf, n2, wr_hi, wr_lo, br):
    n, d = x2d.shape
    da_w = oda.shape[1]
    tm = _row_tile(n, 512)
    row = lambda w: pl.BlockSpec((tm, w), lambda i: (i, 0))
    full = lambda a: pl.BlockSpec(a.shape, lambda i: (0, 0))
    return pl.pallas_call(
        functools.partial(_outproj_kernel, da_w=da_w),
        grid=(n // tm,),
        in_specs=[row(da_w), row(ohg.shape[1]), row(d), full(wo_bf), full(n2), full(wr_hi),
                  full(wr_lo), full(br)],
        out_specs=[row(d), row(d), row(LANES)],
        out_shape=[jax.ShapeDtypeStruct((n, d), F32), jax.ShapeDtypeStruct((n, d), BF16),
                   jax.ShapeDtypeStruct((n, LANES), F32)],
        compiler_params=_params("parallel"),
    )(oda, ohg, x2d, wo_bf, n2, wr_hi, wr_lo, br)


_MOE_EXPERTS_PER_STEP = 4


def _moe_kernel(x_ref, h_ref, gates_ref, wg_ref, wu_ref, wd_ref, y_ref):
    eb = pl.program_id(1)
    per_step = wg_ref.shape[0]

    @pl.when(eb == 0)
    def _init():
        y_ref[...] = h_ref[...]

    x = x_ref[...]
    gates = gates_ref[...]
    lane = lax.broadcasted_iota(jnp.int32, gates.shape, 1)
    hid = []
    for j in range(per_step):
        a = jnp.dot(x, wg_ref[j], preferred_element_type=F32)
        u = jnp.dot(x, wu_ref[j], preferred_element_type=F32)
        gate = jnp.sum(jnp.where(lane == eb * per_step + j, gates, 0.0), axis=-1, keepdims=True)
        hid.append(((a * _sigmoid(a)) * u * gate).astype(BF16))
    y_ref[...] += jnp.dot(jnp.concatenate(hid, axis=-1), wd_ref[...], preferred_element_type=F32)


def _moe(xn_bf, h, gates, wg_bf, wu_bf, wd_bf):
    n, d = h.shape
    n_exp, _, ff = wg_bf.shape
    per_step = math.gcd(n_exp, _MOE_EXPERTS_PER_STEP)
    tm = _row_tile(n, 1024)
    row = lambda w: pl.BlockSpec((tm, w), lambda i, e: (i, 0))
    return pl.pallas_call(
        _moe_kernel,
        grid=(n // tm, n_exp // per_step),
        in_specs=[row(d), row(d), row(LANES),
                  pl.BlockSpec((per_step, d, ff), lambda i, e: (e, 0, 0)),
                  pl.BlockSpec((per_step, d, ff), lambda i, e: (e, 0, 0)),
                  pl.BlockSpec((per_step * ff, d), lambda i, e: (e, 0))],
        out_specs=row(d),
        out_shape=jax.ShapeDtypeStruct((n, d), F32),
        compiler_params=_params("parallel", "arbitrary"),
    )(xn_bf, h, gates, wg_bf, wu_bf, wd_bf.reshape(n_exp * ff, d))


def _pad_tokens(a, batch, seq, seq_pad, value):
    a3 = a.reshape(batch, seq, a.shape[-1])
    a3 = jnp.pad(a3, ((0, 0), (0, seq_pad - seq), (0, 0)), constant_values=value)
    return a3.reshape(batch * seq_pad, a.shape[-1])


def _layer(x, s0, past, layer_idx, w):
    batch, seq, d = x.shape
    n = batch * seq
    x2d = x.reshape(n, d)
    da_w = w["qw"].shape[1]
    heads = da_w // DA_HEAD_W
    lam_init = 0.8 - 0.6 * math.exp(-0.3 * layer_idx)
    lam_args = (w["lq1"], w["lk1"], w["lq2"], w["lk2"], w["sw"])

    q, k, v, hq, f, hi, hg, *attn_in = _inproj(x2d, w["n1"], w["w_in"], w["qw"], w["kw"], w["lb"],
                                               w["gm"], past is None)
    if past is None:
        oda = _attn_prompt(q, *attn_in, *lam_args, batch, seq, lam_init)
    else:
        cache_k3, cache_v3, page_table = past
        r3 = lambda a: a.reshape(batch, seq, da_w)
        oda = _attn_decode(r3(q), r3(k), r3(v), cache_k3, cache_v3, page_table, *lam_args,
                           heads, lam_init).reshape(n, da_w)

    seq_pad = -(-seq // HG_CHUNK) * HG_CHUNK
    if seq == _HG_SHORT:
        ohg, s_new = _hgrn_short(hq, f, hi, hg, s0, w["hnw"], batch)
    elif seq_pad != seq:
        hq_p, hi_p, hg_p = (_pad_tokens(a, batch, seq, seq_pad, 0.0) for a in (hq, hi, hg))
        f_p = _pad_tokens(f, batch, seq, seq_pad, 1.0)
        ohg, s_new = _hgrn(hq_p, f_p, hi_p, hg_p, s0, w["hnw"], batch, seq_pad)
        ohg = ohg.reshape(batch, seq_pad, -1)[:, :seq].reshape(n, -1)
    else:
        ohg, s_new = _hgrn(hq, f, hi, hg, s0, w["hnw"], batch, seq)

    h, xn_bf, gates = _outproj(oda, ohg, x2d, w["w_out"], w["n2"], w["wr_hi"], w["wr_lo"], w["br"])
    y = _moe(xn_bf, h, gates, w["wg"], w["wu"], w["wd"])
    kv_shape = (batch, seq, heads, DA_HEAD_W)
    return y.reshape(batch, seq, d), k.reshape(kv_shape), v.reshape(kv_shape), s_new


def _layer_weights(l, lb_all, norm1_w, w_in, q_norm_w, k_norm_w, lambda_q1, lambda_k1, lambda_q2,
                   lambda_k2, subln_w, hg_norm_w, w_out, norm2_w, w_router_grp, b_router_grp,
                   w_router_exp, b_router_exp, w_gate, w_up, w_down):
    d = w_in.shape[1]
    da_w = w_out.shape[1] // 2
    n_grp_norm = da_w // DA_HEAD_DIM
    row = lambda a: a.reshape(1, -1).astype(F32)
    g_id = jnp.arange(da_w) // DA_HEAD_DIM
    gm = jnp.where(g_id[:, None] == g_id[None, :], 1.0 / DA_HEAD_DIM, 0.0).astype(BF16)
    wr = jnp.zeros((d, LANES), F32)
    wr = wr.at[:, :N_EXPERTS].set(w_router_exp[l]).at[:, N_EXPERTS:N_EXPERTS + N_GROUPS].set(
        w_router_grp[l])
    wr_hi = wr.astype(BF16)
    wr_lo = (wr - wr_hi.astype(F32)).astype(BF16)
    br = jnp.zeros((1, LANES), F32)
    br = br.at[0, :N_EXPERTS].set(b_router_exp[l]).at[0, N_EXPERTS:N_EXPERTS + N_GROUPS].set(
        b_router_grp[l])
    return dict(
        n1=row(norm1_w[l]), w_in=w_in[l].astype(BF16),
        qw=row(jnp.tile(q_norm_w[l], n_grp_norm)), kw=row(jnp.tile(k_norm_w[l], n_grp_norm)),
        lb=row(lb_all[l]), gm=gm,
        lq1=row(lambda_q1[l]), lk1=row(lambda_k1[l]), lq2=row(lambda_q2[l]), lk2=row(lambda_k2[l]),
        sw=row(subln_w[l]), hnw=row(hg_norm_w[l]),
        w_out=w_out[l].astype(BF16), n2=row(norm2_w[l]), wr_hi=wr_hi, wr_lo=wr_lo, br=br,
        wg=w_gate[l].astype(BF16), wu=w_up[l].astype(BF16), wd=w_down[l].astype(BF16),
    )


def kernel(x_prompt, x_sample, cache_k, cache_v, state_hgrn, page_table, norm1_w, w_in, q_norm_w, k_norm_w, lambda_q1, lambda_k1, lambda_q2, lambda_k2, subln_w, hg_lb, hg_norm_w, w_out, norm2_w, w_router_grp, b_router_grp, w_router_exp, b_router_exp, w_gate, w_up, w_down):
    depth = norm1_w.shape[0]
    n_phys, page, heads, head_w = cache_k.shape[1:]
    lb_all = jnp.cumsum(jax.nn.softmax(hg_lb.astype(F32), axis=0), axis=0)
    yp, ys = x_prompt, x_sample
    outs = [[] for _ in range(6)]
    for l in range(depth):
        w = _layer_weights(l, lb_all, norm1_w, w_in, q_norm_w, k_norm_w, lambda_q1, lambda_k1,
                           lambda_q2, lambda_k2, subln_w, hg_norm_w, w_out, norm2_w,
                           w_router_grp, b_router_grp, w_router_exp, b_router_exp,
                           w_gate, w_up, w_down)
        s0p = jnp.zeros((x_prompt.shape[0],) + state_hgrn.shape[2:], F32)
        yp, k_r, v_r, s_r = _layer(yp, s0p, None, l, w)
        past = (cache_k[l].reshape(n_phys, page * heads, head_w),
                cache_v[l].reshape(n_phys, page * heads, head_w), page_table)
        ys, k_s, v_s, s_s = _layer(ys, state_hgrn[l], past, l, w)
        for lst, val in zip(outs, (k_r, v_r, s_r, k_s, v_s, s_s)):
            lst.append(val)
    kp, vp, sp, kss, vss, sss = (jnp.stack(o) for o in outs)
    return (yp, ys, kp, vp, sp, kss, vss, sss)
```

```python
import functools
import math

import jax
import jax.numpy as jnp
from jax import lax
from jax.experimental import pallas as pl
from jax.experimental.pallas import tpu as pltpu

F32 = jnp.float32
BF16 = jnp.bfloat16

DA_HEAD_DIM = 64
DA_HEAD_W = 2 * DA_HEAD_DIM
HG_DK = 128
HG_DV = 128
HG_CHUNK = 128
_HG_FAST_BLOCK = 16
_HG_FAST_LIMIT = 60.0
N_GROUPS = 4
EXPERTS_PER_GROUP = 4
N_EXPERTS = N_GROUPS * EXPERTS_PER_GROUP
EPS = 1e-6
NEG_INF = -1e30
_Q_SCALE = DA_HEAD_DIM ** -0.5 * math.log2(math.e)
LANES = 128
VMEM_LIMIT = 56 * 1024 * 1024

_NT = (((1,), (1,)), ((), ()))


def _row_tile(n, pref):
    for t in range(min(pref, n), 7, -1):
        if n % t == 0 and t % 8 == 0:
            return t
    return n


def _sigmoid(x):
    return 1.0 / (1.0 + jnp.exp(-x))


def _params(*sem):
    return pltpu.CompilerParams(dimension_semantics=sem, vmem_limit_bytes=VMEM_LIMIT)


def _inproj_kernel(x_ref, n1_ref, w_ref, qw_ref, kw_ref, lb_ref, gm_ref,
                   q_ref, k_ref, v_ref, hq_ref, f_ref, hi_ref, hg_ref, *attn_refs, da_w, hg_w):
    x = x_ref[...]
    xn = x * lax.rsqrt(jnp.mean(x * x, axis=-1, keepdims=True) + EPS) * n1_ref[...]
    xb = xn.astype(BF16)

    def proj(lo, width):
        return jnp.dot(xb, w_ref[:, lo:lo + width], preferred_element_type=F32)

    gm = gm_ref[...]

    def group_rms(t, w):
        ms = jnp.dot((t * t).astype(BF16), gm, preferred_element_type=F32)
        return t * lax.rsqrt(ms + EPS) * w

    q_ref[...] = (group_rms(proj(0, da_w), qw_ref[...]) * _Q_SCALE).astype(q_ref.dtype)
    k = group_rms(proj(da_w, da_w), kw_ref[...])
    v = proj(2 * da_w, da_w)
    if attn_refs:
        heads = da_w // DA_HEAD_W
        rows = k.shape[0]
        for h in range(heads):
            hs = slice(h * DA_HEAD_W, (h + 1) * DA_HEAD_W)
            k_ref[pl.ds(h, rows, stride=heads), :] = k[:, hs]
            v_ref[pl.ds(h, rows, stride=heads), :] = v[:, hs]
        kb_ref, vt_ref = attn_refs
        kb_ref[...] = k.astype(BF16)
        vt_ref[...] = v.T.astype(BF16)
    else:
        k_ref[...] = k
        v_ref[...] = v
    base = 3 * da_w
    hq = proj(base, hg_w)
    hq_ref[...] = hq * _sigmoid(hq)
    lb = lb_ref[...]
    f_ref[...] = lb + (1.0 - lb) * _sigmoid(proj(base + hg_w, hg_w))
    hi_ref[...] = proj(base + 2 * hg_w, hg_w)
    hg = proj(base + 3 * hg_w, hg_w)
    hg_ref[...] = hg * _sigmoid(hg)


def _inproj(x2d, n1, w_in_bf, qw, kw, lb, gm, for_prompt):
    n, d = x2d.shape
    da_w = qw.shape[1]
    hg_w = lb.shape[1]
    tm = _row_tile(n, 512)
    row = lambda w: pl.BlockSpec((tm, w), lambda i: (i, 0))
    full = lambda a: pl.BlockSpec(a.shape, lambda i: (0, 0))
    heads = da_w // DA_HEAD_W
    if for_prompt:
        kv_shape = jax.ShapeDtypeStruct((n * heads, DA_HEAD_W), F32)
        kv_spec = pl.BlockSpec((tm * heads, DA_HEAD_W), lambda i: (i, 0))
    else:
        kv_shape = jax.ShapeDtypeStruct((n, da_w), F32)
        kv_spec = row(da_w)
    out_shapes = [jax.ShapeDtypeStruct((n, da_w), BF16 if for_prompt else F32), kv_shape, kv_shape]
    out_shapes += [jax.ShapeDtypeStruct((n, hg_w), F32)] * 4
    out_specs = [row(da_w), kv_spec, kv_spec] + [row(hg_w)] * 4
    if for_prompt:
        out_shapes += [jax.ShapeDtypeStruct((n, da_w), BF16), jax.ShapeDtypeStruct((da_w, n), BF16)]
        out_specs += [row(da_w), pl.BlockSpec((da_w, tm), lambda i: (0, i))]
    return pl.pallas_call(
        functools.partial(_inproj_kernel, da_w=da_w, hg_w=hg_w),
        grid=(n // tm,),
        in_specs=[row(d), full(n1), full(w_in_bf), full(qw), full(kw), full(lb), full(gm)],
        out_specs=out_specs,
        out_shape=out_shapes,
        compiler_params=_params("parallel"),
    )(x2d, n1, w_in_bf, qw, kw, lb, gm)


def _lambda(lq1_ref, lk1_ref, lq2_ref, lk2_ref, lam_init):
    s1 = jnp.sum(lq1_ref[...] * lk1_ref[...], axis=-1, keepdims=True)
    s2 = jnp.sum(lq2_ref[...] * lk2_ref[...], axis=-1, keepdims=True)
    return jnp.exp(s1) - jnp.exp(s2) + lam_init


def _softmax_update(s, m, l, acc, vb):
    m_new = jnp.maximum(m, jnp.max(s, axis=-1, keepdims=True))
    p = jnp.exp2(s - m_new)
    alpha = jnp.exp2(m - m_new)
    l_new = alpha * l + jnp.sum(p, axis=-1, keepdims=True)
    acc_new = alpha * acc + jnp.dot(p.astype(BF16), vb, preferred_element_type=F32)
    return m_new, l_new, acc_new


def _sub_ln(o, sw, lam_init):
    y = o * lax.rsqrt(jnp.mean(o * o, axis=-1, keepdims=True) + EPS)
    return y * sw * (1.0 - lam_init)


_ONES_ROWS = 16
_NEXT_TILE_SLOT = 2


def _attn_prompt_kernel(q_ref, k_ref, vt_ref, lq1_ref, lk1_ref, lq2_ref, lk2_ref, swc_ref,
                        o_ref, s_sc, acc_sc, *, blk, lam_init):
    qi = pl.program_id(2)
    lane = lax.broadcasted_iota(jnp.int32, (blk, DA_HEAD_W), 1)
    zero = jnp.zeros((blk, DA_HEAD_W), BF16)
    ones = jnp.ones((_ONES_ROWS, blk), BF16)

    def query_maps(tile):
        q = q_ref[pl.ds(pl.multiple_of(tile * blk, blk), blk), :]
        return (jnp.where(lane < DA_HEAD_DIM, q, zero),
                jnp.where(lane >= DA_HEAD_DIM, q, zero))

    q_maps = query_maps(qi)

    def update(s, m, mp, vt1):
        m_new = jnp.maximum(m, jnp.max(s, axis=0, keepdims=True))
        p = jnp.exp2(s - m_new)
        alpha = jnp.exp2(m.astype(F32) - m_new.astype(F32))
        acc_sc[mp] = alpha * acc_sc[mp] + jnp.dot(vt1, p, preferred_element_type=F32)
        return m_new

    def stage_scores(c, slot, maps=q_maps):
        kb = k_ref[pl.ds(pl.multiple_of(c * blk, blk), blk), :]
        for mp in range(2):
            s_sc[slot, mp] = lax.dot_general(kb, maps[mp], _NT,
                                             preferred_element_type=F32).astype(BF16)

    def stage_next_tile():
        nxt = jnp.minimum(qi + 1, pl.num_programs(2) - 1)
        stage_scores(0, _NEXT_TILE_SLOT, query_maps(nxt))

    def consume(c, s1, s2, carry):
        m1, m2 = carry
        vt1 = jnp.concatenate([vt_ref[:, pl.ds(pl.multiple_of(c * blk, blk), blk)], ones], axis=0)
        return update(s1, m1, 0, vt1), update(s2, m2, 1, vt1)

    def consume_diagonal(slot, carry):
        key = lax.broadcasted_iota(jnp.int32, (blk, blk), 0)
        qry = lax.broadcasted_iota(jnp.int32, (blk, blk), 1)
        neg = jnp.full((blk, blk), NEG_INF, BF16)
        return consume(qi, jnp.where(key <= qry, s_sc[slot, 0], neg),
                       jnp.where(key <= qry, s_sc[slot, 1], neg), carry)

    @pl.when(qi == 0)
    def _first_tile():
        stage_scores(0, 0)

    @pl.when(qi > 0)
    def _staged_by_previous_tile():
        s_sc[0] = s_sc[_NEXT_TILE_SLOT]

    def chunk_pair(j, c):
        stage_scores(2 * j + 1, 1)
        c = consume(2 * j, s_sc[0, 0], s_sc[0, 1], c)
        stage_scores(2 * j + 2, 0)
        return consume(2 * j + 1, s_sc[1, 0], s_sc[1, 1], c)

    m0 = jnp.full((1, blk), NEG_INF, BF16)
    acc_sc[...] = jnp.zeros(acc_sc.shape, F32)
    carry = lax.fori_loop(0, qi // 2, chunk_pair, (m0, m0))

    def tail_odd(c):
        stage_scores(qi, 1)
        c = consume(qi - 1, s_sc[0, 0], s_sc[0, 1], c)
        stage_next_tile()
        return consume_diagonal(1, c)

    def tail_even(c):
        stage_next_tile()
        return consume_diagonal(0, c)

    lax.cond(qi % 2 == 1, tail_odd, tail_even, carry)
    a1 = acc_sc[0]
    a2 = acc_sc[1]
    lam = _lambda(lq1_ref, lk1_ref, lq2_ref, lk2_ref, lam_init)
    o1 = a1[:DA_HEAD_W] * (1.0 / a1[DA_HEAD_W:DA_HEAD_W + 1])
    o2 = a2[:DA_HEAD_W] * (1.0 / a2[DA_HEAD_W:DA_HEAD_W + 1])
    o = o1 - lam * o2
    y = o * lax.rsqrt(jnp.mean(o * o, axis=0, keepdims=True) + EPS) * swc_ref[...]
    o_ref[...] = (y * (1.0 - lam_init)).T


def _attn_prompt(q, kb, vt, lq1, lk1, lq2, lk2, sw, batch, seq, lam_init):
    n, da_w = q.shape
    heads = da_w // DA_HEAD_W
    blk = _row_tile(seq, 512)
    nq = seq // blk
    swc = sw.reshape(-1, 1)
    small = lambda a: pl.BlockSpec(a.shape, lambda b, h, i: (0, 0))
    return pl.pallas_call(
        functools.partial(_attn_prompt_kernel, blk=blk, lam_init=lam_init),
        grid=(batch, heads, nq),
        in_specs=[
            pl.BlockSpec((seq, DA_HEAD_W), lambda b, h, i: (b, h)),
            pl.BlockSpec((seq, DA_HEAD_W), lambda b, h, i: (b, h)),
            pl.BlockSpec((DA_HEAD_W, seq), lambda b, h, i: (h, b)),
            small(lq1), small(lk1), small(lq2), small(lk2), small(swc),
        ],
        out_specs=pl.BlockSpec((blk, DA_HEAD_W), lambda b, h, i: (b * nq + i, h)),
        out_shape=jax.ShapeDtypeStruct((n, da_w), F32),
        scratch_shapes=[pltpu.VMEM((_NEXT_TILE_SLOT + 1, 2, blk, blk), BF16),
                        pltpu.VMEM((2, DA_HEAD_W + _ONES_ROWS, blk), F32)],
        compiler_params=_params("parallel", "parallel", "arbitrary"),
    )(q, kb, vt, lq1, lk1, lq2, lk2, swc)


_PAGE_SLOTS = 4


def _attn_decode_kernel(pt_ref, q_ref, kn_ref, vn_ref, lq1_ref, lk1_ref, lq2_ref, lk2_ref,
                        sw_ref, ck_hbm, cv_hbm, o_ref, kbuf, vbuf, sem, m_sc, l_sc, acc_sc,
                        *, pages_per_step, heads, page, lam_init):
    g = pl.program_id(1)
    n_steps = pl.num_programs(1)
    step = pl.program_id(0) * n_steps + g
    total = pl.num_programs(0) * n_steps
    slot = step % _PAGE_SLOTS

    def page_copies(b, gg, sl):
        copies = []
        for j in range(pages_per_step):
            pid = pt_ref[b, gg * pages_per_step + j]
            copies.append((pltpu.make_async_copy(ck_hbm.at[pid], kbuf.at[sl, j], sem.at[0, sl]), 0))
            copies.append((pltpu.make_async_copy(cv_hbm.at[pid], vbuf.at[sl, j], sem.at[1, sl]), 1))
        return copies

    def fetch(st):
        for cp, prio in page_copies(st // n_steps, st % n_steps, st % _PAGE_SLOTS):
            cp.start(priority=prio)

    @pl.when(step == 0)
    def _first_fetches():
        for ahead in range(_PAGE_SLOTS - 1):
            pl.when(ahead < total)(lambda ahead=ahead: fetch(ahead))

    @pl.when(step + (_PAGE_SLOTS - 1) < total)
    def _prefetch():
        fetch(step + (_PAGE_SLOTS - 1))

    for cp, _ in page_copies(pl.program_id(0), g, slot):
        cp.wait()
    k_refs = [kbuf.at[slot, j] for j in range(pages_per_step)]
    v_refs = [vbuf.at[slot, j] for j in range(pages_per_step)]
    q = q_ref[...]
    t_new = q.shape[0]
    lane = lax.broadcasted_iota(jnp.int32, (t_new, DA_HEAD_W), 1)
    head_cols = [slice(h * DA_HEAD_W, (h + 1) * DA_HEAD_W) for h in range(heads)]
    q2 = [jnp.concatenate([jnp.where(lane < DA_HEAD_DIM, q[:, hs], 0.0),
                           jnp.where(lane >= DA_HEAD_DIM, q[:, hs], 0.0)], axis=0).astype(BF16)
          for hs in head_cols]

    @pl.when(g == 0)
    def _new_tokens():
        pad = jnp.zeros((LANES - t_new, DA_HEAD_W), F32)
        for h, hs in enumerate(head_cols):
            kn = jnp.concatenate([kn_ref[:, hs], pad], axis=0).astype(BF16)
            vn = jnp.concatenate([vn_ref[:, hs], pad], axis=0).astype(BF16)
            s = lax.dot_general(q2[h], kn, _NT, preferred_element_type=F32)
            r = lax.broadcasted_iota(jnp.int32, s.shape, 0)
            c = lax.broadcasted_iota(jnp.int32, s.shape, 1)
            tok = jnp.where(r >= t_new, r - t_new, r)
            s = jnp.where(c <= tok, s, NEG_INF)
            m = jnp.max(s, axis=-1, keepdims=True)
            p = jnp.exp2(s - m)
            m_sc[h] = m
            l_sc[h] = jnp.sum(p, axis=-1, keepdims=True)
            acc_sc[h] = jnp.dot(p.astype(BF16), vn, preferred_element_type=F32)

    for h in range(heads):
        kb = jnp.concatenate([r_[pl.ds(h, page, stride=heads), :] for r_ in k_refs],
                             axis=0).astype(BF16)
        vb = jnp.concatenate([r_[pl.ds(h, page, stride=heads), :] for r_ in v_refs],
                             axis=0).astype(BF16)
        s = lax.dot_general(q2[h], kb, _NT, preferred_element_type=F32)
        m, l, acc = _softmax_update(s, m_sc[h], l_sc[h], acc_sc[h], vb)
        m_sc[h] = m
        l_sc[h] = l
        acc_sc[h] = acc

    @pl.when(g == pl.num_programs(1) - 1)
    def _finish():
        lam = _lambda(lq1_ref, lk1_ref, lq2_ref, lk2_ref, lam_init)
        sw = sw_ref[...]
        for h in range(heads):
            on = acc_sc[h] * (1.0 / l_sc[h])
            o = on[:t_new] - lam * on[t_new:]
            o_ref[:, h * DA_HEAD_W:(h + 1) * DA_HEAD_W] = _sub_ln(o, sw, lam_init)


def _attn_decode(q3, kn3, vn3, cache_k3, cache_v3, page_table, lq1, lk1, lq2, lk2, sw,
                 heads, lam_init):
    nb, t_new, da_w = q3.shape
    n_pages = page_table.shape[1]
    page = cache_k3.shape[1] // heads
    pps = math.gcd(n_pages, 16)
    n_steps = n_pages // pps
    tok = pl.BlockSpec((None, t_new, da_w), lambda b, g, pt: (b, 0, 0))
    small = lambda a: pl.BlockSpec(a.shape, lambda b, g, pt: (0, 0))
    hbm = pl.BlockSpec(memory_space=pl.ANY)
    page_buf = pltpu.VMEM((_PAGE_SLOTS, pps, page * heads, DA_HEAD_W), cache_k3.dtype)

    grid_spec = pltpu.PrefetchScalarGridSpec(
        num_scalar_prefetch=1,
        grid=(nb, n_steps),
        in_specs=[tok, tok, tok, small(lq1), small(lk1), small(lq2), small(lk2), small(sw), hbm, hbm],
        out_specs=tok,
        scratch_shapes=[page_buf, page_buf,
                        pltpu.SemaphoreType.DMA((2, _PAGE_SLOTS)),
                        pltpu.VMEM((heads, 2 * t_new, 1), F32),
                        pltpu.VMEM((heads, 2 * t_new, 1), F32),
                        pltpu.VMEM((heads, 2 * t_new, DA_HEAD_W), F32)],
    )
    return pl.pallas_call(
        functools.partial(_attn_decode_kernel, pages_per_step=pps, heads=heads, page=page,
                          lam_init=lam_init),
        grid_spec=grid_spec,
        out_shape=jax.ShapeDtypeStruct((nb, t_new, da_w), F32),
        compiler_params=_params("arbitrary", "arbitrary"),
    )(page_table, q3, kn3, vn3, lq1, lk1, lq2, lk2, sw, cache_k3, cache_v3)


def _hgrn_kernel(q_ref, f_ref, v_ref, g_ref, s0_ref, nw_ref, o_ref, s_out_ref, st_sc, b_sc,
                 *, heads, n_chunks):
    c_len = HG_CHUNK
    step = pl.program_id(1)

    @pl.when(step == 0)
    def _load_state():
        for h in range(heads):
            st_sc[h] = s0_ref[h].T

    r = lax.broadcasted_iota(jnp.int32, (c_len, c_len), 0)
    c = lax.broadcasted_iota(jnp.int32, (c_len, c_len), 1)
    tri = jnp.where(c <= r, 1.0, 0.0).astype(BF16)

    def midpoint_mask(lv):
        return ((r // lv) == (c // lv)) & ((r % lv) >= lv // 2) & ((c % lv) < lv // 2)

    level_masks = {lv: midpoint_mask(lv) for lv in (16, 32, 64, 128)}
    blk = _HG_FAST_BLOCK
    diag_mask = ((r // blk) == (c // blk)) & (c <= r)
    sub = lax.broadcasted_iota(jnp.int32, (c_len // 8, 8, 1), 1)
    nw = nw_ref[...]

    def cumsum_rows(x):
        x1 = x.astype(BF16)
        r1 = x - x1.astype(F32)
        x2 = r1.astype(BF16)
        x3 = (r1 - x2.astype(F32)).astype(BF16)
        return (jnp.dot(tri, x1, preferred_element_type=F32)
                + jnp.dot(tri, x2, preferred_element_type=F32)
                + jnp.dot(tri, x3, preferred_element_type=F32))

    def midpoint_terms(q, kk, b, levels):
        a_mat = jnp.zeros((c_len, c_len), F32)
        for lv in levels:
            bl = b.reshape(c_len // lv, lv, HG_DK)
            mid = bl[:, lv // 2 - 1:lv // 2, :]
            e = jnp.exp2(-jnp.abs(bl - mid)).reshape(c_len, HG_DK)
            a_lv = lax.dot_general((q * e).astype(BF16), (kk * e).astype(BF16), _NT,
                                   preferred_element_type=F32)
            a_mat = a_mat + jnp.where(level_masks[lv], a_lv, 0.0)
        return a_mat

    def intra_fast(q, kk, v, b, f):
        bl = b.reshape(c_len // blk, blk, HG_DK)
        lf0 = jnp.log2(f.reshape(c_len // blk, blk, HG_DK)[:, 0:1, :])
        d = (bl - (bl[:, 0:1, :] - lf0)).reshape(c_len, HG_DK)
        a_blk = lax.dot_general((q * jnp.exp2(d)).astype(BF16), (kk * jnp.exp2(-d)).astype(BF16),
                                _NT, preferred_element_type=F32)
        a_mat = jnp.where(diag_mask, a_blk, 0.0) + midpoint_terms(q, kk, b, (32, 64, 128))
        return jnp.dot(a_mat.astype(BF16), v.astype(BF16), preferred_element_type=F32)

    def intra_exact(q, kk, v, b, f):
        del f
        q3 = q.reshape(c_len // 8, 8, HG_DK)
        b3 = b.reshape(c_len // 8, 8, HG_DK)
        k3 = kk.reshape(c_len // 8, 8, HG_DK)
        v3 = v.reshape(c_len // 8, 8, HG_DV)
        o3 = jnp.zeros((c_len // 8, 8, HG_DV), F32)
        for s in range(8):
            dec = jnp.exp2(jnp.minimum(b3 - b3[:, s:s + 1, :], 0.0))
            a = jnp.sum(q3 * k3[:, s:s + 1, :] * dec, axis=-1, keepdims=True)
            o3 = o3 + jnp.where(sub >= s, a, 0.0) * v3[:, s:s + 1, :]
        a_mat = midpoint_terms(q, kk, b, (16, 32, 64, 128))
        return o3.reshape(c_len, HG_DV) + jnp.dot(a_mat.astype(BF16), v.astype(BF16),
                                                  preferred_element_type=F32)

    def head_cols(h):
        return slice(h * HG_DK, (h + 1) * HG_DK)

    def prefix(ci, growth):
        r0 = pl.multiple_of(ci * c_len, c_len)
        for h in range(heads):
            lf = jnp.log2(f_ref[pl.ds(r0, c_len), head_cols(h)])
            b = cumsum_rows(lf)
            b_sc[pl.ds(r0, c_len), head_cols(h)] = b
            bl = b.reshape(c_len // blk, blk, HG_DK)
            before = bl[:, 0:1, :] - lf.reshape(c_len // blk, blk, HG_DK)[:, 0:1, :]
            growth = jnp.maximum(growth, before - bl[:, blk - 1:blk, :])
        return growth

    growth = lax.fori_loop(0, n_chunks, prefix, jnp.zeros((c_len // blk, 1, HG_DK), F32))
    safe = jnp.max(growth) < _HG_FAST_LIMIT

    def run_chunks(intra):
        for ci in range(n_chunks):
            rows = pl.ds(ci * c_len, c_len)
            for h in range(heads):
                hs = head_cols(h)
                q = q_ref[rows, hs]
                f = f_ref[rows, hs]
                v = v_ref[rows, hs]
                b = b_sc[rows, hs]
                kk = 1.0 - f
                st = st_sc[h]
                o = intra(q, kk, v, b, f) + lax.dot_general(
                    (q * jnp.exp2(b)).astype(BF16), st.astype(BF16), _NT,
                    preferred_element_type=F32)
                b_end = b[c_len - 1:c_len, :]
                k_end = kk * jnp.exp2(b_end - b)
                st_sc[h] = st * jnp.exp2(b_end) + jnp.dot(
                    v.T.astype(BF16), k_end.astype(BF16), preferred_element_type=F32)
                y = o * lax.rsqrt(jnp.mean(o * o, axis=-1, keepdims=True) + EPS) * nw
                o_ref[rows, hs] = y * g_ref[rows, hs]

    pl.when(safe)(lambda: run_chunks(intra_fast))
    pl.when(jnp.logical_not(safe))(lambda: run_chunks(intra_exact))

    @pl.when(step == pl.num_programs(1) - 1)
    def _store_state():
        for h in range(heads):
            s_out_ref[h] = st_sc[h].T


def _hgrn(hq, f, hi, hg, s0, nw, batch, seq):
    n, hg_w = hq.shape
    heads = hg_w // HG_DK
    tb = HG_CHUNK * math.gcd(seq // HG_CHUNK, 4)
    steps = seq // tb
    row = pl.BlockSpec((tb, hg_w), lambda b, t: (b * steps + t, 0))
    st = pl.BlockSpec((None, heads, HG_DK, HG_DV), lambda b, t: (b, 0, 0, 0))
    return pl.pallas_call(
        functools.partial(_hgrn_kernel, heads=heads, n_chunks=tb // HG_CHUNK),
        grid=(batch, steps),
        in_specs=[row, row, row, row, st, pl.BlockSpec(nw.shape, lambda b, t: (0, 0))],
        out_specs=[row, st],
        out_shape=[jax.ShapeDtypeStruct((n, hg_w), F32),
                   jax.ShapeDtypeStruct(s0.shape, F32)],
        scratch_shapes=[pltpu.VMEM((heads, HG_DV, HG_DK), F32), pltpu.VMEM((tb, hg_w), F32)],
        compiler_params=_params("parallel", "arbitrary"),
    )(hq, f, hi, hg, s0, nw)


_HG_SHORT = 8


def _hgrn_short_kernel(q_ref, f_ref, v_ref, g_ref, s0_ref, nw_ref, o_ref, s_out_ref, *, heads):
    t = _HG_SHORT
    row = lax.broadcasted_iota(jnp.int32, (t, 1), 0)
    pad8 = jnp.zeros((t, HG_DK), F32)
    pad = jnp.zeros((LANES - t, HG_DK), F32)
    nw = nw_ref[...]
    for h in range(heads):
        hs = slice(h * HG_DK, (h + 1) * HG_DK)
        q = q_ref[:, hs]
        f = f_ref[:, hs]
        v = v_ref[:, hs]
        kk = 1.0 - f
        lf = jnp.log2(f)
        b = jnp.zeros_like(lf)
        for s in range(t):
            b = b + jnp.where(row >= s, lf[s:s + 1, :], 0.0)
        o = jnp.zeros((t, HG_DV), F32)
        for s in range(t):
            dec = jnp.exp2(jnp.minimum(b - b[s:s + 1, :], 0.0))
            a = jnp.sum(q * kk[s:s + 1, :] * dec, axis=-1, keepdims=True)
            o = o + jnp.where(row >= s, a, 0.0) * v[s:s + 1, :]
        st = s0_ref[h].T
        q_dec = jnp.concatenate([q * jnp.exp2(b), pad8], axis=0).astype(BF16)
        o = o + lax.dot_general(q_dec, st.astype(BF16), _NT, preferred_element_type=F32)[:t]
        b_end = b[t - 1:t, :]
        k_end = jnp.concatenate([kk * jnp.exp2(b_end - b), pad], axis=0)
        v_pad = jnp.concatenate([v, pad], axis=0)
        st_new = st * jnp.exp2(b_end) + jnp.dot(v_pad.T.astype(BF16), k_end.astype(BF16),
                                                preferred_element_type=F32)
        s_out_ref[h] = st_new.T
        y = o * lax.rsqrt(jnp.mean(o * o, axis=-1, keepdims=True) + EPS) * nw
        o_ref[:, hs] = y * g_ref[:, hs]


def _hgrn_short(hq, f, hi, hg, s0, nw, batch):
    n, hg_w = hq.shape
    heads = hg_w // HG_DK
    r3 = lambda a: a.reshape(batch, _HG_SHORT, hg_w)
    tok = pl.BlockSpec((None, _HG_SHORT, hg_w), lambda b: (b, 0, 0))
    st = pl.BlockSpec((None, heads, HG_DK, HG_DV), lambda b: (b, 0, 0, 0))
    o, s_new = pl.pallas_call(
        functools.partial(_hgrn_short_kernel, heads=heads),
        grid=(batch,),
        in_specs=[tok, tok, tok, tok, st, pl.BlockSpec(nw.shape, lambda b: (0, 0))],
        out_specs=[tok, st],
        out_shape=[jax.ShapeDtypeStruct((batch, _HG_SHORT, hg_w), F32),
                   jax.ShapeDtypeStruct(s0.shape, F32)],
        compiler_params=_params("parallel"),
    )(r3(hq), r3(f), r3(hi), r3(hg), s0, nw)
    return o.reshape(n, hg_w), s_new


def _outproj_kernel(oda_ref, ohg_ref, x_ref, wo_ref, n2_ref, wrh_ref, wrl_ref, br_ref,
                    h_ref, xn_ref, gates_ref, *, da_w):
    o = (jnp.dot(oda_ref[...].astype(BF16), wo_ref[:da_w, :], preferred_element_type=F32)
         + jnp.dot(ohg_ref[...].astype(BF16), wo_ref[da_w:, :], preferred_element_type=F32))
    h = x_ref[...] + o
    h_ref[...] = h
    xn = h * lax.rsqrt(jnp.mean(h * h, axis=-1, keepdims=True) + EPS) * n2_ref[...]
    xh = xn.astype(BF16)
    xn_ref[...] = xh
    xl = (xn - xh.astype(F32)).astype(BF16)
    wrh = wrh_ref[...]
    both = jnp.dot(xh, jnp.concatenate([wrh, wrl_ref[...]], axis=-1), preferred_element_type=F32)
    logits = (both[:, :LANES] + jnp.dot(xl, wrh, preferred_element_type=F32)
              + both[:, LANES:]) + br_ref[...]
    lane = lax.broadcasted_iota(jnp.int32, logits.shape, 1)
    lane_f = lane.astype(F32)
    big = float(LANES)
    is_g = (lane >= N_EXPERTS) & (lane < N_EXPERTS + N_GROUPS)
    gl = jnp.where(is_g, logits, NEG_INF)
    g_max = jnp.max(gl, axis=-1, keepdims=True)
    g_idx = jnp.min(jnp.where(gl == g_max, lane_f, big), axis=-1, keepdims=True) - N_EXPERTS
    g_w = 1.0 / jnp.sum(jnp.exp(gl - g_max), axis=-1, keepdims=True)
    grp_of_lane = (lane // EXPERTS_PER_GROUP).astype(F32)
    sel = (lane < N_EXPERTS) & (grp_of_lane == g_idx)
    el = jnp.where(sel, logits, NEG_INF)
    v1 = jnp.max(el, axis=-1, keepdims=True)
    i1 = jnp.min(jnp.where(sel, jnp.where(el == v1, lane_f, big), big), axis=-1, keepdims=True)
    el2 = jnp.where(lane_f == i1, NEG_INF, el)
    v2 = jnp.max(el2, axis=-1, keepdims=True)
    sel2 = sel & (lane_f != i1)
    i2 = jnp.min(jnp.where(sel2, jnp.where(el2 == v2, lane_f, big), big), axis=-1, keepdims=True)
    t = jnp.exp(v2 - v1)
    p1 = 1.0 / (1.0 + t)
    p2 = t * p1
    gates_ref[...] = (jnp.where(lane_f == i1, p1 * g_w, 0.0)
                      + jnp.where(lane_f == i2, p2 * g_w, 0.0))


def _outproj(oda, ohg, x2d, wo_bf, n2, wr_hi, wr_lo, br):
    n, d = x2d.shape
    da_w = oda.shape[1]
    tm = _row_tile(n, 512)
    row = lambda w: pl.BlockSpec((tm, w), lambda i: (i, 0))
    full = lambda a: pl.BlockSpec(a.shape, lambda i: (0, 0))
    return pl.pallas_call(
        functools.partial(_outproj_kernel, da_w=da_w),
        grid=(n // tm,),
        in_specs=[row(da_w), row(ohg.shape[1]), row(d), full(wo_bf), full(n2), full(wr_hi),
                  full(wr_lo), full(br)],
        out_specs=[row(d), row(d), row(LANES)],
        out_shape=[jax.ShapeDtypeStruct((n, d), F32), jax.ShapeDtypeStruct((n, d), BF16),
                   jax.ShapeDtypeStruct((n, LANES), F32)],
        compiler_params=_params("parallel"),
    )(oda, ohg, x2d, wo_bf, n2, wr_hi, wr_lo, br)


_MOE_EXPERTS_PER_STEP = 4


def _moe_kernel(x_ref, h_ref, gates_ref, wg_ref, wu_ref, wd_ref, y_ref):
    eb = pl.program_id(1)
    per_step = wg_ref.shape[0]

    @pl.when(eb == 0)
    def _init():
        y_ref[...] = h_ref[...]

    x = x_ref[...]
    gates = gates_ref[...]
    lane = lax.broadcasted_iota(jnp.int32, gates.shape, 1)
    hid = []
    for j in range(per_step):
        a = jnp.dot(x, wg_ref[j], preferred_element_type=F32)
        u = jnp.dot(x, wu_ref[j], preferred_element_type=F32)
        gate = jnp.sum(jnp.where(lane == eb * per_step + j, gates, 0.0), axis=-1, keepdims=True)
        hid.append(((a * _sigmoid(a)) * u * gate).astype(BF16))
    y_ref[...] += jnp.dot(jnp.concatenate(hid, axis=-1), wd_ref[...], preferred_element_type=F32)


def _moe(xn_bf, h, gates, wg_bf, wu_bf, wd_bf):
    n, d = h.shape
    n_exp, _, ff = wg_bf.shape
    per_step = math.gcd(n_exp, _MOE_EXPERTS_PER_STEP)
    tm = _row_tile(n, 1024)
    row = lambda w: pl.BlockSpec((tm, w), lambda i, e: (i, 0))
    return pl.pallas_call(
        _moe_kernel,
        grid=(n // tm, n_exp // per_step),
        in_specs=[row(d), row(d), row(LANES),
                  pl.BlockSpec((per_step, d, ff), lambda i, e: (e, 0, 0)),
                  pl.BlockSpec((per_step, d, ff), lambda i, e: (e, 0, 0)),
                  pl.BlockSpec((per_step * ff, d), lambda i, e: (e, 0))],
        out_specs=row(d),
        out_shape=jax.ShapeDtypeStruct((n, d), F32),
        compiler_params=_params("parallel", "arbitrary"),
    )(xn_bf, h, gates, wg_bf, wu_bf, wd_bf.reshape(n_exp * ff, d))


def _pad_tokens(a, batch, seq, seq_pad, value):
    a3 = a.reshape(batch, seq, a.shape[-1])
    a3 = jnp.pad(a3, ((0, 0), (0, seq_pad - seq), (0, 0)), constant_values=value)
    return a3.reshape(batch * seq_pad, a.shape[-1])


def _layer(x, s0, past, layer_idx, w):
    batch, seq, d = x.shape
    n = batch * seq
    x2d = x.reshape(n, d)
    da_w = w["qw"].shape[1]
    heads = da_w // DA_HEAD_W
    lam_init = 0.8 - 0.6 * math.exp(-0.3 * layer_idx)
    lam_args = (w["lq1"], w["lk1"], w["lq2"], w["lk2"], w["sw"])

    q, k, v, hq, f, hi, hg, *attn_in = _inproj(x2d, w["n1"], w["w_in"], w["qw"], w["kw"], w["lb"],
                                               w["gm"], past is None)
    if past is None:
        oda = _attn_prompt(q, *attn_in, *lam_args, batch, seq, lam_init)
    else:
        cache_k3, cache_v3, page_table = past
        r3 = lambda a: a.reshape(batch, seq, da_w)
        oda = _attn_decode(r3(q), r3(k), r3(v), cache_k3, cache_v3, page_table, *lam_args,
                           heads, lam_init).reshape(n, da_w)

    seq_pad = -(-seq // HG_CHUNK) * HG_CHUNK
    if seq == _HG_SHORT:
        ohg, s_new = _hgrn_short(hq, f, hi, hg, s0, w["hnw"], batch)
    elif seq_pad != seq:
        hq_p, hi_p, hg_p = (_pad_tokens(a, batch, seq, seq_pad, 0.0) for a in (hq, hi, hg))
        f_p = _pad_tokens(f, batch, seq, seq_pad, 1.0)
        ohg, s_new = _hgrn(hq_p, f_p, hi_p, hg_p, s0, w["hnw"], batch, seq_pad)
        ohg = ohg.reshape(batch, seq_pad, -1)[:, :seq].reshape(n, -1)
    else:
        ohg, s_new = _hgrn(hq, f, hi, hg, s0, w["hnw"], batch, seq)

    h, xn_bf, gates = _outproj(oda, ohg, x2d, w["w_out"], w["n2"], w["wr_hi"], w["wr_lo"], w["br"])
    y = _moe(xn_bf, h, gates, w["wg"], w["wu"], w["wd"])
    kv_shape = (batch, seq, heads, DA_HEAD_W)
    return y.reshape(batch, seq, d), k.reshape(kv_shape), v.reshape(kv_shape), s_new


def _layer_weights(l, lb_all, norm1_w, w_in, q_norm_w, k_norm_w, lambda_q1, lambda_k1, lambda_q2,
                   lambda_k2, subln_w, hg_norm_w, w_out, norm2_w, w_router_grp, b_router_grp,
                   w_router_exp, b_router_exp, w_gate, w_up, w_down):
    d = w_in.shape[1]
    da_w = w_out.shape[1] // 2
    n_grp_norm = da_w // DA_HEAD_DIM
    row = lambda a: a.reshape(1, -1).astype(F32)
    g_id = jnp.arange(da_w) // DA_HEAD_DIM
    gm = jnp.where(g_id[:, None] == g_id[None, :], 1.0 / DA_HEAD_DIM, 0.0).astype(BF16)
    wr = jnp.zeros((d, LANES), F32)
    wr = wr.at[:, :N_EXPERTS].set(w_router_exp[l]).at[:, N_EXPERTS:N_EXPERTS + N_GROUPS].set(
        w_router_grp[l])
    wr_hi = wr.astype(BF16)
    wr_lo = (wr - wr_hi.astype(F32)).astype(BF16)
    br = jnp.zeros((1, LANES), F32)
    br = br.at[0, :N_EXPERTS].set(b_router_exp[l]).at[0, N_EXPERTS:N_EXPERTS + N_GROUPS].set(
        b_router_grp[l])
    return dict(
        n1=row(norm1_w[l]), w_in=w_in[l].astype(BF16),
        qw=row(jnp.tile(q_norm_w[l], n_grp_norm)), kw=row(jnp.tile(k_norm_w[l], n_grp_norm)),
        lb=row(lb_all[l]), gm=gm,
        lq1=row(lambda_q1[l]), lk1=row(lambda_k1[l]), lq2=row(lambda_q2[l]), lk2=row(lambda_k2[l]),
        sw=row(subln_w[l]), hnw=row(hg_norm_w[l]),
        w_out=w_out[l].astype(BF16), n2=row(norm2_w[l]), wr_hi=wr_hi, wr_lo=wr_lo, br=br,
        wg=w_gate[l].astype(BF16), wu=w_up[l].astype(BF16), wd=w_down[l].astype(BF16),
    )


def kernel(x_prompt, x_sample, cache_k, cache_v, state_hgrn, page_table, norm1_w, w_in, q_norm_w, k_norm_w, lambda_q1, lambda_k1, lambda_q2, lambda_k2, subln_w, hg_lb, hg_norm_w, w_out, norm2_w, w_router_grp, b_router_grp, w_router_exp, b_router_exp, w_gate, w_up, w_down):
    depth = norm1_w.shape[0]
    n_phys, page, heads, head_w = cache_k.shape[1:]
    lb_all = jnp.cumsum(jax.nn.softmax(hg_lb.astype(F32), axis=0), axis=0)
    yp, ys = x_prompt, x_sample
    outs = [[] for _ in range(6)]
    for l in range(depth):
        w = _layer_weights(l, lb_all, norm1_w, w_in, q_norm_w, k_norm_w, lambda_q1, lambda_k1,
                           lambda_q2, lambda_k2, subln_w, hg_norm_w, w_out, norm2_w,
                           w_router_grp, b_router_grp, w_router_exp, b_router_exp,
                           w_gate, w_up, w_down)
        s0p = jnp.zeros((x_prompt.shape[0],) + state_hgrn.shape[2:], F32)
        yp, k_r, v_r, s_r = _layer(yp, s0p, None, l, w)
        past = (cache_k[l].reshape(n_phys, page * heads, head_w),
                cache_v[l].reshape(n_phys, page * heads, head_w), page_table)
        ys, k_s, v_s, s_s = _layer(ys, state_hgrn[l], past, l, w)
        for lst, val in zip(outs, (k_r, v_r, s_r, k_s, v_s, s_s)):
            lst.append(val)
    kp, vp, sp, kss, vss, sss = (jnp.stack(o) for o in outs)
    return (yp, ys, kp, vp, sp, kss, vss, sss)
```

```python
import functools
import math

import jax
import jax.numpy as jnp
from jax import lax
from jax.experimental import pallas as pl
from jax.experimental.pallas import tpu as pltpu

F32 = jnp.float32
BF16 = jnp.bfloat16

DA_HEAD_DIM = 64
DA_HEAD_W = 2 * DA_HEAD_DIM
HG_DK = 128
HG_DV = 128
HG_CHUNK = 128
_HG_FAST_BLOCK = 16
_HG_FAST_LIMIT = 60.0
N_GROUPS = 4
EXPERTS_PER_GROUP = 4
N_EXPERTS = N_GROUPS * EXPERTS_PER_GROUP
EPS = 1e-6
NEG_INF = -1e30
_Q_SCALE = DA_HEAD_DIM ** -0.5 * math.log2(math.e)
LANES = 128
VMEM_LIMIT = 56 * 1024 * 1024

_NT = (((1,), (1,)), ((), ()))


def _row_tile(n, pref):
    for t in range(min(pref, n), 7, -1):
        if n % t == 0 and t % 8 == 0:
            return t
    return n


def _sigmoid(x):
    return 1.0 / (1.0 + jnp.exp(-x))


def _params(*sem):
    return pltpu.CompilerParams(dimension_semantics=sem, vmem_limit_bytes=VMEM_LIMIT)


def _inproj_kernel(x_ref, n1_ref, w_ref, qw_ref, kw_ref, lb_ref, gm_ref,
                   q_ref, k_ref, v_ref, hq_ref, f_ref, hi_ref, hg_ref, *attn_refs, da_w, hg_w):
    x = x_ref[...]
    xn = x * lax.rsqrt(jnp.mean(x * x, axis=-1, keepdims=True) + EPS) * n1_ref[...]
    xb = xn.astype(BF16)

    def proj(lo, width):
        return jnp.dot(xb, w_ref[:, lo:lo + width], preferred_element_type=F32)

    gm = gm_ref[...]

    def group_rms(t, w):
        sq = (t * t).astype(BF16)
        gw = gm.shape[0]
        ms = jnp.concatenate([jnp.dot(sq[:, lo:lo + gw], gm, preferred_element_type=F32)
                              for lo in range(0, t.shape[1], gw)], axis=-1)
        return t * lax.rsqrt(ms + EPS) * w

    q_ref[...] = (group_rms(proj(0, da_w), qw_ref[...]) * _Q_SCALE).astype(q_ref.dtype)
    k = group_rms(proj(da_w, da_w), kw_ref[...])
    v = proj(2 * da_w, da_w)
    if attn_refs:
        heads = da_w // DA_HEAD_W
        rows = k.shape[0]
        for h in range(heads):
            hs = slice(h * DA_HEAD_W, (h + 1) * DA_HEAD_W)
            k_ref[pl.ds(h, rows, stride=heads), :] = k[:, hs]
            v_ref[pl.ds(h, rows, stride=heads), :] = v[:, hs]
        kb_ref, vt_ref = attn_refs
        kb_ref[...] = k.astype(BF16)
        vt_ref[...] = v.T.astype(BF16)
    else:
        k_ref[...] = k
        v_ref[...] = v
    base = 3 * da_w
    hq = proj(base, hg_w)
    hq_ref[...] = hq * _sigmoid(hq)
    lb = lb_ref[...]
    f_ref[...] = lb + (1.0 - lb) * _sigmoid(proj(base + hg_w, hg_w))
    hi_ref[...] = proj(base + 2 * hg_w, hg_w)
    hg = proj(base + 3 * hg_w, hg_w)
    hg_ref[...] = hg * _sigmoid(hg)


def _inproj(x2d, n1, w_in_bf, qw, kw, lb, gm, for_prompt):
    n, d = x2d.shape
    da_w = qw.shape[1]
    hg_w = lb.shape[1]
    tm = _row_tile(n, 512)
    row = lambda w: pl.BlockSpec((tm, w), lambda i: (i, 0))
    full = lambda a: pl.BlockSpec(a.shape, lambda i: (0, 0))
    heads = da_w // DA_HEAD_W
    if for_prompt:
        kv_shape = jax.ShapeDtypeStruct((n * heads, DA_HEAD_W), F32)
        kv_spec = pl.BlockSpec((tm * heads, DA_HEAD_W), lambda i: (i, 0))
    else:
        kv_shape = jax.ShapeDtypeStruct((n, da_w), F32)
        kv_spec = row(da_w)
    out_shapes = [jax.ShapeDtypeStruct((n, da_w), BF16 if for_prompt else F32), kv_shape, kv_shape]
    out_shapes += [jax.ShapeDtypeStruct((n, hg_w), F32)] * 4
    out_specs = [row(da_w), kv_spec, kv_spec] + [row(hg_w)] * 4
    if for_prompt:
        out_shapes += [jax.ShapeDtypeStruct((n, da_w), BF16), jax.ShapeDtypeStruct((da_w, n), BF16)]
        out_specs += [row(da_w), pl.BlockSpec((da_w, tm), lambda i: (0, i))]
    return pl.pallas_call(
        functools.partial(_inproj_kernel, da_w=da_w, hg_w=hg_w),
        grid=(n // tm,),
        in_specs=[row(d), full(n1), full(w_in_bf), full(qw), full(kw), full(lb), full(gm)],
        out_specs=out_specs,
        out_shape=out_shapes,
        compiler_params=_params("parallel"),
    )(x2d, n1, w_in_bf, qw, kw, lb, gm)


def _lambda(lq1_ref, lk1_ref, lq2_ref, lk2_ref, lam_init):
    s1 = jnp.sum(lq1_ref[...] * lk1_ref[...], axis=-1, keepdims=True)
    s2 = jnp.sum(lq2_ref[...] * lk2_ref[...], axis=-1, keepdims=True)
    return jnp.exp(s1) - jnp.exp(s2) + lam_init


def _softmax_update(s, m, l, acc, vb):
    m_new = jnp.maximum(m, jnp.max(s, axis=-1, keepdims=True))
    p = jnp.exp2(s - m_new)
    alpha = jnp.exp2(m - m_new)
    l_new = alpha * l + jnp.sum(p, axis=-1, keepdims=True)
    acc_new = alpha * acc + jnp.dot(p.astype(BF16), vb, preferred_element_type=F32)
    return m_new, l_new, acc_new


def _sub_ln(o, sw, lam_init):
    y = o * lax.rsqrt(jnp.mean(o * o, axis=-1, keepdims=True) + EPS)
    return y * sw * (1.0 - lam_init)


_ONES_ROWS = 16
_NEXT_TILE_SLOT = 2
_Q_COLS = 256


def _attn_prompt_kernel(q_ref, k_ref, vt_ref, lq1_ref, lk1_ref, lq2_ref, lk2_ref, swc_ref,
                        o_ref, s_sc, mx_sc, acc_sc, *, blk, lam_init):
    qi = pl.program_id(2)
    lane = lax.broadcasted_iota(jnp.int32, (blk, DA_HEAD_W), 1)
    zero = jnp.zeros((blk, DA_HEAD_W), BF16)
    ones = jnp.ones((_ONES_ROWS, blk), BF16)

    def query_maps(tile):
        q = q_ref[pl.ds(pl.multiple_of(tile * blk, blk), blk), :]
        return (jnp.where(lane < DA_HEAD_DIM, q, zero),
                jnp.where(lane >= DA_HEAD_DIM, q, zero))

    q_maps = query_maps(qi)

    n_cb = max(blk // _Q_COLS, 1)
    cb_w = blk // n_cb

    def step(carry, consume_c, consume_slot, diagonal=False, stage=None):
        vt1 = jnp.concatenate([vt_ref[:, pl.ds(pl.multiple_of(consume_c * blk, blk), blk)], ones],
                              axis=0)
        if stage is not None:
            stage_c, stage_slot, maps = stage
            kb = k_ref[pl.ds(pl.multiple_of(stage_c * blk, blk), blk), :]
        m_new = ([], [])
        n_kh = max(blk // _Q_COLS, 1)
        kh_w = blk // n_kh

        def masked(s, row0, col0):
            key = lax.broadcasted_iota(jnp.int32, s.shape, 0) + row0
            qry = lax.broadcasted_iota(jnp.int32, s.shape, 1) + col0
            return jnp.where(key <= qry, s, jnp.full(s.shape, NEG_INF, BF16))

        for cb in range(n_cb):
            cols = slice(cb * cb_w, (cb + 1) * cb_w)
            for mp in range(2):
                if diagonal:
                    s_max = jnp.max(masked(s_sc[consume_slot, mp, :, cols], 0, cb * cb_w),
                                    axis=0, keepdims=True)
                else:
                    s_max = mx_sc[consume_slot, mp, :, cols].astype(BF16)
                m_old = carry[mp][:, cols]
                m = jnp.maximum(m_old, s_max)
                alpha = jnp.exp2(m_old.astype(F32) - m.astype(F32))
                pv = None
                staged_max = None
                for kh in range(n_kh):
                    rows = slice(kh * kh_w, (kh + 1) * kh_w)
                    if stage is not None:
                        s = lax.dot_general(kb[rows], maps[mp][cols], _NT,
                                            preferred_element_type=F32).astype(BF16)
                        s_sc[stage_slot, mp, rows, cols] = s
                        unit_max = jnp.max(s, axis=0, keepdims=True)
                        staged_max = unit_max if staged_max is None else jnp.maximum(staged_max,
                                                                                     unit_max)
                    s = s_sc[consume_slot, mp, rows, cols]
                    if diagonal:
                        s = masked(s, kh * kh_w, cb * cb_w)
                    p = jnp.exp2(s - m)
                    d = jnp.dot(vt1[:, rows], p, preferred_element_type=F32)
                    pv = d if pv is None else pv + d
                acc_sc[mp, :, cols] = alpha * acc_sc[mp, :, cols] + pv
                if stage is not None:
                    mx_sc[stage_slot, mp, :, cols] = staged_max.astype(F32)
                m_new[mp].append(m)
        return tuple(jnp.concatenate(ms, axis=1) for ms in m_new)

    def stage_only(c, slot, maps):
        kb = k_ref[pl.ds(pl.multiple_of(c * blk, blk), blk), :]
        for mp in range(2):
            s = lax.dot_general(kb, maps[mp], _NT, preferred_element_type=F32).astype(BF16)
            s_sc[slot, mp] = s
            mx_sc[slot, mp] = jnp.max(s, axis=0, keepdims=True).astype(F32)

    @pl.when(qi == 0)
    def _first_tile():
        stage_only(0, 0, q_maps)

    @pl.when(qi > 0)
    def _staged_by_previous_tile():
        s_sc[0] = s_sc[_NEXT_TILE_SLOT]
        mx_sc[0] = mx_sc[_NEXT_TILE_SLOT]

    def chunk_pair(j, c):
        c = step(c, 2 * j, 0, stage=(2 * j + 1, 1, q_maps))
        return step(c, 2 * j + 1, 1, stage=(2 * j + 2, 0, q_maps))

    m0 = jnp.full((1, blk), NEG_INF, BF16)
    acc_sc[...] = jnp.zeros(acc_sc.shape, F32)
    carry = lax.fori_loop(0, qi // 2, chunk_pair, (m0, m0))

    next_tile = (0, _NEXT_TILE_SLOT, query_maps(jnp.minimum(qi + 1, pl.num_programs(2) - 1)))

    def tail_odd(c):
        c = step(c, qi - 1, 0, stage=(qi, 1, q_maps))
        return step(c, qi, 1, diagonal=True, stage=next_tile)

    def tail_even(c):
        return step(c, qi, 0, diagonal=True, stage=next_tile)

    lax.cond(qi % 2 == 1, tail_odd, tail_even, carry)
    a1 = acc_sc[0]
    a2 = acc_sc[1]
    lam = _lambda(lq1_ref, lk1_ref, lq2_ref, lk2_ref, lam_init)
    o1 = a1[:DA_HEAD_W] * (1.0 / a1[DA_HEAD_W:DA_HEAD_W + 1])
    o2 = a2[:DA_HEAD_W] * (1.0 / a2[DA_HEAD_W:DA_HEAD_W + 1])
    o = o1 - lam * o2
    y = o * lax.rsqrt(jnp.mean(o * o, axis=0, keepdims=True) + EPS) * swc_ref[...]
    o_ref[...] = (y * (1.0 - lam_init)).T


def _attn_prompt(q, kb, vt, lq1, lk1, lq2, lk2, sw, batch, seq, lam_init):
    n, da_w = q.shape
    heads = da_w // DA_HEAD_W
    blk = _row_tile(seq, 512)
    nq = seq // blk
    swc = sw.reshape(-1, 1)
    small = lambda a: pl.BlockSpec(a.shape, lambda b, h, i: (0, 0))
    return pl.pallas_call(
        functools.partial(_attn_prompt_kernel, blk=blk, lam_init=lam_init),
        grid=(batch, heads, nq),
        in_specs=[
            pl.BlockSpec((seq, DA_HEAD_W), lambda b, h, i: (b, h)),
            pl.BlockSpec((seq, DA_HEAD_W), lambda b, h, i: (b, h)),
            pl.BlockSpec((DA_HEAD_W, seq), lambda b, h, i: (h, b)),
            small(lq1), small(lk1), small(lq2), small(lk2), small(swc),
        ],
        out_specs=pl.BlockSpec((blk, DA_HEAD_W), lambda b, h, i: (b * nq + i, h)),
        out_shape=jax.ShapeDtypeStruct((n, da_w), F32),
        scratch_shapes=[pltpu.VMEM((_NEXT_TILE_SLOT + 1, 2, blk, blk), BF16),
                        pltpu.VMEM((_NEXT_TILE_SLOT + 1, 2, 1, blk), F32),
                        pltpu.VMEM((2, DA_HEAD_W + _ONES_ROWS, blk), F32)],
        compiler_params=_params("parallel", "parallel", "arbitrary"),
    )(q, kb, vt, lq1, lk1, lq2, lk2, swc)


_PAGE_SLOTS = 4


def _attn_decode_kernel(pt_ref, q_ref, kn_ref, vn_ref, lq1_ref, lk1_ref, lq2_ref, lk2_ref,
                        sw_ref, ck_hbm, cv_hbm, o_ref, kbuf, vbuf, sem, m_sc, l_sc, acc_sc,
                        *, pages_per_step, heads, page, lam_init):
    g = pl.program_id(1)
    n_steps = pl.num_programs(1)
    step = pl.program_id(0) * n_steps + g
    total = pl.num_programs(0) * n_steps
    slot = step % _PAGE_SLOTS

    def page_copies(b, gg, sl):
        copies = []
        for j in range(pages_per_step):
            pid = pt_ref[b, gg * pages_per_step + j]
            copies.append((pltpu.make_async_copy(ck_hbm.at[pid], kbuf.at[sl, j], sem.at[0, sl]), 0))
            copies.append((pltpu.make_async_copy(cv_hbm.at[pid], vbuf.at[sl, j], sem.at[1, sl]), 1))
        return copies

    def fetch(st):
        for cp, prio in page_copies(st // n_steps, st % n_steps, st % _PAGE_SLOTS):
            cp.start(priority=prio)

    @pl.when(step == 0)
    def _first_fetches():
        for ahead in range(_PAGE_SLOTS - 1):
            pl.when(ahead < total)(lambda ahead=ahead: fetch(ahead))

    @pl.when(step + (_PAGE_SLOTS - 1) < total)
    def _prefetch():
        fetch(step + (_PAGE_SLOTS - 1))

    for cp, _ in page_copies(pl.program_id(0), g, slot):
        cp.wait()
    k_refs = [kbuf.at[slot, j] for j in range(pages_per_step)]
    v_refs = [vbuf.at[slot, j] for j in range(pages_per_step)]
    q = q_ref[...]
    t_new = q.shape[0]
    lane = lax.broadcasted_iota(jnp.int32, (t_new, DA_HEAD_W), 1)
    head_cols = [slice(h * DA_HEAD_W, (h + 1) * DA_HEAD_W) for h in range(heads)]
    q2 = [jnp.concatenate([jnp.where(lane < DA_HEAD_DIM, q[:, hs], 0.0),
                           jnp.where(lane >= DA_HEAD_DIM, q[:, hs], 0.0)], axis=0).astype(BF16)
          for hs in head_cols]

    @pl.when(g == 0)
    def _new_tokens():
        pad = jnp.zeros((LANES - t_new, DA_HEAD_W), F32)
        for h, hs in enumerate(head_cols):
            kn = jnp.concatenate([kn_ref[:, hs], pad], axis=0).astype(BF16)
            vn = jnp.concatenate([vn_ref[:, hs], pad], axis=0).astype(BF16)
            s = lax.dot_general(q2[h], kn, _NT, preferred_element_type=F32)
            r = lax.broadcasted_iota(jnp.int32, s.shape, 0)
            c = lax.broadcasted_iota(jnp.int32, s.shape, 1)
            tok = jnp.where(r >= t_new, r - t_new, r)
            s = jnp.where(c <= tok, s, NEG_INF)
            m = jnp.max(s, axis=-1, keepdims=True)
            p = jnp.exp2(s - m)
            m_sc[h] = m
            l_sc[h] = jnp.sum(p, axis=-1, keepdims=True)
            acc_sc[h] = jnp.dot(p.astype(BF16), vn, preferred_element_type=F32)

    for h in range(heads):
        kb = jnp.concatenate([r_[pl.ds(h, page, stride=heads), :] for r_ in k_refs],
                             axis=0).astype(BF16)
        vb = jnp.concatenate([r_[pl.ds(h, page, stride=heads), :] for r_ in v_refs],
                             axis=0).astype(BF16)
        s = lax.dot_general(q2[h], kb, _NT, preferred_element_type=F32)
        m, l, acc = _softmax_update(s, m_sc[h], l_sc[h], acc_sc[h], vb)
        m_sc[h] = m
        l_sc[h] = l
        acc_sc[h] = acc

    @pl.when(g == pl.num_programs(1) - 1)
    def _finish():
        lam = _lambda(lq1_ref, lk1_ref, lq2_ref, lk2_ref, lam_init)
        sw = sw_ref[...]
        for h in range(heads):
            on = acc_sc[h] * (1.0 / l_sc[h])
            o = on[:t_new] - lam * on[t_new:]
            o_ref[:, h * DA_HEAD_W:(h + 1) * DA_HEAD_W] = _sub_ln(o, sw, lam_init)


def _attn_decode(q3, kn3, vn3, cache_k3, cache_v3, page_table, lq1, lk1, lq2, lk2, sw,
                 heads, lam_init):
    nb, t_new, da_w = q3.shape
    n_pages = page_table.shape[1]
    page = cache_k3.shape[1] // heads
    pps = math.gcd(n_pages, 16)
    n_steps = n_pages // pps
    tok = pl.BlockSpec((None, t_new, da_w), lambda b, g, pt: (b, 0, 0))
    small = lambda a: pl.BlockSpec(a.shape, lambda b, g, pt: (0, 0))
    hbm = pl.BlockSpec(memory_space=pl.ANY)
    page_buf = pltpu.VMEM((_PAGE_SLOTS, pps, page * heads, DA_HEAD_W), cache_k3.dtype)

    grid_spec = pltpu.PrefetchScalarGridSpec(
        num_scalar_prefetch=1,
        grid=(nb, n_steps),
        in_specs=[tok, tok, tok, small(lq1), small(lk1), small(lq2), small(lk2), small(sw), hbm, hbm],
        out_specs=tok,
        scratch_shapes=[page_buf, page_buf,
                        pltpu.SemaphoreType.DMA((2, _PAGE_SLOTS)),
                        pltpu.VMEM((heads, 2 * t_new, 1), F32),
                        pltpu.VMEM((heads, 2 * t_new, 1), F32),
                        pltpu.VMEM((heads, 2 * t_new, DA_HEAD_W), F32)],
    )
    return pl.pallas_call(
        functools.partial(_attn_decode_kernel, pages_per_step=pps, heads=heads, page=page,
                          lam_init=lam_init),
        grid_spec=grid_spec,
        out_shape=jax.ShapeDtypeStruct((nb, t_new, da_w), F32),
        compiler_params=_params("arbitrary", "arbitrary"),
    )(page_table, q3, kn3, vn3, lq1, lk1, lq2, lk2, sw, cache_k3, cache_v3)


def _hgrn_kernel(q_ref, f_ref, v_ref, g_ref, s0_ref, nw_ref, o_ref, s_out_ref, st_sc, b_sc,
                 *, heads, n_chunks):
    c_len = HG_CHUNK
    step = pl.program_id(1)

    @pl.when(step == 0)
    def _load_state():
        for h in range(heads):
            st_sc[h] = s0_ref[h].T

    r = lax.broadcasted_iota(jnp.int32, (c_len, c_len), 0)
    c = lax.broadcasted_iota(jnp.int32, (c_len, c_len), 1)
    tri = jnp.where(c <= r, 1.0, 0.0).astype(BF16)

    def midpoint_mask(lv):
        return ((r // lv) == (c // lv)) & ((r % lv) >= lv // 2) & ((c % lv) < lv // 2)

    level_masks = {lv: midpoint_mask(lv) for lv in (16, 32, 64, 128)}
    blk = _HG_FAST_BLOCK
    diag_mask = ((r // blk) == (c // blk)) & (c <= r)
    sub = lax.broadcasted_iota(jnp.int32, (c_len // 8, 8, 1), 1)
    nw = nw_ref[...]

    def cumsum_rows(x):
        x1 = x.astype(BF16)
        r1 = x - x1.astype(F32)
        x2 = r1.astype(BF16)
        x3 = (r1 - x2.astype(F32)).astype(BF16)
        return (jnp.dot(tri, x1, preferred_element_type=F32)
                + jnp.dot(tri, x2, preferred_element_type=F32)
                + jnp.dot(tri, x3, preferred_element_type=F32))

    def midpoint_terms(q, kk, b, levels):
        a_mat = jnp.zeros((c_len, c_len), F32)
        for lv in levels:
            bl = b.reshape(c_len // lv, lv, HG_DK)
            mid = bl[:, lv // 2 - 1:lv // 2, :]
            e = jnp.exp2(-jnp.abs(bl - mid)).reshape(c_len, HG_DK)
            a_lv = lax.dot_general((q * e).astype(BF16), (kk * e).astype(BF16), _NT,
                                   preferred_element_type=F32)
            a_mat = a_mat + jnp.where(level_masks[lv], a_lv, 0.0)
        return a_mat

    def intra_fast(q, kk, v, b, f):
        bl = b.reshape(c_len // blk, blk, HG_DK)
        lf0 = jnp.log2(f.reshape(c_len // blk, blk, HG_DK)[:, 0:1, :])
        d = (bl - (bl[:, 0:1, :] - lf0)).reshape(c_len, HG_DK)
        a_blk = lax.dot_general((q * jnp.exp2(d)).astype(BF16), (kk * jnp.exp2(-d)).astype(BF16),
                                _NT, preferred_element_type=F32)
        a_mat = jnp.where(diag_mask, a_blk, 0.0) + midpoint_terms(q, kk, b, (32, 64, 128))
        return jnp.dot(a_mat.astype(BF16), v.astype(BF16), preferred_element_type=F32)

    def intra_exact(q, kk, v, b, f):
        del f
        q3 = q.reshape(c_len // 8, 8, HG_DK)
        b3 = b.reshape(c_len // 8, 8, HG_DK)
        k3 = kk.reshape(c_len // 8, 8, HG_DK)
        v3 = v.reshape(c_len // 8, 8, HG_DV)
        o3 = jnp.zeros((c_len // 8, 8, HG_DV), F32)
        for s in range(8):
            dec = jnp.exp2(jnp.minimum(b3 - b3[:, s:s + 1, :], 0.0))
            a = jnp.sum(q3 * k3[:, s:s + 1, :] * dec, axis=-1, keepdims=True)
            o3 = o3 + jnp.where(sub >= s, a, 0.0) * v3[:, s:s + 1, :]
        a_mat = midpoint_terms(q, kk, b, (16, 32, 64, 128))
        return o3.reshape(c_len, HG_DV) + jnp.dot(a_mat.astype(BF16), v.astype(BF16),
                                                  preferred_element_type=F32)

    def head_cols(h):
        return slice(h * HG_DK, (h + 1) * HG_DK)

    def prefix(ci, growth):
        r0 = pl.multiple_of(ci * c_len, c_len)
        for h in range(heads):
            lf = jnp.log2(f_ref[pl.ds(r0, c_len), head_cols(h)])
            b = cumsum_rows(lf)
            b_sc[pl.ds(r0, c_len), head_cols(h)] = b
            bl = b.reshape(c_len // blk, blk, HG_DK)
            before = bl[:, 0:1, :] - lf.reshape(c_len // blk, blk, HG_DK)[:, 0:1, :]
            growth = jnp.maximum(growth, before - bl[:, blk - 1:blk, :])
        return growth

    growth = lax.fori_loop(0, n_chunks, prefix, jnp.zeros((c_len // blk, 1, HG_DK), F32))
    safe = jnp.max(growth) < _HG_FAST_LIMIT

    def run_chunks(intra):
        for ci in range(n_chunks):
            rows = pl.ds(ci * c_len, c_len)
            for h in range(heads):
                hs = head_cols(h)
                q = q_ref[rows, hs]
                f = f_ref[rows, hs]
                v = v_ref[rows, hs]
                b = b_sc[rows, hs]
                kk = 1.0 - f
                st = st_sc[h]
                o = intra(q, kk, v, b, f) + lax.dot_general(
                    (q * jnp.exp2(b)).astype(BF16), st.astype(BF16), _NT,
                    preferred_element_type=F32)
                b_end = b[c_len - 1:c_len, :]
                k_end = kk * jnp.exp2(b_end - b)
                st_sc[h] = st * jnp.exp2(b_end) + jnp.dot(
                    v.T.astype(BF16), k_end.astype(BF16), preferred_element_type=F32)
                y = o * lax.rsqrt(jnp.mean(o * o, axis=-1, keepdims=True) + EPS) * nw
                o_ref[rows, hs] = y * g_ref[rows, hs]

    pl.when(safe)(lambda: run_chunks(intra_fast))
    pl.when(jnp.logical_not(safe))(lambda: run_chunks(intra_exact))

    @pl.when(step == pl.num_programs(1) - 1)
    def _store_state():
        for h in range(heads):
            s_out_ref[h] = st_sc[h].T


def _hgrn(hq, f, hi, hg, s0, nw, batch, seq):
    n, hg_w = hq.shape
    heads = hg_w // HG_DK
    tb = HG_CHUNK * math.gcd(seq // HG_CHUNK, 4)
    steps = seq // tb
    row = pl.BlockSpec((tb, hg_w), lambda b, t: (b * steps + t, 0))
    st = pl.BlockSpec((None, heads, HG_DK, HG_DV), lambda b, t: (b, 0, 0, 0))
    return pl.pallas_call(
        functools.partial(_hgrn_kernel, heads=heads, n_chunks=tb // HG_CHUNK),
        grid=(batch, steps),
        in_specs=[row, row, row, row, st, pl.BlockSpec(nw.shape, lambda b, t: (0, 0))],
        out_specs=[row, st],
        out_shape=[jax.ShapeDtypeStruct((n, hg_w), F32),
                   jax.ShapeDtypeStruct(s0.shape, F32)],
        scratch_shapes=[pltpu.VMEM((heads, HG_DV, HG_DK), F32), pltpu.VMEM((tb, hg_w), F32)],
        compiler_params=_params("parallel", "arbitrary"),
    )(hq, f, hi, hg, s0, nw)


_HG_SHORT = 8


def _hgrn_short_kernel(q_ref, f_ref, v_ref, g_ref, s0_ref, nw_ref, o_ref, s_out_ref, *, heads):
    t = _HG_SHORT
    row = lax.broadcasted_iota(jnp.int32, (t, 1), 0)
    pad8 = jnp.zeros((t, HG_DK), F32)
    pad = jnp.zeros((LANES - t, HG_DK), F32)
    nw = nw_ref[...]
    for h in range(heads):
        hs = slice(h * HG_DK, (h + 1) * HG_DK)
        q = q_ref[:, hs]
        f = f_ref[:, hs]
        v = v_ref[:, hs]
        kk = 1.0 - f
        lf = jnp.log2(f)
        b = jnp.zeros_like(lf)
        for s in range(t):
            b = b + jnp.where(row >= s, lf[s:s + 1, :], 0.0)
        o = jnp.zeros((t, HG_DV), F32)
        for s in range(t):
            dec = jnp.exp2(jnp.minimum(b - b[s:s + 1, :], 0.0))
            a = jnp.sum(q * kk[s:s + 1, :] * dec, axis=-1, keepdims=True)
            o = o + jnp.where(row >= s, a, 0.0) * v[s:s + 1, :]
        st = s0_ref[h].T
        q_dec = jnp.concatenate([q * jnp.exp2(b), pad8], axis=0).astype(BF16)
        o = o + lax.dot_general(q_dec, st.astype(BF16), _NT, preferred_element_type=F32)[:t]
        b_end = b[t - 1:t, :]
        k_end = jnp.concatenate([kk * jnp.exp2(b_end - b), pad], axis=0)
        v_pad = jnp.concatenate([v, pad], axis=0)
        st_new = st * jnp.exp2(b_end) + jnp.dot(v_pad.T.astype(BF16), k_end.astype(BF16),
                                                preferred_element_type=F32)
        s_out_ref[h] = st_new.T
        y = o * lax.rsqrt(jnp.mean(o * o, axis=-1, keepdims=True) + EPS) * nw
        o_ref[:, hs] = y * g_ref[:, hs]


def _hgrn_short(hq, f, hi, hg, s0, nw, batch):
    n, hg_w = hq.shape
    heads = hg_w // HG_DK
    r3 = lambda a: a.reshape(batch, _HG_SHORT, hg_w)
    tok = pl.BlockSpec((None, _HG_SHORT, hg_w), lambda b: (b, 0, 0))
    st = pl.BlockSpec((None, heads, HG_DK, HG_DV), lambda b: (b, 0, 0, 0))
    o, s_new = pl.pallas_call(
        functools.partial(_hgrn_short_kernel, heads=heads),
        grid=(batch,),
        in_specs=[tok, tok, tok, tok, st, pl.BlockSpec(nw.shape, lambda b: (0, 0))],
        out_specs=[tok, st],
        out_shape=[jax.ShapeDtypeStruct((batch, _HG_SHORT, hg_w), F32),
                   jax.ShapeDtypeStruct(s0.shape, F32)],
        compiler_params=_params("parallel"),
    )(r3(hq), r3(f), r3(hi), r3(hg), s0, nw)
    return o.reshape(n, hg_w), s_new


def _outproj_kernel(oda_ref, ohg_ref, x_ref, wo_ref, n2_ref, wrh_ref, wrl_ref, br_ref,
                    h_ref, xn_ref, gates_ref, *, da_w):
    o = (jnp.dot(oda_ref[...].astype(BF16), wo_ref[:da_w, :], preferred_element_type=F32)
         + jnp.dot(ohg_ref[...].astype(BF16), wo_ref[da_w:, :], preferred_element_type=F32))
    h = x_ref[...] + o
    h_ref[...] = h
    xn = h * lax.rsqrt(jnp.mean(h * h, axis=-1, keepdims=True) + EPS) * n2_ref[...]
    xh = xn.astype(BF16)
    xn_ref[...] = xh
    xl = (xn - xh.astype(F32)).astype(BF16)
    wrh = wrh_ref[...]
    both = jnp.dot(xh, jnp.concatenate([wrh, wrl_ref[...]], axis=-1), preferred_element_type=F32)
    logits = (both[:, :LANES] + jnp.dot(xl, wrh, preferred_element_type=F32)
              + both[:, LANES:]) + br_ref[...]
    lane = lax.broadcasted_iota(jnp.int32, logits.shape, 1)
    lane_f = lane.astype(F32)
    big = float(LANES)
    is_g = (lane >= N_EXPERTS) & (lane < N_EXPERTS + N_GROUPS)
    gl = jnp.where(is_g, logits, NEG_INF)
    g_max = jnp.max(gl, axis=-1, keepdims=True)
    g_idx = jnp.min(jnp.where(gl == g_max, lane_f, big), axis=-1, keepdims=True) - N_EXPERTS
    g_w = 1.0 / jnp.sum(jnp.exp(gl - g_max), axis=-1, keepdims=True)
    grp_of_lane = (lane // EXPERTS_PER_GROUP).astype(F32)
    sel = (lane < N_EXPERTS) & (grp_of_lane == g_idx)
    el = jnp.where(sel, logits, NEG_INF)
    v1 = jnp.max(el, axis=-1, keepdims=True)
    i1 = jnp.min(jnp.where(sel, jnp.where(el == v1, lane_f, big), big), axis=-1, keepdims=True)
    el2 = jnp.where(lane_f == i1, NEG_INF, el)
    v2 = jnp.max(el2, axis=-1, keepdims=True)
    sel2 = sel & (lane_f != i1)
    i2 = jnp.min(jnp.where(sel2, jnp.where(el2 == v2, lane_f, big), big), axis=-1, keepdims=True)
    t = jnp.exp(v2 - v1)
    p1 = 1.0 / (1.0 + t)
    p2 = t * p1
    gates_ref[...] = (jnp.where(lane_f == i1, p1 * g_w, 0.0)
                      + jnp.where(lane_f == i2, p2 * g_w, 0.0))


def _outproj(oda, ohg, x2d, wo_bf, n2, wr_hi, wr_lo, br):
    n, d = x2d.shape
    da_w = oda.shape[1]
    tm = _row_tile(n, 512)
    row = lambda w: pl.BlockSpec((tm, w), lambda i: (i, 0))
    full = lambda a: pl.BlockSpec(a.shape, lambda i: (0, 0))
    return pl.pallas_call(
        functools.partial(_outproj_kernel, da_w=da_w),
        grid=(n // tm,),
        in_specs=[row(da_w), row(ohg.shape[1]), row(d), full(wo_bf), full(n2), full(wr_hi),
                  full(wr_lo), full(br)],
        out_specs=[row(d), row(d), row(LANES)],
        out_shape=[jax.ShapeDtypeStruct((n, d), F32), jax.ShapeDtypeStruct((n, d), BF16),
                   jax.ShapeDtypeStruct((n, LANES), F32)],
        compiler_params=_params("parallel"),
    )(oda, ohg, x2d, wo_bf, n2, wr_hi, wr_lo, br)


_MOE_EXPERTS_PER_STEP = 4


def _moe_kernel(x_ref, h_ref, gates_ref, wg_ref, wu_ref, wd_ref, y_ref):
    eb = pl.program_id(1)
    per_step = wg_ref.shape[0]

    @pl.when(eb == 0)
    def _init():
        y_ref[...] = h_ref[...]

    x = x_ref[...]
    gates = gates_ref[...]
    lane = lax.broadcasted_iota(jnp.int32, gates.shape, 1)
    hid = []
    for j in range(per_step):
        a = jnp.dot(x, wg_ref[j], preferred_element_type=F32)
        u = jnp.dot(x, wu_ref[j], preferred_element_type=F32)
        gate = jnp.sum(jnp.where(lane == eb * per_step + j, gates, 0.0), axis=-1, keepdims=True)
        hid.append(((a * _sigmoid(a)) * u * gate).astype(BF16))
    y_ref[...] += jnp.dot(jnp.concatenate(hid, axis=-1), wd_ref[...], preferred_element_type=F32)


def _moe(xn_bf, h, gates, wg_bf, wu_bf, wd_bf):
    n, d = h.shape
    n_exp, _, ff = wg_bf.shape
    per_step = math.gcd(n_exp, _MOE_EXPERTS_PER_STEP)
    tm = _row_tile(n, 1024)
    row = lambda w: pl.BlockSpec((tm, w), lambda i, e: (i, 0))
    return pl.pallas_call(
        _moe_kernel,
        grid=(n // tm, n_exp // per_step),
        in_specs=[row(d), row(d), row(LANES),
                  pl.BlockSpec((per_step, d, ff), lambda i, e: (e, 0, 0)),
                  pl.BlockSpec((per_step, d, ff), lambda i, e: (e, 0, 0)),
                  pl.BlockSpec((per_step * ff, d), lambda i, e: (e, 0))],
        out_specs=row(d),
        out_shape=jax.ShapeDtypeStruct((n, d), F32),
        compiler_params=_params("parallel", "arbitrary"),
    )(xn_bf, h, gates, wg_bf, wu_bf, wd_bf.reshape(n_exp * ff, d))


def _pad_tokens(a, batch, seq, seq_pad, value):
    a3 = a.reshape(batch, seq, a.shape[-1])
    a3 = jnp.pad(a3, ((0, 0), (0, seq_pad - seq), (0, 0)), constant_values=value)
    return a3.reshape(batch * seq_pad, a.shape[-1])


def _layer(x, s0, past, layer_idx, w):
    batch, seq, d = x.shape
    n = batch * seq
    x2d = x.reshape(n, d)
    da_w = w["qw"].shape[1]
    heads = da_w // DA_HEAD_W
    lam_init = 0.8 - 0.6 * math.exp(-0.3 * layer_idx)
    lam_args = (w["lq1"], w["lk1"], w["lq2"], w["lk2"], w["sw"])

    q, k, v, hq, f, hi, hg, *attn_in = _inproj(x2d, w["n1"], w["w_in"], w["qw"], w["kw"], w["lb"],
                                               w["gm"], past is None)
    if past is None:
        oda = _attn_prompt(q, *attn_in, *lam_args, batch, seq, lam_init)
    else:
        cache_k3, cache_v3, page_table = past
        r3 = lambda a: a.reshape(batch, seq, da_w)
        oda = _attn_decode(r3(q), r3(k), r3(v), cache_k3, cache_v3, page_table, *lam_args,
                           heads, lam_init).reshape(n, da_w)

    seq_pad = -(-seq // HG_CHUNK) * HG_CHUNK
    if seq == _HG_SHORT:
        ohg, s_new = _hgrn_short(hq, f, hi, hg, s0, w["hnw"], batch)
    elif seq_pad != seq:
        hq_p, hi_p, hg_p = (_pad_tokens(a, batch, seq, seq_pad, 0.0) for a in (hq, hi, hg))
        f_p = _pad_tokens(f, batch, seq, seq_pad, 1.0)
        ohg, s_new = _hgrn(hq_p, f_p, hi_p, hg_p, s0, w["hnw"], batch, seq_pad)
        ohg = ohg.reshape(batch, seq_pad, -1)[:, :seq].reshape(n, -1)
    else:
        ohg, s_new = _hgrn(hq, f, hi, hg, s0, w["hnw"], batch, seq)

    h, xn_bf, gates = _outproj(oda, ohg, x2d, w["w_out"], w["n2"], w["wr_hi"], w["wr_lo"], w["br"])
    y = _moe(xn_bf, h, gates, w["wg"], w["wu"], w["wd"])
    kv_shape = (batch, seq, heads, DA_HEAD_W)
    return y.reshape(batch, seq, d), k.reshape(kv_shape), v.reshape(kv_shape), s_new


def _layer_weights(l, lb_all, norm1_w, w_in, q_norm_w, k_norm_w, lambda_q1, lambda_k1, lambda_q2,
                   lambda_k2, subln_w, hg_norm_w, w_out, norm2_w, w_router_grp, b_router_grp,
                   w_router_exp, b_router_exp, w_gate, w_up, w_down):
    d = w_in.shape[1]
    da_w = w_out.shape[1] // 2
    n_grp_norm = da_w // DA_HEAD_DIM
    row = lambda a: a.reshape(1, -1).astype(F32)
    g_id = jnp.arange(math.gcd(da_w, 2 * LANES)) // DA_HEAD_DIM
    gm = jnp.where(g_id[:, None] == g_id[None, :], 1.0 / DA_HEAD_DIM, 0.0).astype(BF16)
    wr = jnp.zeros((d, LANES), F32)
    wr = wr.at[:, :N_EXPERTS].set(w_router_exp[l]).at[:, N_EXPERTS:N_EXPERTS + N_GROUPS].set(
        w_router_grp[l])
    wr_hi = wr.astype(BF16)
    wr_lo = (wr - wr_hi.astype(F32)).astype(BF16)
    br = jnp.zeros((1, LANES), F32)
    br = br.at[0, :N_EXPERTS].set(b_router_exp[l]).at[0, N_EXPERTS:N_EXPERTS + N_GROUPS].set(
        b_router_grp[l])
    return dict(
        n1=row(norm1_w[l]), w_in=w_in[l].astype(BF16),
        qw=row(jnp.tile(q_norm_w[l], n_grp_norm)), kw=row(jnp.tile(k_norm_w[l], n_grp_norm)),
        lb=row(lb_all[l]), gm=gm,
        lq1=row(lambda_q1[l]), lk1=row(lambda_k1[l]), lq2=row(lambda_q2[l]), lk2=row(lambda_k2[l]),
        sw=row(subln_w[l]), hnw=row(hg_norm_w[l]),
        w_out=w_out[l].astype(BF16), n2=row(norm2_w[l]), wr_hi=wr_hi, wr_lo=wr_lo, br=br,
        wg=w_gate[l].astype(BF16), wu=w_up[l].astype(BF16), wd=w_down[l].astype(BF16),
    )


def kernel(x_prompt, x_sample, cache_k, cache_v, state_hgrn, page_table, norm1_w, w_in, q_norm_w, k_norm_w, lambda_q1, lambda_k1, lambda_q2, lambda_k2, subln_w, hg_lb, hg_norm_w, w_out, norm2_w, w_router_grp, b_router_grp, w_router_exp, b_router_exp, w_gate, w_up, w_down):
    depth = norm1_w.shape[0]
    n_phys, page, heads, head_w = cache_k.shape[1:]
    lb_all = jnp.cumsum(jax.nn.softmax(hg_lb.astype(F32), axis=0), axis=0)
    yp, ys = x_prompt, x_sample
    outs = [[] for _ in range(6)]
    for l in range(depth):
        w = _layer_weights(l, lb_all, norm1_w, w_in, q_norm_w, k_norm_w, lambda_q1, lambda_k1,
                           lambda_q2, lambda_k2, subln_w, hg_norm_w, w_out, norm2_w,
                           w_router_grp, b_router_grp, w_router_exp, b_router_exp,
                           w_gate, w_up, w_down)
        s0p = jnp.zeros((x_prompt.shape[0],) + state_hgrn.shape[2:], F32)
        yp, k_r, v_r, s_r = _layer(yp, s0p, None, l, w)
        past = (cache_k[l].reshape(n_phys, page * heads, head_w),
                cache_v[l].reshape(n_phys, page * heads, head_w), page_table)
        ys, k_s, v_s, s_s = _layer(ys, state_hgrn[l], past, l, w)
        for lst, val in zip(outs, (k_r, v_r, s_r, k_s, v_s, s_s)):
            lst.append(val)
    kp, vp, sp, kss, vss, sss = (jnp.stack(o) for o in outs)
    return (yp, ys, kp, vp, sp, kss, vss, sss)
```

```python
import functools
import math

import jax
import jax.numpy as jnp
from jax import lax
from jax.experimental import pallas as pl
from jax.experimental.pallas import tpu as pltpu

F32 = jnp.float32
BF16 = jnp.bfloat16

DA_HEAD_DIM = 64
DA_HEAD_W = 2 * DA_HEAD_DIM
HG_DK = 128
HG_DV = 128
HG_CHUNK = 128
_HG_FAST_BLOCK = 16
_HG_FAST_LIMIT = 60.0
N_GROUPS = 4
EXPERTS_PER_GROUP = 4
N_EXPERTS = N_GROUPS * EXPERTS_PER_GROUP
EPS = 1e-6
NEG_INF = -1e30
_Q_SCALE = DA_HEAD_DIM ** -0.5 * math.log2(math.e)
LANES = 128
VMEM_LIMIT = 56 * 1024 * 1024

_NT = (((1,), (1,)), ((), ()))


def _row_tile(n, pref):
    for t in range(min(pref, n), 7, -1):
        if n % t == 0 and t % 8 == 0:
            return t
    return n


def _sigmoid(x):
    return 1.0 / (1.0 + jnp.exp(-x))


def _params(*sem):
    return pltpu.CompilerParams(dimension_semantics=sem, vmem_limit_bytes=VMEM_LIMIT)


def _inproj_kernel(x_ref, n1_ref, w_ref, qw_ref, kw_ref, lb_ref, gm_ref,
                   q_ref, k_ref, v_ref, hq_ref, f_ref, hi_ref, hg_ref, *attn_refs, da_w, hg_w):
    x = x_ref[...]
    xn = x * lax.rsqrt(jnp.mean(x * x, axis=-1, keepdims=True) + EPS) * n1_ref[...]
    xb = xn.astype(BF16)

    def proj(lo, width):
        return jnp.dot(xb, w_ref[:, lo:lo + width], preferred_element_type=F32)

    gm = gm_ref[...]

    def group_rms(t, w):
        sq = (t * t).astype(BF16)
        gw = gm.shape[0]
        ms = jnp.concatenate([jnp.dot(sq[:, lo:lo + gw], gm, preferred_element_type=F32)
                              for lo in range(0, t.shape[1], gw)], axis=-1)
        return t * lax.rsqrt(ms + EPS) * w

    q_ref[...] = (group_rms(proj(0, da_w), qw_ref[...]) * _Q_SCALE).astype(q_ref.dtype)
    k = group_rms(proj(da_w, da_w), kw_ref[...])
    v = proj(2 * da_w, da_w)
    if attn_refs:
        heads = da_w // DA_HEAD_W
        rows = k.shape[0]
        for h in range(heads):
            hs = slice(h * DA_HEAD_W, (h + 1) * DA_HEAD_W)
            k_ref[pl.ds(h, rows, stride=heads), :] = k[:, hs]
            v_ref[pl.ds(h, rows, stride=heads), :] = v[:, hs]
        kb_ref, vt_ref = attn_refs
        kb_ref[...] = k.astype(BF16)
        vt_ref[...] = v.T.astype(BF16)
    else:
        k_ref[...] = k
        v_ref[...] = v
    base = 3 * da_w
    hq = proj(base, hg_w)
    hq_ref[...] = hq * _sigmoid(hq)
    lb = lb_ref[...]
    f_ref[...] = lb + (1.0 - lb) * _sigmoid(proj(base + hg_w, hg_w))
    hi_ref[...] = proj(base + 2 * hg_w, hg_w)
    hg = proj(base + 3 * hg_w, hg_w)
    hg_ref[...] = hg * _sigmoid(hg)


def _inproj(x2d, n1, w_in_bf, qw, kw, lb, gm, for_prompt):
    n, d = x2d.shape
    da_w = qw.shape[1]
    hg_w = lb.shape[1]
    tm = _row_tile(n, 512)
    row = lambda w: pl.BlockSpec((tm, w), lambda i: (i, 0))
    full = lambda a: pl.BlockSpec(a.shape, lambda i: (0, 0))
    heads = da_w // DA_HEAD_W
    if for_prompt:
        kv_shape = jax.ShapeDtypeStruct((n * heads, DA_HEAD_W), F32)
        kv_spec = pl.BlockSpec((tm * heads, DA_HEAD_W), lambda i: (i, 0))
    else:
        kv_shape = jax.ShapeDtypeStruct((n, da_w), F32)
        kv_spec = row(da_w)
    out_shapes = [jax.ShapeDtypeStruct((n, da_w), BF16 if for_prompt else F32), kv_shape, kv_shape]
    out_shapes += [jax.ShapeDtypeStruct((n, hg_w), F32)] * 4
    out_specs = [row(da_w), kv_spec, kv_spec] + [row(hg_w)] * 4
    if for_prompt:
        out_shapes += [jax.ShapeDtypeStruct((n, da_w), BF16), jax.ShapeDtypeStruct((da_w, n), BF16)]
        out_specs += [row(da_w), pl.BlockSpec((da_w, tm), lambda i: (0, i))]
    return pl.pallas_call(
        functools.partial(_inproj_kernel, da_w=da_w, hg_w=hg_w),
        grid=(n // tm,),
        in_specs=[row(d), full(n1), full(w_in_bf), full(qw), full(kw), full(lb), full(gm)],
        out_specs=out_specs,
        out_shape=out_shapes,
        compiler_params=_params("parallel"),
    )(x2d, n1, w_in_bf, qw, kw, lb, gm)


def _lambda(lq1_ref, lk1_ref, lq2_ref, lk2_ref, lam_init):
    s1 = jnp.sum(lq1_ref[...] * lk1_ref[...], axis=-1, keepdims=True)
    s2 = jnp.sum(lq2_ref[...] * lk2_ref[...], axis=-1, keepdims=True)
    return jnp.exp(s1) - jnp.exp(s2) + lam_init


def _softmax_update(s, m, l, acc, vb):
    m_new = jnp.maximum(m, jnp.max(s, axis=-1, keepdims=True))
    p = jnp.exp2(s - m_new)
    alpha = jnp.exp2(m - m_new)
    l_new = alpha * l + jnp.sum(p, axis=-1, keepdims=True)
    acc_new = alpha * acc + jnp.dot(p.astype(BF16), vb, preferred_element_type=F32)
    return m_new, l_new, acc_new


def _sub_ln(o, sw, lam_init):
    y = o * lax.rsqrt(jnp.mean(o * o, axis=-1, keepdims=True) + EPS)
    return y * sw * (1.0 - lam_init)


_ONES_ROWS = 16
_NEXT_TILE_SLOT = 2
_Q_COLS = 256


def _attn_prompt_kernel(q_ref, k_ref, vt_ref, lq1_ref, lk1_ref, lq2_ref, lk2_ref, swc_ref,
                        o_ref, s_sc, mx_sc, acc_sc, *, blk, lam_init):
    qi = pl.program_id(2)
    lane = lax.broadcasted_iota(jnp.int32, (blk, DA_HEAD_W), 1)
    zero = jnp.zeros((blk, DA_HEAD_W), BF16)
    ones = jnp.ones((_ONES_ROWS, blk), BF16)

    def query_maps(tile):
        q = q_ref[pl.ds(pl.multiple_of(tile * blk, blk), blk), :]
        return (jnp.where(lane < DA_HEAD_DIM, q, zero),
                jnp.where(lane >= DA_HEAD_DIM, q, zero))

    q_maps = query_maps(qi)

    n_cb = max(blk // _Q_COLS, 1)
    cb_w = blk // n_cb

    def step(carry, consume_c, consume_slot, diagonal=False, stage=None, stage_diagonal=False):
        vt1 = jnp.concatenate([vt_ref[:, pl.ds(pl.multiple_of(consume_c * blk, blk), blk)], ones],
                              axis=0)
        if stage is not None:
            stage_c, stage_slot, maps = stage
            kb = k_ref[pl.ds(pl.multiple_of(stage_c * blk, blk), blk), :]
        m_new = ([], [])
        n_kh = max(blk // _Q_COLS, 1)
        kh_w = blk // n_kh

        def masked(s, row0, col0):
            key = lax.broadcasted_iota(jnp.int32, s.shape, 0) + row0
            qry = lax.broadcasted_iota(jnp.int32, s.shape, 1) + col0
            return jnp.where(key <= qry, s, jnp.full(s.shape, NEG_INF, BF16))

        for cb in range(n_cb):
            cols = slice(cb * cb_w, (cb + 1) * cb_w)
            for mp in range(2):
                if diagonal:
                    live = slice(0, (cb + 1) * kh_w)
                    s_max = jnp.max(masked(s_sc[consume_slot, mp, live, cols], 0, cb * cb_w),
                                    axis=0, keepdims=True)
                else:
                    s_max = mx_sc[consume_slot, mp, :, cols].astype(BF16)
                m_old = carry[mp][:, cols]
                m = jnp.maximum(m_old, s_max)
                alpha = jnp.exp2(m_old.astype(F32) - m.astype(F32))
                pv = None
                staged_max = None
                for kh in range(n_kh):
                    rows = slice(kh * kh_w, (kh + 1) * kh_w)
                    if stage is not None and not (stage_diagonal and kh > cb):
                        s = lax.dot_general(kb[rows], maps[mp][cols], _NT,
                                            preferred_element_type=F32).astype(BF16)
                        s_sc[stage_slot, mp, rows, cols] = s
                        unit_max = jnp.max(s, axis=0, keepdims=True)
                        staged_max = unit_max if staged_max is None else jnp.maximum(staged_max,
                                                                                     unit_max)
                    if diagonal and kh > cb:
                        continue
                    s = s_sc[consume_slot, mp, rows, cols]
                    if diagonal and kh == cb:
                        s = masked(s, kh * kh_w, cb * cb_w)
                    p = jnp.exp2(s - m)
                    d = jnp.dot(vt1[:, rows], p, preferred_element_type=F32)
                    pv = d if pv is None else pv + d
                acc_sc[mp, :, cols] = alpha * acc_sc[mp, :, cols] + pv
                if stage is not None:
                    mx_sc[stage_slot, mp, :, cols] = staged_max.astype(F32)
                m_new[mp].append(m)
        return tuple(jnp.concatenate(ms, axis=1) for ms in m_new)

    def stage_only(c, slot, maps):
        kb = k_ref[pl.ds(pl.multiple_of(c * blk, blk), blk), :]
        for mp in range(2):
            s = lax.dot_general(kb, maps[mp], _NT, preferred_element_type=F32).astype(BF16)
            s_sc[slot, mp] = s
            mx_sc[slot, mp] = jnp.max(s, axis=0, keepdims=True).astype(F32)

    @pl.when(qi == 0)
    def _first_tile():
        stage_only(0, 0, q_maps)

    @pl.when(qi > 0)
    def _staged_by_previous_tile():
        s_sc[0] = s_sc[_NEXT_TILE_SLOT]
        mx_sc[0] = mx_sc[_NEXT_TILE_SLOT]

    def chunk_pair(j, c):
        c = step(c, 2 * j, 0, stage=(2 * j + 1, 1, q_maps))
        return step(c, 2 * j + 1, 1, stage=(2 * j + 2, 0, q_maps))

    m0 = jnp.full((1, blk), NEG_INF, BF16)
    acc_sc[...] = jnp.zeros(acc_sc.shape, F32)
    carry = lax.fori_loop(0, qi // 2, chunk_pair, (m0, m0))

    next_tile = (0, _NEXT_TILE_SLOT, query_maps(jnp.minimum(qi + 1, pl.num_programs(2) - 1)))

    def tail_odd(c):
        c = step(c, qi - 1, 0, stage=(qi, 1, q_maps), stage_diagonal=True)
        return step(c, qi, 1, diagonal=True, stage=next_tile)

    def tail_even(c):
        return step(c, qi, 0, diagonal=True, stage=next_tile)

    lax.cond(qi % 2 == 1, tail_odd, tail_even, carry)
    a1 = acc_sc[0]
    a2 = acc_sc[1]
    lam = _lambda(lq1_ref, lk1_ref, lq2_ref, lk2_ref, lam_init)
    o1 = a1[:DA_HEAD_W] * (1.0 / a1[DA_HEAD_W:DA_HEAD_W + 1])
    o2 = a2[:DA_HEAD_W] * (1.0 / a2[DA_HEAD_W:DA_HEAD_W + 1])
    o = o1 - lam * o2
    y = o * lax.rsqrt(jnp.mean(o * o, axis=0, keepdims=True) + EPS) * swc_ref[...]
    o_ref[...] = (y * (1.0 - lam_init)).T


def _attn_prompt(q, kb, vt, lq1, lk1, lq2, lk2, sw, batch, seq, lam_init):
    n, da_w = q.shape
    heads = da_w // DA_HEAD_W
    blk = _row_tile(seq, 512)
    nq = seq // blk
    swc = sw.reshape(-1, 1)
    small = lambda a: pl.BlockSpec(a.shape, lambda b, h, i: (0, 0))
    return pl.pallas_call(
        functools.partial(_attn_prompt_kernel, blk=blk, lam_init=lam_init),
        grid=(batch, heads, nq),
        in_specs=[
            pl.BlockSpec((seq, DA_HEAD_W), lambda b, h, i: (b, h)),
            pl.BlockSpec((seq, DA_HEAD_W), lambda b, h, i: (b, h)),
            pl.BlockSpec((DA_HEAD_W, seq), lambda b, h, i: (h, b)),
            small(lq1), small(lk1), small(lq2), small(lk2), small(swc),
        ],
        out_specs=pl.BlockSpec((blk, DA_HEAD_W), lambda b, h, i: (b * nq + i, h)),
        out_shape=jax.ShapeDtypeStruct((n, da_w), F32),
        scratch_shapes=[pltpu.VMEM((_NEXT_TILE_SLOT + 1, 2, blk, blk), BF16),
                        pltpu.VMEM((_NEXT_TILE_SLOT + 1, 2, 1, blk), F32),
                        pltpu.VMEM((2, DA_HEAD_W + _ONES_ROWS, blk), F32)],
        compiler_params=_params("parallel", "parallel", "arbitrary"),
    )(q, kb, vt, lq1, lk1, lq2, lk2, swc)


_PAGE_SLOTS = 4


def _attn_decode_kernel(pt_ref, q_ref, kn_ref, vn_ref, lq1_ref, lk1_ref, lq2_ref, lk2_ref,
                        sw_ref, ck_hbm, cv_hbm, o_ref, kbuf, vbuf, sem, m_sc, l_sc, acc_sc,
                        *, pages_per_step, heads, page, lam_init):
    g = pl.program_id(1)
    n_steps = pl.num_programs(1)
    step = pl.program_id(0) * n_steps + g
    total = pl.num_programs(0) * n_steps
    slot = step % _PAGE_SLOTS

    def page_copies(b, gg, sl):
        copies = []
        for j in range(pages_per_step):
            pid = pt_ref[b, gg * pages_per_step + j]
            copies.append((pltpu.make_async_copy(ck_hbm.at[pid], kbuf.at[sl, j], sem.at[0, sl]), 0))
            copies.append((pltpu.make_async_copy(cv_hbm.at[pid], vbuf.at[sl, j], sem.at[1, sl]), 1))
        return copies

    def fetch(st):
        for cp, prio in page_copies(st // n_steps, st % n_steps, st % _PAGE_SLOTS):
            cp.start(priority=prio)

    @pl.when(step == 0)
    def _first_fetches():
        for ahead in range(_PAGE_SLOTS - 1):
            pl.when(ahead < total)(lambda ahead=ahead: fetch(ahead))

    @pl.when(step + (_PAGE_SLOTS - 1) < total)
    def _prefetch():
        fetch(step + (_PAGE_SLOTS - 1))

    for cp, _ in page_copies(pl.program_id(0), g, slot):
        cp.wait()
    k_refs = [kbuf.at[slot, j] for j in range(pages_per_step)]
    v_refs = [vbuf.at[slot, j] for j in range(pages_per_step)]
    q = q_ref[...]
    t_new = q.shape[0]
    lane = lax.broadcasted_iota(jnp.int32, (t_new, DA_HEAD_W), 1)
    head_cols = [slice(h * DA_HEAD_W, (h + 1) * DA_HEAD_W) for h in range(heads)]
    q2 = [jnp.concatenate([jnp.where(lane < DA_HEAD_DIM, q[:, hs], 0.0),
                           jnp.where(lane >= DA_HEAD_DIM, q[:, hs], 0.0)], axis=0).astype(BF16)
          for hs in head_cols]

    @pl.when(g == 0)
    def _new_tokens():
        pad = jnp.zeros((LANES - t_new, DA_HEAD_W), F32)
        for h, hs in enumerate(head_cols):
            kn = jnp.concatenate([kn_ref[:, hs], pad], axis=0).astype(BF16)
            vn = jnp.concatenate([vn_ref[:, hs], pad], axis=0).astype(BF16)
            s = lax.dot_general(q2[h], kn, _NT, preferred_element_type=F32)
            r = lax.broadcasted_iota(jnp.int32, s.shape, 0)
            c = lax.broadcasted_iota(jnp.int32, s.shape, 1)
            tok = jnp.where(r >= t_new, r - t_new, r)
            s = jnp.where(c <= tok, s, NEG_INF)
            m = jnp.max(s, axis=-1, keepdims=True)
            p = jnp.exp2(s - m)
            m_sc[h] = m
            l_sc[h] = jnp.sum(p, axis=-1, keepdims=True)
            acc_sc[h] = jnp.dot(p.astype(BF16), vn, preferred_element_type=F32)

    for h in range(heads):
        kb = jnp.concatenate([r_[pl.ds(h, page, stride=heads), :] for r_ in k_refs],
                             axis=0).astype(BF16)
        vb = jnp.concatenate([r_[pl.ds(h, page, stride=heads), :] for r_ in v_refs],
                             axis=0).astype(BF16)
        s = lax.dot_general(q2[h], kb, _NT, preferred_element_type=F32)
        m, l, acc = _softmax_update(s, m_sc[h], l_sc[h], acc_sc[h], vb)
        m_sc[h] = m
        l_sc[h] = l
        acc_sc[h] = acc

    @pl.when(g == pl.num_programs(1) - 1)
    def _finish():
        lam = _lambda(lq1_ref, lk1_ref, lq2_ref, lk2_ref, lam_init)
        sw = sw_ref[...]
        for h in range(heads):
            on = acc_sc[h] * (1.0 / l_sc[h])
            o = on[:t_new] - lam * on[t_new:]
            o_ref[:, h * DA_HEAD_W:(h + 1) * DA_HEAD_W] = _sub_ln(o, sw, lam_init)


def _attn_decode(q3, kn3, vn3, cache_k3, cache_v3, page_table, lq1, lk1, lq2, lk2, sw,
                 heads, lam_init):
    nb, t_new, da_w = q3.shape
    n_pages = page_table.shape[1]
    page = cache_k3.shape[1] // heads
    pps = math.gcd(n_pages, 16)
    n_steps = n_pages // pps
    tok = pl.BlockSpec((None, t_new, da_w), lambda b, g, pt: (b, 0, 0))
    small = lambda a: pl.BlockSpec(a.shape, lambda b, g, pt: (0, 0))
    hbm = pl.BlockSpec(memory_space=pl.ANY)
    page_buf = pltpu.VMEM((_PAGE_SLOTS, pps, page * heads, DA_HEAD_W), cache_k3.dtype)

    grid_spec = pltpu.PrefetchScalarGridSpec(
        num_scalar_prefetch=1,
        grid=(nb, n_steps),
        in_specs=[tok, tok, tok, small(lq1), small(lk1), small(lq2), small(lk2), small(sw), hbm, hbm],
        out_specs=tok,
        scratch_shapes=[page_buf, page_buf,
                        pltpu.SemaphoreType.DMA((2, _PAGE_SLOTS)),
                        pltpu.VMEM((heads, 2 * t_new, 1), F32),
                        pltpu.VMEM((heads, 2 * t_new, 1), F32),
                        pltpu.VMEM((heads, 2 * t_new, DA_HEAD_W), F32)],
    )
    return pl.pallas_call(
        functools.partial(_attn_decode_kernel, pages_per_step=pps, heads=heads, page=page,
                          lam_init=lam_init),
        grid_spec=grid_spec,
        out_shape=jax.ShapeDtypeStruct((nb, t_new, da_w), F32),
        compiler_params=_params("arbitrary", "arbitrary"),
    )(page_table, q3, kn3, vn3, lq1, lk1, lq2, lk2, sw, cache_k3, cache_v3)


def _hgrn_kernel(q_ref, f_ref, v_ref, g_ref, s0_ref, nw_ref, o_ref, s_out_ref, st_sc, b_sc,
                 *, heads, n_chunks):
    c_len = HG_CHUNK
    step = pl.program_id(1)

    @pl.when(step == 0)
    def _load_state():
        for h in range(heads):
            st_sc[h] = s0_ref[h].T

    r = lax.broadcasted_iota(jnp.int32, (c_len, c_len), 0)
    c = lax.broadcasted_iota(jnp.int32, (c_len, c_len), 1)
    tri = jnp.where(c <= r, 1.0, 0.0).astype(BF16)

    def midpoint_mask(lv):
        return ((r // lv) == (c // lv)) & ((r % lv) >= lv // 2) & ((c % lv) < lv // 2)

    level_masks = {lv: midpoint_mask(lv) for lv in (16, 32, 64, 128)}
    blk = _HG_FAST_BLOCK
    diag_mask = ((r // blk) == (c // blk)) & (c <= r)
    sub = lax.broadcasted_iota(jnp.int32, (c_len // 8, 8, 1), 1)
    nw = nw_ref[...]

    def cumsum_rows(x):
        x1 = x.astype(BF16)
        r1 = x - x1.astype(F32)
        x2 = r1.astype(BF16)
        x3 = (r1 - x2.astype(F32)).astype(BF16)
        return (jnp.dot(tri, x1, preferred_element_type=F32)
                + jnp.dot(tri, x2, preferred_element_type=F32)
                + jnp.dot(tri, x3, preferred_element_type=F32))

    def midpoint_term(q, kk, b, lv):
        bl = b.reshape(c_len // lv, lv, HG_DK)
        mid = bl[:, lv // 2 - 1:lv // 2, :]
        e = jnp.exp2(-jnp.abs(bl - mid)).reshape(c_len, HG_DK)
        a_lv = lax.dot_general((q * e).astype(BF16), (kk * e).astype(BF16), _NT,
                               preferred_element_type=F32)
        return jnp.where(level_masks[lv], a_lv, 0.0)

    def intra_fast(data):
        mats = []
        for q, kk, _, b, f in data:
            bl = b.reshape(c_len // blk, blk, HG_DK)
            lf0 = jnp.log2(f.reshape(c_len // blk, blk, HG_DK)[:, 0:1, :])
            d = (bl - (bl[:, 0:1, :] - lf0)).reshape(c_len, HG_DK)
            a_blk = lax.dot_general((q * jnp.exp2(d)).astype(BF16),
                                    (kk * jnp.exp2(-d)).astype(BF16), _NT,
                                    preferred_element_type=F32)
            mats.append(jnp.where(diag_mask, a_blk, 0.0))
        for lv in (32, 64, 128):
            for h, (q, kk, _, b, _) in enumerate(data):
                mats[h] = mats[h] + midpoint_term(q, kk, b, lv)
        return [(a_mat, None) for a_mat in mats]

    def intra_exact(data):
        out = []
        for q, kk, v, b, _ in data:
            q3 = q.reshape(c_len // 8, 8, HG_DK)
            b3 = b.reshape(c_len // 8, 8, HG_DK)
            k3 = kk.reshape(c_len // 8, 8, HG_DK)
            v3 = v.reshape(c_len // 8, 8, HG_DV)
            o3 = jnp.zeros((c_len // 8, 8, HG_DV), F32)
            for s in range(8):
                dec = jnp.exp2(jnp.minimum(b3 - b3[:, s:s + 1, :], 0.0))
                a = jnp.sum(q3 * k3[:, s:s + 1, :] * dec, axis=-1, keepdims=True)
                o3 = o3 + jnp.where(sub >= s, a, 0.0) * v3[:, s:s + 1, :]
            a_mat = sum(midpoint_term(q, kk, b, lv) for lv in (16, 32, 64, 128))
            out.append((a_mat, o3.reshape(c_len, HG_DV)))
        return out

    def head_cols(h):
        return slice(h * HG_DK, (h + 1) * HG_DK)

    def prefix(ci, growth):
        r0 = pl.multiple_of(ci * c_len, c_len)
        for h in range(heads):
            lf = jnp.log2(f_ref[pl.ds(r0, c_len), head_cols(h)])
            b = cumsum_rows(lf)
            b_sc[pl.ds(r0, c_len), head_cols(h)] = b
            bl = b.reshape(c_len // blk, blk, HG_DK)
            before = bl[:, 0:1, :] - lf.reshape(c_len // blk, blk, HG_DK)[:, 0:1, :]
            growth = jnp.maximum(growth, before - bl[:, blk - 1:blk, :])
        return growth

    growth = lax.fori_loop(0, n_chunks, prefix, jnp.zeros((c_len // blk, 1, HG_DK), F32))
    safe = jnp.max(growth) < _HG_FAST_LIMIT

    def run_chunks(intra):
        for ci in range(n_chunks):
            rows = pl.ds(ci * c_len, c_len)
            data = []
            for h in range(heads):
                hs = head_cols(h)
                f = f_ref[rows, hs]
                data.append((q_ref[rows, hs], 1.0 - f, v_ref[rows, hs], b_sc[rows, hs], f))
            mats = intra(data)
            outs = []
            for h, (q, kk, v, b, _) in enumerate(data):
                a_mat, o_blk = mats[h]
                st = st_sc[h]
                o = (jnp.dot(a_mat.astype(BF16), v.astype(BF16), preferred_element_type=F32)
                     + lax.dot_general((q * jnp.exp2(b)).astype(BF16), st.astype(BF16), _NT,
                                       preferred_element_type=F32))
                outs.append(o if o_blk is None else o + o_blk)
                b_end = b[c_len - 1:c_len, :]
                k_end = kk * jnp.exp2(b_end - b)
                st_sc[h] = st * jnp.exp2(b_end) + jnp.dot(
                    v.T.astype(BF16), k_end.astype(BF16), preferred_element_type=F32)
            for h, o in enumerate(outs):
                y = o * lax.rsqrt(jnp.mean(o * o, axis=-1, keepdims=True) + EPS) * nw
                o_ref[rows, head_cols(h)] = y * g_ref[rows, head_cols(h)]

    pl.when(safe)(lambda: run_chunks(intra_fast))
    pl.when(jnp.logical_not(safe))(lambda: run_chunks(intra_exact))

    @pl.when(step == pl.num_programs(1) - 1)
    def _store_state():
        for h in range(heads):
            s_out_ref[h] = st_sc[h].T


def _hgrn(hq, f, hi, hg, s0, nw, batch, seq):
    n, hg_w = hq.shape
    heads = hg_w // HG_DK
    tb = HG_CHUNK * math.gcd(seq // HG_CHUNK, 4)
    steps = seq // tb
    row = pl.BlockSpec((tb, hg_w), lambda b, t: (b * steps + t, 0))
    st = pl.BlockSpec((None, heads, HG_DK, HG_DV), lambda b, t: (b, 0, 0, 0))
    return pl.pallas_call(
        functools.partial(_hgrn_kernel, heads=heads, n_chunks=tb // HG_CHUNK),
        grid=(batch, steps),
        in_specs=[row, row, row, row, st, pl.BlockSpec(nw.shape, lambda b, t: (0, 0))],
        out_specs=[row, st],
        out_shape=[jax.ShapeDtypeStruct((n, hg_w), F32),
                   jax.ShapeDtypeStruct(s0.shape, F32)],
        scratch_shapes=[pltpu.VMEM((heads, HG_DV, HG_DK), F32), pltpu.VMEM((tb, hg_w), F32)],
        compiler_params=_params("parallel", "arbitrary"),
    )(hq, f, hi, hg, s0, nw)


_HG_SHORT = 8


def _hgrn_short_kernel(q_ref, f_ref, v_ref, g_ref, s0_ref, nw_ref, o_ref, s_out_ref, *, heads):
    t = _HG_SHORT
    row = lax.broadcasted_iota(jnp.int32, (t, 1), 0)
    pad8 = jnp.zeros((t, HG_DK), F32)
    pad = jnp.zeros((LANES - t, HG_DK), F32)
    nw = nw_ref[...]
    for h in range(heads):
        hs = slice(h * HG_DK, (h + 1) * HG_DK)
        q = q_ref[:, hs]
        f = f_ref[:, hs]
        v = v_ref[:, hs]
        kk = 1.0 - f
        lf = jnp.log2(f)
        b = jnp.zeros_like(lf)
        for s in range(t):
            b = b + jnp.where(row >= s, lf[s:s + 1, :], 0.0)
        o = jnp.zeros((t, HG_DV), F32)
        for s in range(t):
            dec = jnp.exp2(jnp.minimum(b - b[s:s + 1, :], 0.0))
            a = jnp.sum(q * kk[s:s + 1, :] * dec, axis=-1, keepdims=True)
            o = o + jnp.where(row >= s, a, 0.0) * v[s:s + 1, :]
        st = s0_ref[h].T
        q_dec = jnp.concatenate([q * jnp.exp2(b), pad8], axis=0).astype(BF16)
        o = o + lax.dot_general(q_dec, st.astype(BF16), _NT, preferred_element_type=F32)[:t]
        b_end = b[t - 1:t, :]
        k_end = jnp.concatenate([kk * jnp.exp2(b_end - b), pad], axis=0)
        v_pad = jnp.concatenate([v, pad], axis=0)
        st_new = st * jnp.exp2(b_end) + jnp.dot(v_pad.T.astype(BF16), k_end.astype(BF16),
                                                preferred_element_type=F32)
        s_out_ref[h] = st_new.T
        y = o * lax.rsqrt(jnp.mean(o * o, axis=-1, keepdims=True) + EPS) * nw
        o_ref[:, hs] = y * g_ref[:, hs]


def _hgrn_short(hq, f, hi, hg, s0, nw, batch):
    n, hg_w = hq.shape
    heads = hg_w // HG_DK
    r3 = lambda a: a.reshape(batch, _HG_SHORT, hg_w)
    tok = pl.BlockSpec((None, _HG_SHORT, hg_w), lambda b: (b, 0, 0))
    st = pl.BlockSpec((None, heads, HG_DK, HG_DV), lambda b: (b, 0, 0, 0))
    o, s_new = pl.pallas_call(
        functools.partial(_hgrn_short_kernel, heads=heads),
        grid=(batch,),
        in_specs=[tok, tok, tok, tok, st, pl.BlockSpec(nw.shape, lambda b: (0, 0))],
        out_specs=[tok, st],
        out_shape=[jax.ShapeDtypeStruct((batch, _HG_SHORT, hg_w), F32),
                   jax.ShapeDtypeStruct(s0.shape, F32)],
        compiler_params=_params("parallel"),
    )(r3(hq), r3(f), r3(hi), r3(hg), s0, nw)
    return o.reshape(n, hg_w), s_new


def _outproj_kernel(oda_ref, ohg_ref, x_ref, wo_ref, n2_ref, wrh_ref, wrl_ref, br_ref,
                    h_ref, xn_ref, gates_ref, *, da_w):
    o = (jnp.dot(oda_ref[...].astype(BF16), wo_ref[:da_w, :], preferred_element_type=F32)
         + jnp.dot(ohg_ref[...].astype(BF16), wo_ref[da_w:, :], preferred_element_type=F32))
    h = x_ref[...] + o
    h_ref[...] = h
    xn = h * lax.rsqrt(jnp.mean(h * h, axis=-1, keepdims=True) + EPS) * n2_ref[...]
    xh = xn.astype(BF16)
    xn_ref[...] = xh
    xl = (xn - xh.astype(F32)).astype(BF16)
    wrh = wrh_ref[...]
    both = jnp.dot(xh, jnp.concatenate([wrh, wrl_ref[...]], axis=-1), preferred_element_type=F32)
    logits = (both[:, :LANES] + jnp.dot(xl, wrh, preferred_element_type=F32)
              + both[:, LANES:]) + br_ref[...]
    lane = lax.broadcasted_iota(jnp.int32, logits.shape, 1)
    lane_f = lane.astype(F32)
    big = float(LANES)
    is_g = (lane >= N_EXPERTS) & (lane < N_EXPERTS + N_GROUPS)
    gl = jnp.where(is_g, logits, NEG_INF)
    g_max = jnp.max(gl, axis=-1, keepdims=True)
    g_idx = jnp.min(jnp.where(gl == g_max, lane_f, big), axis=-1, keepdims=True) - N_EXPERTS
    g_w = 1.0 / jnp.sum(jnp.exp(gl - g_max), axis=-1, keepdims=True)
    grp_of_lane = (lane // EXPERTS_PER_GROUP).astype(F32)
    sel = (lane < N_EXPERTS) & (grp_of_lane == g_idx)
    el = jnp.where(sel, logits, NEG_INF)
    v1 = jnp.max(el, axis=-1, keepdims=True)
    i1 = jnp.min(jnp.where(sel, jnp.where(el == v1, lane_f, big), big), axis=-1, keepdims=True)
    el2 = jnp.where(lane_f == i1, NEG_INF, el)
    v2 = jnp.max(el2, axis=-1, keepdims=True)
    sel2 = sel & (lane_f != i1)
    i2 = jnp.min(jnp.where(sel2, jnp.where(el2 == v2, lane_f, big), big), axis=-1, keepdims=True)
    t = jnp.exp(v2 - v1)
    p1 = 1.0 / (1.0 + t)
    p2 = t * p1
    gates_ref[...] = (jnp.where(lane_f == i1, p1 * g_w, 0.0)
                      + jnp.where(lane_f == i2, p2 * g_w, 0.0))


def _outproj(oda, ohg, x2d, wo_bf, n2, wr_hi, wr_lo, br):
    n, d = x2d.shape
    da_w = oda.shape[1]
    tm = _row_tile(n, 512)
    row = lambda w: pl.BlockSpec((tm, w), lambda i: (i, 0))
    full = lambda a: pl.BlockSpec(a.shape, lambda i: (0, 0))
    return pl.pallas_call(
        functools.partial(_outproj_kernel, da_w=da_w),
        grid=(n // tm,),
        in_specs=[row(da_w), row(ohg.shape[1]), row(d), full(wo_bf), full(n2), full(wr_hi),
                  full(wr_lo), full(br)],
        out_specs=[row(d), row(d), row(LANES)],
        out_shape=[jax.ShapeDtypeStruct((n, d), F32), jax.ShapeDtypeStruct((n, d), BF16),
                   jax.ShapeDtypeStruct((n, LANES), F32)],
        compiler_params=_params("parallel"),
    )(oda, ohg, x2d, wo_bf, n2, wr_hi, wr_lo, br)


_MOE_EXPERTS_PER_STEP = 4


def _moe_kernel(x_ref, h_ref, gates_ref, wg_ref, wu_ref, wd_ref, y_ref):
    eb = pl.program_id(1)
    per_step = wg_ref.shape[0]

    @pl.when(eb == 0)
    def _init():
        y_ref[...] = h_ref[...]

    x = x_ref[...]
    gates = gates_ref[...]
    lane = lax.broadcasted_iota(jnp.int32, gates.shape, 1)
    hid = []
    for j in range(per_step):
        a = jnp.dot(x, wg_ref[j], preferred_element_type=F32)
        u = jnp.dot(x, wu_ref[j], preferred_element_type=F32)
        gate = jnp.sum(jnp.where(lane == eb * per_step + j, gates, 0.0), axis=-1, keepdims=True)
        hid.append(((a * _sigmoid(a)) * u * gate).astype(BF16))
    y_ref[...] += jnp.dot(jnp.concatenate(hid, axis=-1), wd_ref[...], preferred_element_type=F32)


def _moe(xn_bf, h, gates, wg_bf, wu_bf, wd_bf):
    n, d = h.shape
    n_exp, _, ff = wg_bf.shape
    per_step = math.gcd(n_exp, _MOE_EXPERTS_PER_STEP)
    tm = _row_tile(n, 1024)
    row = lambda w: pl.BlockSpec((tm, w), lambda i, e: (i, 0))
    return pl.pallas_call(
        _moe_kernel,
        grid=(n // tm, n_exp // per_step),
        in_specs=[row(d), row(d), row(LANES),
                  pl.BlockSpec((per_step, d, ff), lambda i, e: (e, 0, 0)),
                  pl.BlockSpec((per_step, d, ff), lambda i, e: (e, 0, 0)),
                  pl.BlockSpec((per_step * ff, d), lambda i, e: (e, 0))],
        out_specs=row(d),
        out_shape=jax.ShapeDtypeStruct((n, d), F32),
        compiler_params=_params("parallel", "arbitrary"),
    )(xn_bf, h, gates, wg_bf, wu_bf, wd_bf.reshape(n_exp * ff, d))


def _pad_tokens(a, batch, seq, seq_pad, value):
    a3 = a.reshape(batch, seq, a.shape[-1])
    a3 = jnp.pad(a3, ((0, 0), (0, seq_pad - seq), (0, 0)), constant_values=value)
    return a3.reshape(batch * seq_pad, a.shape[-1])


def _layer(x, s0, past, layer_idx, w):
    batch, seq, d = x.shape
    n = batch * seq
    x2d = x.reshape(n, d)
    da_w = w["qw"].shape[1]
    heads = da_w // DA_HEAD_W
    lam_init = 0.8 - 0.6 * math.exp(-0.3 * layer_idx)
    lam_args = (w["lq1"], w["lk1"], w["lq2"], w["lk2"], w["sw"])

    q, k, v, hq, f, hi, hg, *attn_in = _inproj(x2d, w["n1"], w["w_in"], w["qw"], w["kw"], w["lb"],
                                               w["gm"], past is None)
    if past is None:
        oda = _attn_prompt(q, *attn_in, *lam_args, batch, seq, lam_init)
    else:
        cache_k3, cache_v3, page_table = past
        r3 = lambda a: a.reshape(batch, seq, da_w)
        oda = _attn_decode(r3(q), r3(k), r3(v), cache_k3, cache_v3, page_table, *lam_args,
                           heads, lam_init).reshape(n, da_w)

    seq_pad = -(-seq // HG_CHUNK) * HG_CHUNK
    if seq == _HG_SHORT:
        ohg, s_new = _hgrn_short(hq, f, hi, hg, s0, w["hnw"], batch)
    elif seq_pad != seq:
        hq_p, hi_p, hg_p = (_pad_tokens(a, batch, seq, seq_pad, 0.0) for a in (hq, hi, hg))
        f_p = _pad_tokens(f, batch, seq, seq_pad, 1.0)
        ohg, s_new = _hgrn(hq_p, f_p, hi_p, hg_p, s0, w["hnw"], batch, seq_pad)
        ohg = ohg.reshape(batch, seq_pad, -1)[:, :seq].reshape(n, -1)
    else:
        ohg, s_new = _hgrn(hq, f, hi, hg, s0, w["hnw"], batch, seq)

    h, xn_bf, gates = _outproj(oda, ohg, x2d, w["w_out"], w["n2"], w["wr_hi"], w["wr_lo"], w["br"])
    y = _moe(xn_bf, h, gates, w["wg"], w["wu"], w["wd"])
    kv_shape = (batch, seq, heads, DA_HEAD_W)
    return y.reshape(batch, seq, d), k.reshape(kv_shape), v.reshape(kv_shape), s_new


def _layer_weights(l, lb_all, norm1_w, w_in, q_norm_w, k_norm_w, lambda_q1, lambda_k1, lambda_q2,
                   lambda_k2, subln_w, hg_norm_w, w_out, norm2_w, w_router_grp, b_router_grp,
                   w_router_exp, b_router_exp, w_gate, w_up, w_down):
    d = w_in.shape[1]
    da_w = w_out.shape[1] // 2
    n_grp_norm = da_w // DA_HEAD_DIM
    row = lambda a: a.reshape(1, -1).astype(F32)
    g_id = jnp.arange(math.gcd(da_w, 2 * LANES)) // DA_HEAD_DIM
    gm = jnp.where(g_id[:, None] == g_id[None, :], 1.0 / DA_HEAD_DIM, 0.0).astype(BF16)
    wr = jnp.zeros((d, LANES), F32)
    wr = wr.at[:, :N_EXPERTS].set(w_router_exp[l]).at[:, N_EXPERTS:N_EXPERTS + N_GROUPS].set(
        w_router_grp[l])
    wr_hi = wr.astype(BF16)
    wr_lo = (wr - wr_hi.astype(F32)).astype(BF16)
    br = jnp.zeros((1, LANES), F32)
    br = br.at[0, :N_EXPERTS].set(b_router_exp[l]).at[0, N_EXPERTS:N_EXPERTS + N_GROUPS].set(
        b_router_grp[l])
    return dict(
        n1=row(norm1_w[l]), w_in=w_in[l].astype(BF16),
        qw=row(jnp.tile(q_norm_w[l], n_grp_norm)), kw=row(jnp.tile(k_norm_w[l], n_grp_norm)),
        lb=row(lb_all[l]), gm=gm,
        lq1=row(lambda_q1[l]), lk1=row(lambda_k1[l]), lq2=row(lambda_q2[l]), lk2=row(lambda_k2[l]),
        sw=row(subln_w[l]), hnw=row(hg_norm_w[l]),
        w_out=w_out[l].astype(BF16), n2=row(norm2_w[l]), wr_hi=wr_hi, wr_lo=wr_lo, br=br,
        wg=w_gate[l].astype(BF16), wu=w_up[l].astype(BF16), wd=w_down[l].astype(BF16),
    )


def kernel(x_prompt, x_sample, cache_k, cache_v, state_hgrn, page_table, norm1_w, w_in, q_norm_w, k_norm_w, lambda_q1, lambda_k1, lambda_q2, lambda_k2, subln_w, hg_lb, hg_norm_w, w_out, norm2_w, w_router_grp, b_router_grp, w_router_exp, b_router_exp, w_gate, w_up, w_down):
    depth = norm1_w.shape[0]
    n_phys, page, heads, head_w = cache_k.shape[1:]
    lb_all = jnp.cumsum(jax.nn.softmax(hg_lb.astype(F32), axis=0), axis=0)
    yp, ys = x_prompt, x_sample
    outs = [[] for _ in range(6)]
    for l in range(depth):
        w = _layer_weights(l, lb_all, norm1_w, w_in, q_norm_w, k_norm_w, lambda_q1, lambda_k1,
                           lambda_q2, lambda_k2, subln_w, hg_norm_w, w_out, norm2_w,
                           w_router_grp, b_router_grp, w_router_exp, b_router_exp,
                           w_gate, w_up, w_down)
        s0p = jnp.zeros((x_prompt.shape[0],) + state_hgrn.shape[2:], F32)
        yp, k_r, v_r, s_r = _layer(yp, s0p, None, l, w)
        past = (cache_k[l].reshape(n_phys, page * heads, head_w),
                cache_v[l].reshape(n_phys, page * heads, head_w), page_table)
        ys, k_s, v_s, s_s = _layer(ys, state_hgrn[l], past, l, w)
        for lst, val in zip(outs, (k_r, v_r, s_r, k_s, v_s, s_s)):
            lst.append(val)
    kp, vp, sp, kss, vss, sss = (jnp.stack(o) for o in outs)
    return (yp, ys, kp, vp, sp, kss, vss, sss)
```

```python
import functools
import math

import jax
import jax.numpy as jnp
from jax import lax
from jax.experimental import pallas as pl
from jax.experimental.pallas import tpu as pltpu

F32 = jnp.float32
BF16 = jnp.bfloat16

DA_HEAD_DIM = 64
DA_HEAD_W = 2 * DA_HEAD_DIM
HG_DK = 128
HG_DV = 128
HG_CHUNK = 128
_HG_FAST_BLOCK = 16
_HG_FAST_LIMIT = 60.0
N_GROUPS = 4
EXPERTS_PER_GROUP = 4
N_EXPERTS = N_GROUPS * EXPERTS_PER_GROUP
EPS = 1e-6
NEG_INF = -1e30
_Q_SCALE = DA_HEAD_DIM ** -0.5 * math.log2(math.e)
LANES = 128
VMEM_LIMIT = 56 * 1024 * 1024

_NT = (((1,), (1,)), ((), ()))


def _row_tile(n, pref):
    for t in range(min(pref, n), 7, -1):
        if n % t == 0 and t % 8 == 0:
            return t
    return n


def _sigmoid(x):
    return 1.0 / (1.0 + jnp.exp(-x))


def _params(*sem):
    return pltpu.CompilerParams(dimension_semantics=sem, vmem_limit_bytes=VMEM_LIMIT)


def _inproj_kernel(x_ref, n1_ref, w_ref, qw_ref, kw_ref, lb_ref, gm_ref,
                   q_ref, k_ref, v_ref, hq_ref, f_ref, hi_ref, hg_ref, *attn_refs, da_w, hg_w):
    x = x_ref[...]
    xn = x * lax.rsqrt(jnp.mean(x * x, axis=-1, keepdims=True) + EPS) * n1_ref[...]
    xb = xn.astype(BF16)

    def proj(lo, width):
        return jnp.dot(xb, w_ref[:, lo:lo + width], preferred_element_type=F32)

    gm = gm_ref[...]

    def group_rms(t, w):
        sq = (t * t).astype(BF16)
        gw = gm.shape[0]
        ms = jnp.concatenate([jnp.dot(sq[:, lo:lo + gw], gm, preferred_element_type=F32)
                              for lo in range(0, t.shape[1], gw)], axis=-1)
        return t * lax.rsqrt(ms + EPS) * w

    q_ref[...] = (group_rms(proj(0, da_w), qw_ref[...]) * _Q_SCALE).astype(q_ref.dtype)
    k = group_rms(proj(da_w, da_w), kw_ref[...])
    v = proj(2 * da_w, da_w)
    if attn_refs:
        heads = da_w // DA_HEAD_W
        rows = k.shape[0]
        for h in range(heads):
            hs = slice(h * DA_HEAD_W, (h + 1) * DA_HEAD_W)
            k_ref[pl.ds(h, rows, stride=heads), :] = k[:, hs]
            v_ref[pl.ds(h, rows, stride=heads), :] = v[:, hs]
        kb_ref, vt_ref = attn_refs
        kb_ref[...] = k.astype(BF16)
        vt_ref[...] = v.T.astype(BF16)
    else:
        k_ref[...] = k
        v_ref[...] = v
    base = 3 * da_w
    hq = proj(base, hg_w)
    hq_ref[...] = hq * _sigmoid(hq)
    lb = lb_ref[...]
    f_ref[...] = lb + (1.0 - lb) * _sigmoid(proj(base + hg_w, hg_w))
    hi_ref[...] = proj(base + 2 * hg_w, hg_w)
    hg = proj(base + 3 * hg_w, hg_w)
    hg_ref[...] = hg * _sigmoid(hg)


def _inproj(x2d, n1, w_in_bf, qw, kw, lb, gm, for_prompt):
    n, d = x2d.shape
    da_w = qw.shape[1]
    hg_w = lb.shape[1]
    tm = _row_tile(n, 512)
    row = lambda w: pl.BlockSpec((tm, w), lambda i: (i, 0))
    full = lambda a: pl.BlockSpec(a.shape, lambda i: (0, 0))
    heads = da_w // DA_HEAD_W
    if for_prompt:
        kv_shape = jax.ShapeDtypeStruct((n * heads, DA_HEAD_W), F32)
        kv_spec = pl.BlockSpec((tm * heads, DA_HEAD_W), lambda i: (i, 0))
    else:
        kv_shape = jax.ShapeDtypeStruct((n, da_w), F32)
        kv_spec = row(da_w)
    out_shapes = [jax.ShapeDtypeStruct((n, da_w), BF16 if for_prompt else F32), kv_shape, kv_shape]
    out_shapes += [jax.ShapeDtypeStruct((n, hg_w), F32)] * 4
    out_specs = [row(da_w), kv_spec, kv_spec] + [row(hg_w)] * 4
    if for_prompt:
        out_shapes += [jax.ShapeDtypeStruct((n, da_w), BF16), jax.ShapeDtypeStruct((da_w, n), BF16)]
        out_specs += [row(da_w), pl.BlockSpec((da_w, tm), lambda i: (0, i))]
    return pl.pallas_call(
        functools.partial(_inproj_kernel, da_w=da_w, hg_w=hg_w),
        grid=(n // tm,),
        in_specs=[row(d), full(n1), full(w_in_bf), full(qw), full(kw), full(lb), full(gm)],
        out_specs=out_specs,
        out_shape=out_shapes,
        compiler_params=_params("parallel"),
    )(x2d, n1, w_in_bf, qw, kw, lb, gm)


def _lambda(lq1_ref, lk1_ref, lq2_ref, lk2_ref, lam_init):
    s1 = jnp.sum(lq1_ref[...] * lk1_ref[...], axis=-1, keepdims=True)
    s2 = jnp.sum(lq2_ref[...] * lk2_ref[...], axis=-1, keepdims=True)
    return jnp.exp(s1) - jnp.exp(s2) + lam_init


def _softmax_update(s, m, l, acc, vb):
    m_new = jnp.maximum(m, jnp.max(s, axis=-1, keepdims=True))
    p = jnp.exp2(s - m_new)
    alpha = jnp.exp2(m - m_new)
    l_new = alpha * l + jnp.sum(p, axis=-1, keepdims=True)
    acc_new = alpha * acc + jnp.dot(p.astype(BF16), vb, preferred_element_type=F32)
    return m_new, l_new, acc_new


def _sub_ln(o, sw, lam_init):
    y = o * lax.rsqrt(jnp.mean(o * o, axis=-1, keepdims=True) + EPS)
    return y * sw * (1.0 - lam_init)


_ONES_ROWS = 16
_NEXT_TILE_SLOT = 2
_Q_COLS = 256


def _attn_prompt_kernel(q_ref, k_ref, vt_ref, lq1_ref, lk1_ref, lq2_ref, lk2_ref, swc_ref,
                        o_ref, s_sc, mx_sc, acc_sc, *, blk, lam_init):
    qi = pl.program_id(2)
    lane = lax.broadcasted_iota(jnp.int32, (blk, DA_HEAD_W), 1)
    zero = jnp.zeros((blk, DA_HEAD_W), BF16)
    ones = jnp.ones((_ONES_ROWS, blk), BF16)

    def query_maps(tile):
        q = q_ref[pl.ds(pl.multiple_of(tile * blk, blk), blk), :]
        return (jnp.where(lane < DA_HEAD_DIM, q, zero),
                jnp.where(lane >= DA_HEAD_DIM, q, zero))

    q_maps = query_maps(qi)

    n_cb = max(blk // _Q_COLS, 1)
    cb_w = blk // n_cb

    def step(carry, consume_c, consume_slot, diagonal=False, stage=None, stage_diagonal=False):
        vt1 = jnp.concatenate([vt_ref[:, pl.ds(pl.multiple_of(consume_c * blk, blk), blk)], ones],
                              axis=0)
        if stage is not None:
            stage_c, stage_slot, maps = stage
            kb = k_ref[pl.ds(pl.multiple_of(stage_c * blk, blk), blk), :]
        m_new = ([], [])
        n_kh = max(blk // _Q_COLS, 1)
        kh_w = blk // n_kh

        def masked(s, row0, col0):
            key = lax.broadcasted_iota(jnp.int32, s.shape, 0) + row0
            qry = lax.broadcasted_iota(jnp.int32, s.shape, 1) + col0
            return jnp.where(key <= qry, s, jnp.full(s.shape, NEG_INF, BF16))

        for cb in range(n_cb):
            cols = slice(cb * cb_w, (cb + 1) * cb_w)
            for mp in range(2):
                if diagonal:
                    live = slice(0, (cb + 1) * kh_w)
                    s_max = jnp.max(masked(s_sc[consume_slot, mp, live, cols], 0, cb * cb_w),
                                    axis=0, keepdims=True)
                else:
                    s_max = mx_sc[consume_slot, mp, :, cols].astype(BF16)
                m_old = carry[mp][:, cols]
                m = jnp.maximum(m_old, s_max)
                alpha = jnp.exp2(m_old.astype(F32) - m.astype(F32))
                pv = None
                staged_max = None
                for kh in range(n_kh):
                    rows = slice(kh * kh_w, (kh + 1) * kh_w)
                    if stage is not None and not (stage_diagonal and kh > cb):
                        s = lax.dot_general(kb[rows], maps[mp][cols], _NT,
                                            preferred_element_type=F32).astype(BF16)
                        s_sc[stage_slot, mp, rows, cols] = s
                        unit_max = jnp.max(s, axis=0, keepdims=True)
                        staged_max = unit_max if staged_max is None else jnp.maximum(staged_max,
                                                                                     unit_max)
                    if diagonal and kh > cb:
                        continue
                    s = s_sc[consume_slot, mp, rows, cols]
                    if diagonal and kh == cb:
                        s = masked(s, kh * kh_w, cb * cb_w)
                    p = jnp.exp2(s - m)
                    d = jnp.dot(vt1[:, rows], p, preferred_element_type=F32)
                    pv = d if pv is None else pv + d
                acc_sc[mp, :, cols] = alpha * acc_sc[mp, :, cols] + pv
                if stage is not None:
                    mx_sc[stage_slot, mp, :, cols] = staged_max.astype(F32)
                m_new[mp].append(m)
        return tuple(jnp.concatenate(ms, axis=1) for ms in m_new)

    def stage_only(c, slot, maps):
        kb = k_ref[pl.ds(pl.multiple_of(c * blk, blk), blk), :]
        for mp in range(2):
            s = lax.dot_general(kb, maps[mp], _NT, preferred_element_type=F32).astype(BF16)
            s_sc[slot, mp] = s
            mx_sc[slot, mp] = jnp.max(s, axis=0, keepdims=True).astype(F32)

    @pl.when(qi == 0)
    def _first_tile():
        stage_only(0, 0, q_maps)

    @pl.when(qi > 0)
    def _staged_by_previous_tile():
        s_sc[0] = s_sc[_NEXT_TILE_SLOT]
        mx_sc[0] = mx_sc[_NEXT_TILE_SLOT]

    def chunk_pair(j, c):
        c = step(c, 2 * j, 0, stage=(2 * j + 1, 1, q_maps))
        return step(c, 2 * j + 1, 1, stage=(2 * j + 2, 0, q_maps))

    m0 = jnp.full((1, blk), NEG_INF, BF16)
    acc_sc[...] = jnp.zeros(acc_sc.shape, F32)
    carry = lax.fori_loop(0, qi // 2, chunk_pair, (m0, m0))

    next_tile = (0, _NEXT_TILE_SLOT, query_maps(jnp.minimum(qi + 1, pl.num_programs(2) - 1)))

    def tail_odd(c):
        c = step(c, qi - 1, 0, stage=(qi, 1, q_maps), stage_diagonal=True)
        return step(c, qi, 1, diagonal=True, stage=next_tile)

    def tail_even(c):
        return step(c, qi, 0, diagonal=True, stage=next_tile)

    lax.cond(qi % 2 == 1, tail_odd, tail_even, carry)
    a1 = acc_sc[0]
    a2 = acc_sc[1]
    lam = _lambda(lq1_ref, lk1_ref, lq2_ref, lk2_ref, lam_init)
    o1 = a1[:DA_HEAD_W] * (1.0 / a1[DA_HEAD_W:DA_HEAD_W + 1])
    o2 = a2[:DA_HEAD_W] * (1.0 / a2[DA_HEAD_W:DA_HEAD_W + 1])
    o = o1 - lam * o2
    y = o * lax.rsqrt(jnp.mean(o * o, axis=0, keepdims=True) + EPS) * swc_ref[...]
    o_ref[...] = (y * (1.0 - lam_init)).T


def _attn_prompt(q, kb, vt, lq1, lk1, lq2, lk2, sw, batch, seq, lam_init):
    n, da_w = q.shape
    heads = da_w // DA_HEAD_W
    blk = _row_tile(seq, 512)
    nq = seq // blk
    swc = sw.reshape(-1, 1)
    small = lambda a: pl.BlockSpec(a.shape, lambda b, h, i: (0, 0))
    return pl.pallas_call(
        functools.partial(_attn_prompt_kernel, blk=blk, lam_init=lam_init),
        grid=(batch, heads, nq),
        in_specs=[
            pl.BlockSpec((seq, DA_HEAD_W), lambda b, h, i: (b, h)),
            pl.BlockSpec((seq, DA_HEAD_W), lambda b, h, i: (b, h)),
            pl.BlockSpec((DA_HEAD_W, seq), lambda b, h, i: (h, b)),
            small(lq1), small(lk1), small(lq2), small(lk2), small(swc),
        ],
        out_specs=pl.BlockSpec((blk, DA_HEAD_W), lambda b, h, i: (b * nq + i, h)),
        out_shape=jax.ShapeDtypeStruct((n, da_w), F32),
        scratch_shapes=[pltpu.VMEM((_NEXT_TILE_SLOT + 1, 2, blk, blk), BF16),
                        pltpu.VMEM((_NEXT_TILE_SLOT + 1, 2, 1, blk), F32),
                        pltpu.VMEM((2, DA_HEAD_W + _ONES_ROWS, blk), F32)],
        compiler_params=_params("parallel", "parallel", "arbitrary"),
    )(q, kb, vt, lq1, lk1, lq2, lk2, swc)


_PAGE_SLOTS = 4


def _attn_decode_kernel(pt_ref, q_ref, kn_ref, vn_ref, lq1_ref, lk1_ref, lq2_ref, lk2_ref,
                        sw_ref, ck_hbm, cv_hbm, o_ref, kbuf, vbuf, sem, m_sc, l_sc, acc_sc,
                        *, pages_per_step, heads, page, lam_init):
    g = pl.program_id(1)
    n_steps = pl.num_programs(1)
    step = pl.program_id(0) * n_steps + g
    total = pl.num_programs(0) * n_steps
    slot = step % _PAGE_SLOTS

    def page_copies(b, gg, sl):
        copies = []
        for j in range(pages_per_step):
            pid = pt_ref[b, gg * pages_per_step + j]
            copies.append((pltpu.make_async_copy(ck_hbm.at[pid], kbuf.at[sl, j], sem.at[0, sl]), 0))
            copies.append((pltpu.make_async_copy(cv_hbm.at[pid], vbuf.at[sl, j], sem.at[1, sl]), 1))
        return copies

    def fetch(st):
        for cp, prio in page_copies(st // n_steps, st % n_steps, st % _PAGE_SLOTS):
            cp.start(priority=prio)

    @pl.when(step == 0)
    def _first_fetches():
        for ahead in range(_PAGE_SLOTS - 1):
            pl.when(ahead < total)(lambda ahead=ahead: fetch(ahead))

    @pl.when(step + (_PAGE_SLOTS - 1) < total)
    def _prefetch():
        fetch(step + (_PAGE_SLOTS - 1))

    for cp, _ in page_copies(pl.program_id(0), g, slot):
        cp.wait()
    k_refs = [kbuf.at[slot, j] for j in range(pages_per_step)]
    v_refs = [vbuf.at[slot, j] for j in range(pages_per_step)]
    q = q_ref[...]
    t_new = q.shape[0]
    lane = lax.broadcasted_iota(jnp.int32, (t_new, DA_HEAD_W), 1)
    head_cols = [slice(h * DA_HEAD_W, (h + 1) * DA_HEAD_W) for h in range(heads)]
    q2 = [jnp.concatenate([jnp.where(lane < DA_HEAD_DIM, q[:, hs], 0.0),
                           jnp.where(lane >= DA_HEAD_DIM, q[:, hs], 0.0)], axis=0).astype(BF16)
          for hs in head_cols]

    @pl.when(g == 0)
    def _new_tokens():
        pad = jnp.zeros((LANES - t_new, DA_HEAD_W), F32)
        for h, hs in enumerate(head_cols):
            kn = jnp.concatenate([kn_ref[:, hs], pad], axis=0).astype(BF16)
            vn = jnp.concatenate([vn_ref[:, hs], pad], axis=0).astype(BF16)
            s = lax.dot_general(q2[h], kn, _NT, preferred_element_type=F32)
            r = lax.broadcasted_iota(jnp.int32, s.shape, 0)
            c = lax.broadcasted_iota(jnp.int32, s.shape, 1)
            tok = jnp.where(r >= t_new, r - t_new, r)
            s = jnp.where(c <= tok, s, NEG_INF)
            m = jnp.max(s, axis=-1, keepdims=True)
            p = jnp.exp2(s - m)
            m_sc[h] = m
            l_sc[h] = jnp.sum(p, axis=-1, keepdims=True)
            acc_sc[h] = jnp.dot(p.astype(BF16), vn, preferred_element_type=F32)

    def head_rows(refs, h):
        return jnp.concatenate([r_[pl.ds(h, page, stride=heads), :] for r_ in refs],
                               axis=0).astype(BF16)

    scores = [lax.dot_general(q2[h], head_rows(k_refs, h), _NT, preferred_element_type=F32)
              for h in range(heads)]
    probs = []
    for h in range(heads):
        m_new = jnp.maximum(m_sc[h], jnp.max(scores[h], axis=-1, keepdims=True))
        p = jnp.exp2(scores[h] - m_new)
        alpha = jnp.exp2(m_sc[h] - m_new)
        m_sc[h] = m_new
        l_sc[h] = alpha * l_sc[h] + jnp.sum(p, axis=-1, keepdims=True)
        probs.append((p.astype(BF16), alpha))
    for h in range(heads):
        p, alpha = probs[h]
        acc_sc[h] = alpha * acc_sc[h] + jnp.dot(p, head_rows(v_refs, h),
                                                preferred_element_type=F32)

    @pl.when(g == pl.num_programs(1) - 1)
    def _finish():
        lam = _lambda(lq1_ref, lk1_ref, lq2_ref, lk2_ref, lam_init)
        sw = sw_ref[...]
        for h in range(heads):
            on = acc_sc[h] * (1.0 / l_sc[h])
            o = on[:t_new] - lam * on[t_new:]
            o_ref[:, h * DA_HEAD_W:(h + 1) * DA_HEAD_W] = _sub_ln(o, sw, lam_init)


def _attn_decode(q3, kn3, vn3, cache_k3, cache_v3, page_table, lq1, lk1, lq2, lk2, sw,
                 heads, lam_init):
    nb, t_new, da_w = q3.shape
    n_pages = page_table.shape[1]
    page = cache_k3.shape[1] // heads
    pps = math.gcd(n_pages, 16)
    n_steps = n_pages // pps
    tok = pl.BlockSpec((None, t_new, da_w), lambda b, g, pt: (b, 0, 0))
    small = lambda a: pl.BlockSpec(a.shape, lambda b, g, pt: (0, 0))
    hbm = pl.BlockSpec(memory_space=pl.ANY)
    page_buf = pltpu.VMEM((_PAGE_SLOTS, pps, page * heads, DA_HEAD_W), cache_k3.dtype)

    grid_spec = pltpu.PrefetchScalarGridSpec(
        num_scalar_prefetch=1,
        grid=(nb, n_steps),
        in_specs=[tok, tok, tok, small(lq1), small(lk1), small(lq2), small(lk2), small(sw), hbm, hbm],
        out_specs=tok,
        scratch_shapes=[page_buf, page_buf,
                        pltpu.SemaphoreType.DMA((2, _PAGE_SLOTS)),
                        pltpu.VMEM((heads, 2 * t_new, 1), F32),
                        pltpu.VMEM((heads, 2 * t_new, 1), F32),
                        pltpu.VMEM((heads, 2 * t_new, DA_HEAD_W), F32)],
    )
    return pl.pallas_call(
        functools.partial(_attn_decode_kernel, pages_per_step=pps, heads=heads, page=page,
                          lam_init=lam_init),
        grid_spec=grid_spec,
        out_shape=jax.ShapeDtypeStruct((nb, t_new, da_w), F32),
        compiler_params=_params("arbitrary", "arbitrary"),
    )(page_table, q3, kn3, vn3, lq1, lk1, lq2, lk2, sw, cache_k3, cache_v3)


def _hgrn_kernel(q_ref, f_ref, v_ref, g_ref, s0_ref, nw_ref, o_ref, s_out_ref, st_sc, b_sc,
                 *, heads, n_chunks):
    c_len = HG_CHUNK
    step = pl.program_id(1)

    @pl.when(step == 0)
    def _load_state():
        for h in range(heads):
            st_sc[h] = s0_ref[h].T

    r = lax.broadcasted_iota(jnp.int32, (c_len, c_len), 0)
    c = lax.broadcasted_iota(jnp.int32, (c_len, c_len), 1)
    tri = jnp.where(c <= r, 1.0, 0.0).astype(BF16)

    def midpoint_mask(lv):
        return ((r // lv) == (c // lv)) & ((r % lv) >= lv // 2) & ((c % lv) < lv // 2)

    level_masks = {lv: midpoint_mask(lv) for lv in (16, 32, 64, 128)}
    blk = _HG_FAST_BLOCK
    diag_mask = ((r // blk) == (c // blk)) & (c <= r)
    sub = lax.broadcasted_iota(jnp.int32, (c_len // 8, 8, 1), 1)
    nw = nw_ref[...]

    def cumsum_rows(x):
        x1 = x.astype(BF16)
        r1 = x - x1.astype(F32)
        x2 = r1.astype(BF16)
        x3 = (r1 - x2.astype(F32)).astype(BF16)
        return (jnp.dot(tri, x1, preferred_element_type=F32)
                + jnp.dot(tri, x2, preferred_element_type=F32)
                + jnp.dot(tri, x3, preferred_element_type=F32))

    def midpoint_term(q, kk, b, lv):
        bl = b.reshape(c_len // lv, lv, HG_DK)
        mid = bl[:, lv // 2 - 1:lv // 2, :]
        e = jnp.exp2(-jnp.abs(bl - mid)).reshape(c_len, HG_DK)
        a_lv = lax.dot_general((q * e).astype(BF16), (kk * e).astype(BF16), _NT,
                               preferred_element_type=F32)
        return jnp.where(level_masks[lv], a_lv, 0.0)

    def intra_fast(data):
        mats = []
        for q, kk, _, b, f in data:
            bl = b.reshape(c_len // blk, blk, HG_DK)
            lf0 = jnp.log2(f.reshape(c_len // blk, blk, HG_DK)[:, 0:1, :])
            d = (bl - (bl[:, 0:1, :] - lf0)).reshape(c_len, HG_DK)
            a_blk = lax.dot_general((q * jnp.exp2(d)).astype(BF16),
                                    (kk * jnp.exp2(-d)).astype(BF16), _NT,
                                    preferred_element_type=F32)
            mats.append(jnp.where(diag_mask, a_blk, 0.0))
        for lv in (32, 64, 128):
            for h, (q, kk, _, b, _) in enumerate(data):
                mats[h] = mats[h] + midpoint_term(q, kk, b, lv)
        return [(a_mat, None) for a_mat in mats]

    def intra_exact(data):
        out = []
        for q, kk, v, b, _ in data:
            q3 = q.reshape(c_len // 8, 8, HG_DK)
            b3 = b.reshape(c_len // 8, 8, HG_DK)
            k3 = kk.reshape(c_len // 8, 8, HG_DK)
            v3 = v.reshape(c_len // 8, 8, HG_DV)
            o3 = jnp.zeros((c_len // 8, 8, HG_DV), F32)
            for s in range(8):
                dec = jnp.exp2(jnp.minimum(b3 - b3[:, s:s + 1, :], 0.0))
                a = jnp.sum(q3 * k3[:, s:s + 1, :] * dec, axis=-1, keepdims=True)
                o3 = o3 + jnp.where(sub >= s, a, 0.0) * v3[:, s:s + 1, :]
            a_mat = sum(midpoint_term(q, kk, b, lv) for lv in (16, 32, 64, 128))
            out.append((a_mat, o3.reshape(c_len, HG_DV)))
        return out

    def head_cols(h):
        return slice(h * HG_DK, (h + 1) * HG_DK)

    def prefix(ci, growth):
        r0 = pl.multiple_of(ci * c_len, c_len)
        for h in range(heads):
            lf = jnp.log2(f_ref[pl.ds(r0, c_len), head_cols(h)])
            b = cumsum_rows(lf)
            b_sc[pl.ds(r0, c_len), head_cols(h)] = b
            bl = b.reshape(c_len // blk, blk, HG_DK)
            before = bl[:, 0:1, :] - lf.reshape(c_len // blk, blk, HG_DK)[:, 0:1, :]
            growth = jnp.maximum(growth, before - bl[:, blk - 1:blk, :])
        return growth

    growth = lax.fori_loop(0, n_chunks, prefix, jnp.zeros((c_len // blk, 1, HG_DK), F32))
    safe = jnp.max(growth) < _HG_FAST_LIMIT

    def run_chunks(intra):
        for ci in range(n_chunks):
            rows = pl.ds(ci * c_len, c_len)
            data = []
            for h in range(heads):
                hs = head_cols(h)
                f = f_ref[rows, hs]
                data.append((q_ref[rows, hs], 1.0 - f, v_ref[rows, hs], b_sc[rows, hs], f))
            mats = intra(data)
            outs = []
            for h, (q, kk, v, b, _) in enumerate(data):
                a_mat, o_blk = mats[h]
                st = st_sc[h]
                o = (jnp.dot(a_mat.astype(BF16), v.astype(BF16), preferred_element_type=F32)
                     + lax.dot_general((q * jnp.exp2(b)).astype(BF16), st.astype(BF16), _NT,
                                       preferred_element_type=F32))
                outs.append(o if o_blk is None else o + o_blk)
                b_end = b[c_len - 1:c_len, :]
                k_end = kk * jnp.exp2(b_end - b)
                st_sc[h] = st * jnp.exp2(b_end) + jnp.dot(
                    v.T.astype(BF16), k_end.astype(BF16), preferred_element_type=F32)
            for h, o in enumerate(outs):
                y = o * lax.rsqrt(jnp.mean(o * o, axis=-1, keepdims=True) + EPS) * nw
                o_ref[rows, head_cols(h)] = y * g_ref[rows, head_cols(h)]

    pl.when(safe)(lambda: run_chunks(intra_fast))
    pl.when(jnp.logical_not(safe))(lambda: run_chunks(intra_exact))

    @pl.when(step == pl.num_programs(1) - 1)
    def _store_state():
        for h in range(heads):
            s_out_ref[h] = st_sc[h].T


def _hgrn(hq, f, hi, hg, s0, nw, batch, seq):
    n, hg_w = hq.shape
    heads = hg_w // HG_DK
    tb = HG_CHUNK * math.gcd(seq // HG_CHUNK, 4)
    steps = seq // tb
    row = pl.BlockSpec((tb, hg_w), lambda b, t: (b * steps + t, 0))
    st = pl.BlockSpec((None, heads, HG_DK, HG_DV), lambda b, t: (b, 0, 0, 0))
    return pl.pallas_call(
        functools.partial(_hgrn_kernel, heads=heads, n_chunks=tb // HG_CHUNK),
        grid=(batch, steps),
        in_specs=[row, row, row, row, st, pl.BlockSpec(nw.shape, lambda b, t: (0, 0))],
        out_specs=[row, st],
        out_shape=[jax.ShapeDtypeStruct((n, hg_w), F32),
                   jax.ShapeDtypeStruct(s0.shape, F32)],
        scratch_shapes=[pltpu.VMEM((heads, HG_DV, HG_DK), F32), pltpu.VMEM((tb, hg_w), F32)],
        compiler_params=_params("parallel", "arbitrary"),
    )(hq, f, hi, hg, s0, nw)


_HG_SHORT = 8


def _hgrn_short_kernel(q_ref, f_ref, v_ref, g_ref, s0_ref, nw_ref, o_ref, s_out_ref, *, heads):
    t = _HG_SHORT
    row = lax.broadcasted_iota(jnp.int32, (t, 1), 0)
    pad8 = jnp.zeros((t, HG_DK), F32)
    pad = jnp.zeros((LANES - t, HG_DK), F32)
    nw = nw_ref[...]
    for h in range(heads):
        hs = slice(h * HG_DK, (h + 1) * HG_DK)
        q = q_ref[:, hs]
        f = f_ref[:, hs]
        v = v_ref[:, hs]
        kk = 1.0 - f
        lf = jnp.log2(f)
        b = jnp.zeros_like(lf)
        for s in range(t):
            b = b + jnp.where(row >= s, lf[s:s + 1, :], 0.0)
        o = jnp.zeros((t, HG_DV), F32)
        for s in range(t):
            dec = jnp.exp2(jnp.minimum(b - b[s:s + 1, :], 0.0))
            a = jnp.sum(q * kk[s:s + 1, :] * dec, axis=-1, keepdims=True)
            o = o + jnp.where(row >= s, a, 0.0) * v[s:s + 1, :]
        st = s0_ref[h].T
        q_dec = jnp.concatenate([q * jnp.exp2(b), pad8], axis=0).astype(BF16)
        o = o + lax.dot_general(q_dec, st.astype(BF16), _NT, preferred_element_type=F32)[:t]
        b_end = b[t - 1:t, :]
        k_end = jnp.concatenate([kk * jnp.exp2(b_end - b), pad], axis=0)
        v_pad = jnp.concatenate([v, pad], axis=0)
        st_new = st * jnp.exp2(b_end) + jnp.dot(v_pad.T.astype(BF16), k_end.astype(BF16),
                                                preferred_element_type=F32)
        s_out_ref[h] = st_new.T
        y = o * lax.rsqrt(jnp.mean(o * o, axis=-1, keepdims=True) + EPS) * nw
        o_ref[:, hs] = y * g_ref[:, hs]


def _hgrn_short(hq, f, hi, hg, s0, nw, batch):
    n, hg_w = hq.shape
    heads = hg_w // HG_DK
    r3 = lambda a: a.reshape(batch, _HG_SHORT, hg_w)
    tok = pl.BlockSpec((None, _HG_SHORT, hg_w), lambda b: (b, 0, 0))
    st = pl.BlockSpec((None, heads, HG_DK, HG_DV), lambda b: (b, 0, 0, 0))
    o, s_new = pl.pallas_call(
        functools.partial(_hgrn_short_kernel, heads=heads),
        grid=(batch,),
        in_specs=[tok, tok, tok, tok, st, pl.BlockSpec(nw.shape, lambda b: (0, 0))],
        out_specs=[tok, st],
        out_shape=[jax.ShapeDtypeStruct((batch, _HG_SHORT, hg_w), F32),
                   jax.ShapeDtypeStruct(s0.shape, F32)],
        compiler_params=_params("parallel"),
    )(r3(hq), r3(f), r3(hi), r3(hg), s0, nw)
    return o.reshape(n, hg_w), s_new


def _outproj_kernel(oda_ref, ohg_ref, x_ref, wo_ref, n2_ref, wrh_ref, wrl_ref, br_ref,
                    h_ref, xn_ref, gates_ref, *, da_w):
    n_parts = 2 if oda_ref.shape[0] % 16 == 0 else 1
    part = oda_ref.shape[0] // n_parts

    def project(i):
        rows = pl.ds(i * part, part)
        o = (jnp.dot(oda_ref[rows, :].astype(BF16), wo_ref[:da_w, :], preferred_element_type=F32)
             + jnp.dot(ohg_ref[rows, :].astype(BF16), wo_ref[da_w:, :], preferred_element_type=F32))
        h = x_ref[rows, :] + o
        h_ref[rows, :] = h
        return h

    def route(i, h):
        rows = pl.ds(i * part, part)
        xn = h * lax.rsqrt(jnp.mean(h * h, axis=-1, keepdims=True) + EPS) * n2_ref[...]
        xh = xn.astype(BF16)
        xn_ref[rows, :] = xh
        xl = (xn - xh.astype(F32)).astype(BF16)
        wrh = wrh_ref[...]
        both = jnp.dot(xh, jnp.concatenate([wrh, wrl_ref[...]], axis=-1),
                       preferred_element_type=F32)
        logits = (both[:, :LANES] + jnp.dot(xl, wrh, preferred_element_type=F32)
                  + both[:, LANES:]) + br_ref[...]
        lane = lax.broadcasted_iota(jnp.int32, logits.shape, 1)
        lane_f = lane.astype(F32)
        big = float(LANES)
        is_g = (lane >= N_EXPERTS) & (lane < N_EXPERTS + N_GROUPS)
        gl = jnp.where(is_g, logits, NEG_INF)
        g_max = jnp.max(gl, axis=-1, keepdims=True)
        g_idx = jnp.min(jnp.where(gl == g_max, lane_f, big), axis=-1, keepdims=True) - N_EXPERTS
        g_w = 1.0 / jnp.sum(jnp.exp(gl - g_max), axis=-1, keepdims=True)
        grp_of_lane = (lane // EXPERTS_PER_GROUP).astype(F32)
        sel = (lane < N_EXPERTS) & (grp_of_lane == g_idx)
        el = jnp.where(sel, logits, NEG_INF)
        v1 = jnp.max(el, axis=-1, keepdims=True)
        i1 = jnp.min(jnp.where(sel, jnp.where(el == v1, lane_f, big), big), axis=-1, keepdims=True)
        el2 = jnp.where(lane_f == i1, NEG_INF, el)
        v2 = jnp.max(el2, axis=-1, keepdims=True)
        sel2 = sel & (lane_f != i1)
        i2 = jnp.min(jnp.where(sel2, jnp.where(el2 == v2, lane_f, big), big), axis=-1,
                     keepdims=True)
        t = jnp.exp(v2 - v1)
        p1 = 1.0 / (1.0 + t)
        p2 = t * p1
        gates_ref[rows, :] = (jnp.where(lane_f == i1, p1 * g_w, 0.0)
                              + jnp.where(lane_f == i2, p2 * g_w, 0.0))

    hs = [project(i) for i in range(n_parts)]
    for i in range(n_parts):
        route(i, hs[i])


def _outproj(oda, ohg, x2d, wo_bf, n2, wr_hi, wr_lo, br):
    n, d = x2d.shape
    da_w = oda.shape[1]
    tm = _row_tile(n, 512)
    row = lambda w: pl.BlockSpec((tm, w), lambda i: (i, 0))
    full = lambda a: pl.BlockSpec(a.shape, lambda i: (0, 0))
    return pl.pallas_call(
        functools.partial(_outproj_kernel, da_w=da_w),
        grid=(n // tm,),
        in_specs=[row(da_w), row(ohg.shape[1]), row(d), full(wo_bf), full(n2), full(wr_hi),
                  full(wr_lo), full(br)],
        out_specs=[row(d), row(d), row(LANES)],
        out_shape=[jax.ShapeDtypeStruct((n, d), F32), jax.ShapeDtypeStruct((n, d), BF16),
                   jax.ShapeDtypeStruct((n, LANES), F32)],
        compiler_params=_params("parallel"),
    )(oda, ohg, x2d, wo_bf, n2, wr_hi, wr_lo, br)


_MOE_EXPERTS_PER_STEP = 4


def _moe_kernel(x_ref, h_ref, gates_ref, wg_ref, wu_ref, wd_ref, y_ref):
    eb = pl.program_id(1)
    per_step = wg_ref.shape[0]

    @pl.when(eb == 0)
    def _init():
        y_ref[...] = h_ref[...]

    x = x_ref[...]
    gates = gates_ref[...]
    lane = lax.broadcasted_iota(jnp.int32, gates.shape, 1)
    hid = []
    for j in range(per_step):
        a = jnp.dot(x, wg_ref[j], preferred_element_type=F32)
        u = jnp.dot(x, wu_ref[j], preferred_element_type=F32)
        gate = jnp.sum(jnp.where(lane == eb * per_step + j, gates, 0.0), axis=-1, keepdims=True)
        hid.append(((a * _sigmoid(a)) * u * gate).astype(BF16))
    y_ref[...] += jnp.dot(jnp.concatenate(hid, axis=-1), wd_ref[...], preferred_element_type=F32)


def _moe(xn_bf, h, gates, wg_bf, wu_bf, wd_bf):
    n, d = h.shape
    n_exp, _, ff = wg_bf.shape
    per_step = math.gcd(n_exp, _MOE_EXPERTS_PER_STEP)
    tm = _row_tile(n, 1024)
    row = lambda w: pl.BlockSpec((tm, w), lambda i, e: (i, 0))
    return pl.pallas_call(
        _moe_kernel,
        grid=(n // tm, n_exp // per_step),
        in_specs=[row(d), row(d), row(LANES),
                  pl.BlockSpec((per_step, d, ff), lambda i, e: (e, 0, 0)),
                  pl.BlockSpec((per_step, d, ff), lambda i, e: (e, 0, 0)),
                  pl.BlockSpec((per_step * ff, d), lambda i, e: (e, 0))],
        out_specs=row(d),
        out_shape=jax.ShapeDtypeStruct((n, d), F32),
        compiler_params=_params("parallel", "arbitrary"),
    )(xn_bf, h, gates, wg_bf, wu_bf, wd_bf.reshape(n_exp * ff, d))


def _pad_tokens(a, batch, seq, seq_pad, value):
    a3 = a.reshape(batch, seq, a.shape[-1])
    a3 = jnp.pad(a3, ((0, 0), (0, seq_pad - seq), (0, 0)), constant_values=value)
    return a3.reshape(batch * seq_pad, a.shape[-1])


def _layer(x, s0, past, layer_idx, w):
    batch, seq, d = x.shape
    n = batch * seq
    x2d = x.reshape(n, d)
    da_w = w["qw"].shape[1]
    heads = da_w // DA_HEAD_W
    lam_init = 0.8 - 0.6 * math.exp(-0.3 * layer_idx)
    lam_args = (w["lq1"], w["lk1"], w["lq2"], w["lk2"], w["sw"])

    q, k, v, hq, f, hi, hg, *attn_in = _inproj(x2d, w["n1"], w["w_in"], w["qw"], w["kw"], w["lb"],
                                               w["gm"], past is None)
    if past is None:
        oda = _attn_prompt(q, *attn_in, *lam_args, batch, seq, lam_init)
    else:
        cache_k3, cache_v3, page_table = past
        r3 = lambda a: a.reshape(batch, seq, da_w)
        oda = _attn_decode(r3(q), r3(k), r3(v), cache_k3, cache_v3, page_table, *lam_args,
                           heads, lam_init).reshape(n, da_w)

    seq_pad = -(-seq // HG_CHUNK) * HG_CHUNK
    if seq == _HG_SHORT:
        ohg, s_new = _hgrn_short(hq, f, hi, hg, s0, w["hnw"], batch)
    elif seq_pad != seq:
        hq_p, hi_p, hg_p = (_pad_tokens(a, batch, seq, seq_pad, 0.0) for a in (hq, hi, hg))
        f_p = _pad_tokens(f, batch, seq, seq_pad, 1.0)
        ohg, s_new = _hgrn(hq_p, f_p, hi_p, hg_p, s0, w["hnw"], batch, seq_pad)
        ohg = ohg.reshape(batch, seq_pad, -1)[:, :seq].reshape(n, -1)
    else:
        ohg, s_new = _hgrn(hq, f, hi, hg, s0, w["hnw"], batch, seq)

    h, xn_bf, gates = _outproj(oda, ohg, x2d, w["w_out"], w["n2"], w["wr_hi"], w["wr_lo"], w["br"])
    y = _moe(xn_bf, h, gates, w["wg"], w["wu"], w["wd"])
    kv_shape = (batch, seq, heads, DA_HEAD_W)
    return y.reshape(batch, seq, d), k.reshape(kv_shape), v.reshape(kv_shape), s_new


def _layer_weights(l, lb_all, norm1_w, w_in, q_norm_w, k_norm_w, lambda_q1, lambda_k1, lambda_q2,
                   lambda_k2, subln_w, hg_norm_w, w_out, norm2_w, w_router_grp, b_router_grp,
                   w_router_exp, b_router_exp, w_gate, w_up, w_down):
    d = w_in.shape[1]
    da_w = w_out.shape[1] // 2
    n_grp_norm = da_w // DA_HEAD_DIM
    row = lambda a: a.reshape(1, -1).astype(F32)
    g_id = jnp.arange(math.gcd(da_w, 2 * LANES)) // DA_HEAD_DIM
    gm = jnp.where(g_id[:, None] == g_id[None, :], 1.0 / DA_HEAD_DIM, 0.0).astype(BF16)
    wr = jnp.zeros((d, LANES), F32)
    wr = wr.at[:, :N_EXPERTS].set(w_router_exp[l]).at[:, N_EXPERTS:N_EXPERTS + N_GROUPS].set(
        w_router_grp[l])
    wr_hi = wr.astype(BF16)
    wr_lo = (wr - wr_hi.astype(F32)).astype(BF16)
    br = jnp.zeros((1, LANES), F32)
    br = br.at[0, :N_EXPERTS].set(b_router_exp[l]).at[0, N_EXPERTS:N_EXPERTS + N_GROUPS].set(
        b_router_grp[l])
    return dict(
        n1=row(norm1_w[l]), w_in=w_in[l].astype(BF16),
        qw=row(jnp.tile(q_norm_w[l], n_grp_norm)), kw=row(jnp.tile(k_norm_w[l], n_grp_norm)),
        lb=row(lb_all[l]), gm=gm,
        lq1=row(lambda_q1[l]), lk1=row(lambda_k1[l]), lq2=row(lambda_q2[l]), lk2=row(lambda_k2[l]),
        sw=row(subln_w[l]), hnw=row(hg_norm_w[l]),
        w_out=w_out[l].astype(BF16), n2=row(norm2_w[l]), wr_hi=wr_hi, wr_lo=wr_lo, br=br,
        wg=w_gate[l].astype(BF16), wu=w_up[l].astype(BF16), wd=w_down[l].astype(BF16),
    )


def kernel(x_prompt, x_sample, cache_k, cache_v, state_hgrn, page_table, norm1_w, w_in, q_norm_w, k_norm_w, lambda_q1, lambda_k1, lambda_q2, lambda_k2, subln_w, hg_lb, hg_norm_w, w_out, norm2_w, w_router_grp, b_router_grp, w_router_exp, b_router_exp, w_gate, w_up, w_down):
    depth = norm1_w.shape[0]
    n_phys, page, heads, head_w = cache_k.shape[1:]
    lb_all = jnp.cumsum(jax.nn.softmax(hg_lb.astype(F32), axis=0), axis=0)
    yp, ys = x_prompt, x_sample
    outs = [[] for _ in range(6)]
    for l in range(depth):
        w = _layer_weights(l, lb_all, norm1_w, w_in, q_norm_w, k_norm_w, lambda_q1, lambda_k1,
                           lambda_q2, lambda_k2, subln_w, hg_norm_w, w_out, norm2_w,
                           w_router_grp, b_router_grp, w_router_exp, b_router_exp,
                           w_gate, w_up, w_down)
        s0p = jnp.zeros((x_prompt.shape[0],) + state_hgrn.shape[2:], F32)
        yp, k_r, v_r, s_r = _layer(yp, s0p, None, l, w)
        past = (cache_k[l].reshape(n_phys, page * heads, head_w),
                cache_v[l].reshape(n_phys, page * heads, head_w), page_table)
        ys, k_s, v_s, s_s = _layer(ys, state_hgrn[l], past, l, w)
        for lst, val in zip(outs, (k_r, v_r, s_r, k_s, v_s, s_s)):
            lst.append(val)
    kp, vp, sp, kss, vss, sss = (jnp.stack(o) for o in outs)
    return (yp, ys, kp, vp, sp, kss, vss, sss)
```

```python
import functools
import math

import jax
import jax.numpy as jnp
from jax import lax
from jax.experimental import pallas as pl
from jax.experimental.pallas import tpu as pltpu

F32 = jnp.float32
BF16 = jnp.bfloat16

DA_HEAD_DIM = 64
DA_HEAD_W = 2 * DA_HEAD_DIM
HG_DK = 128
HG_DV = 128
HG_CHUNK = 128
_HG_FAST_BLOCK = 16
_HG_FAST_LIMIT = 60.0
N_GROUPS = 4
EXPERTS_PER_GROUP = 4
N_EXPERTS = N_GROUPS * EXPERTS_PER_GROUP
EPS = 1e-6
NEG_INF = -1e30
_Q_SCALE = DA_HEAD_DIM ** -0.5 * math.log2(math.e)
LANES = 128
VMEM_LIMIT = 56 * 1024 * 1024

_NT = (((1,), (1,)), ((), ()))


def _row_tile(n, pref):
    for t in range(min(pref, n), 7, -1):
        if n % t == 0 and t % 8 == 0:
            return t
    return n


def _sigmoid(x):
    return 1.0 / (1.0 + jnp.exp(-x))


def _params(*sem):
    return pltpu.CompilerParams(dimension_semantics=sem, vmem_limit_bytes=VMEM_LIMIT)


def _inproj_kernel(x_ref, n1_ref, w_ref, qw_ref, kw_ref, lb_ref, gm_ref,
                   q_ref, k_ref, v_ref, hq_ref, f_ref, hi_ref, hg_ref, *attn_refs, da_w, hg_w):
    x = x_ref[...]
    xn = x * lax.rsqrt(jnp.mean(x * x, axis=-1, keepdims=True) + EPS) * n1_ref[...]
    xb = xn.astype(BF16)

    def proj(lo, width):
        return jnp.dot(xb, w_ref[:, lo:lo + width], preferred_element_type=F32)

    gm = gm_ref[...]

    def group_rms(t, w):
        sq = (t * t).astype(BF16)
        gw = gm.shape[0]
        ms = jnp.concatenate([jnp.dot(sq[:, lo:lo + gw], gm, preferred_element_type=F32)
                              for lo in range(0, t.shape[1], gw)], axis=-1)
        return t * lax.rsqrt(ms + EPS) * w

    q_ref[...] = (group_rms(proj(0, da_w), qw_ref[...]) * _Q_SCALE).astype(q_ref.dtype)
    k = group_rms(proj(da_w, da_w), kw_ref[...])
    v = proj(2 * da_w, da_w)
    if attn_refs:
        heads = da_w // DA_HEAD_W
        rows = k.shape[0]
        for h in range(heads):
            hs = slice(h * DA_HEAD_W, (h + 1) * DA_HEAD_W)
            k_ref[pl.ds(h, rows, stride=heads), :] = k[:, hs]
            v_ref[pl.ds(h, rows, stride=heads), :] = v[:, hs]
        kb_ref, vt_ref = attn_refs
        kb_ref[...] = k.astype(BF16)
        vt_ref[...] = v.T.astype(BF16)
    else:
        k_ref[...] = k
        v_ref[...] = v
    base = 3 * da_w
    hq = proj(base, hg_w)
    hq_ref[...] = hq * _sigmoid(hq)
    lb = lb_ref[...]
    f_ref[...] = lb + (1.0 - lb) * _sigmoid(proj(base + hg_w, hg_w))
    hi_ref[...] = proj(base + 2 * hg_w, hg_w)
    hg = proj(base + 3 * hg_w, hg_w)
    hg_ref[...] = hg * _sigmoid(hg)


def _inproj(x2d, n1, w_in_bf, qw, kw, lb, gm, for_prompt):
    n, d = x2d.shape
    da_w = qw.shape[1]
    hg_w = lb.shape[1]
    tm = _row_tile(n, 512)
    row = lambda w: pl.BlockSpec((tm, w), lambda i: (i, 0))
    full = lambda a: pl.BlockSpec(a.shape, lambda i: (0, 0))
    heads = da_w // DA_HEAD_W
    if for_prompt:
        kv_shape = jax.ShapeDtypeStruct((n * heads, DA_HEAD_W), F32)
        kv_spec = pl.BlockSpec((tm * heads, DA_HEAD_W), lambda i: (i, 0))
    else:
        kv_shape = jax.ShapeDtypeStruct((n, da_w), F32)
        kv_spec = row(da_w)
    out_shapes = [jax.ShapeDtypeStruct((n, da_w), BF16 if for_prompt else F32), kv_shape, kv_shape]
    out_shapes += [jax.ShapeDtypeStruct((n, hg_w), F32)] * 4
    out_specs = [row(da_w), kv_spec, kv_spec] + [row(hg_w)] * 4
    if for_prompt:
        out_shapes += [jax.ShapeDtypeStruct((n, da_w), BF16), jax.ShapeDtypeStruct((da_w, n), BF16)]
        out_specs += [row(da_w), pl.BlockSpec((da_w, tm), lambda i: (0, i))]
    return pl.pallas_call(
        functools.partial(_inproj_kernel, da_w=da_w, hg_w=hg_w),
        grid=(n // tm,),
        in_specs=[row(d), full(n1), full(w_in_bf), full(qw), full(kw), full(lb), full(gm)],
        out_specs=out_specs,
        out_shape=out_shapes,
        compiler_params=_params("parallel"),
    )(x2d, n1, w_in_bf, qw, kw, lb, gm)


def _lambda(lq1_ref, lk1_ref, lq2_ref, lk2_ref, lam_init):
    s1 = jnp.sum(lq1_ref[...] * lk1_ref[...], axis=-1, keepdims=True)
    s2 = jnp.sum(lq2_ref[...] * lk2_ref[...], axis=-1, keepdims=True)
    return jnp.exp(s1) - jnp.exp(s2) + lam_init


def _softmax_update(s, m, l, acc, vb):
    m_new = jnp.maximum(m, jnp.max(s, axis=-1, keepdims=True))
    p = jnp.exp2(s - m_new)
    alpha = jnp.exp2(m - m_new)
    l_new = alpha * l + jnp.sum(p, axis=-1, keepdims=True)
    acc_new = alpha * acc + jnp.dot(p.astype(BF16), vb, preferred_element_type=F32)
    return m_new, l_new, acc_new


def _sub_ln(o, sw, lam_init):
    y = o * lax.rsqrt(jnp.mean(o * o, axis=-1, keepdims=True) + EPS)
    return y * sw * (1.0 - lam_init)


_ONES_ROWS = 16
_NEXT_TILE_SLOT = 2
_Q_COLS = 256


def _attn_prompt_kernel(q_ref, k_ref, vt_ref, lq1_ref, lk1_ref, lq2_ref, lk2_ref, swc_ref,
                        o_ref, s_sc, mx_sc, acc_sc, *, blk, lam_init):
    qi = pl.program_id(2)
    lane = lax.broadcasted_iota(jnp.int32, (blk, DA_HEAD_W), 1)
    zero = jnp.zeros((blk, DA_HEAD_W), BF16)
    ones = jnp.ones((_ONES_ROWS, blk), BF16)

    def query_maps(tile):
        q = q_ref[pl.ds(pl.multiple_of(tile * blk, blk), blk), :]
        return (jnp.where(lane < DA_HEAD_DIM, q, zero),
                jnp.where(lane >= DA_HEAD_DIM, q, zero))

    q_maps = query_maps(qi)

    n_cb = max(blk // _Q_COLS, 1)
    cb_w = blk // n_cb

    def step(carry, consume_c, consume_slot, diagonal=False, stage=None, stage_diagonal=False):
        vt1 = jnp.concatenate([vt_ref[:, pl.ds(pl.multiple_of(consume_c * blk, blk), blk)], ones],
                              axis=0)
        if stage is not None:
            stage_c, stage_slot, maps = stage
            kb = k_ref[pl.ds(pl.multiple_of(stage_c * blk, blk), blk), :]
        m_new = ([], [])
        n_kh = max(blk // _Q_COLS, 1)
        kh_w = blk // n_kh

        def masked(s, row0, col0):
            key = lax.broadcasted_iota(jnp.int32, s.shape, 0) + row0
            qry = lax.broadcasted_iota(jnp.int32, s.shape, 1) + col0
            return jnp.where(key <= qry, s, jnp.full(s.shape, NEG_INF, BF16))

        for cb in range(n_cb):
            cols = slice(cb * cb_w, (cb + 1) * cb_w)
            for mp in range(2):
                if diagonal:
                    live = slice(0, (cb + 1) * kh_w)
                    s_max = jnp.max(masked(s_sc[consume_slot, mp, live, cols], 0, cb * cb_w),
                                    axis=0, keepdims=True)
                else:
                    s_max = mx_sc[consume_slot, mp, :, cols].astype(BF16)
                m_old = carry[mp][:, cols]
                m = jnp.maximum(m_old, s_max)
                alpha = jnp.exp2(m_old.astype(F32) - m.astype(F32))
                pv = None
                staged_max = None
                for kh in range(n_kh):
                    rows = slice(kh * kh_w, (kh + 1) * kh_w)
                    if stage is not None and not (stage_diagonal and kh > cb):
                        s = lax.dot_general(kb[rows], maps[mp][cols], _NT,
                                            preferred_element_type=F32).astype(BF16)
                        s_sc[stage_slot, mp, rows, cols] = s
                        unit_max = jnp.max(s, axis=0, keepdims=True)
                        staged_max = unit_max if staged_max is None else jnp.maximum(staged_max,
                                                                                     unit_max)
                    if diagonal and kh > cb:
                        continue
                    s = s_sc[consume_slot, mp, rows, cols]
                    if diagonal and kh == cb:
                        s = masked(s, kh * kh_w, cb * cb_w)
                    p = jnp.exp2(s - m)
                    d = jnp.dot(vt1[:, rows], p, preferred_element_type=F32)
                    pv = d if pv is None else pv + d
                acc_sc[mp, :, cols] = alpha * acc_sc[mp, :, cols] + pv
                if stage is not None:
                    mx_sc[stage_slot, mp, :, cols] = staged_max.astype(F32)
                m_new[mp].append(m)
        return tuple(jnp.concatenate(ms, axis=1) for ms in m_new)

    def stage_only(c, slot, maps):
        kb = k_ref[pl.ds(pl.multiple_of(c * blk, blk), blk), :]
        for mp in range(2):
            s = lax.dot_general(kb, maps[mp], _NT, preferred_element_type=F32).astype(BF16)
            s_sc[slot, mp] = s
            mx_sc[slot, mp] = jnp.max(s, axis=0, keepdims=True).astype(F32)

    @pl.when(qi == 0)
    def _first_tile():
        stage_only(0, 0, q_maps)

    @pl.when(qi > 0)
    def _staged_by_previous_tile():
        s_sc[0] = s_sc[_NEXT_TILE_SLOT]
        mx_sc[0] = mx_sc[_NEXT_TILE_SLOT]

    def chunk_pair(j, c):
        c = step(c, 2 * j, 0, stage=(2 * j + 1, 1, q_maps))
        return step(c, 2 * j + 1, 1, stage=(2 * j + 2, 0, q_maps))

    m0 = jnp.full((1, blk), NEG_INF, BF16)
    acc_sc[...] = jnp.zeros(acc_sc.shape, F32)
    carry = lax.fori_loop(0, qi // 2, chunk_pair, (m0, m0))

    next_tile = (0, _NEXT_TILE_SLOT, query_maps(jnp.minimum(qi + 1, pl.num_programs(2) - 1)))

    def tail_odd(c):
        c = step(c, qi - 1, 0, stage=(qi, 1, q_maps), stage_diagonal=True)
        return step(c, qi, 1, diagonal=True, stage=next_tile)

    def tail_even(c):
        return step(c, qi, 0, diagonal=True, stage=next_tile)

    lax.cond(qi % 2 == 1, tail_odd, tail_even, carry)
    a1 = acc_sc[0]
    a2 = acc_sc[1]
    lam = _lambda(lq1_ref, lk1_ref, lq2_ref, lk2_ref, lam_init)
    o1 = a1[:DA_HEAD_W] * (1.0 / a1[DA_HEAD_W:DA_HEAD_W + 1])
    o2 = a2[:DA_HEAD_W] * (1.0 / a2[DA_HEAD_W:DA_HEAD_W + 1])
    o = o1 - lam * o2
    y = o * lax.rsqrt(jnp.mean(o * o, axis=0, keepdims=True) + EPS) * swc_ref[...]
    o_ref[...] = (y * (1.0 - lam_init)).T


def _attn_prompt(q, kb, vt, lq1, lk1, lq2, lk2, sw, batch, seq, lam_init):
    n, da_w = q.shape
    heads = da_w // DA_HEAD_W
    blk = _row_tile(seq, 512)
    nq = seq // blk
    swc = sw.reshape(-1, 1)
    small = lambda a: pl.BlockSpec(a.shape, lambda b, h, i: (0, 0))
    return pl.pallas_call(
        functools.partial(_attn_prompt_kernel, blk=blk, lam_init=lam_init),
        grid=(batch, heads, nq),
        in_specs=[
            pl.BlockSpec((seq, DA_HEAD_W), lambda b, h, i: (b, h)),
            pl.BlockSpec((seq, DA_HEAD_W), lambda b, h, i: (b, h)),
            pl.BlockSpec((DA_HEAD_W, seq), lambda b, h, i: (h, b)),
            small(lq1), small(lk1), small(lq2), small(lk2), small(swc),
        ],
        out_specs=pl.BlockSpec((blk, DA_HEAD_W), lambda b, h, i: (b * nq + i, h)),
        out_shape=jax.ShapeDtypeStruct((n, da_w), F32),
        scratch_shapes=[pltpu.VMEM((_NEXT_TILE_SLOT + 1, 2, blk, blk), BF16),
                        pltpu.VMEM((_NEXT_TILE_SLOT + 1, 2, 1, blk), F32),
                        pltpu.VMEM((2, DA_HEAD_W + _ONES_ROWS, blk), F32)],
        compiler_params=_params("parallel", "parallel", "arbitrary"),
    )(q, kb, vt, lq1, lk1, lq2, lk2, swc)


_PAGE_SLOTS = 4


def _attn_decode_kernel(pt_ref, q_ref, kn_ref, vn_ref, lq1_ref, lk1_ref, lq2_ref, lk2_ref,
                        sw_ref, ck_hbm, cv_hbm, o_ref, kbuf, vbuf, sem, m_sc, l_sc, acc_sc,
                        *, pages_per_step, heads, page, lam_init):
    g = pl.program_id(1)
    n_steps = pl.num_programs(1)
    step = pl.program_id(0) * n_steps + g
    total = pl.num_programs(0) * n_steps
    slot = step % _PAGE_SLOTS

    def page_copies(b, gg, sl):
        copies = []
        for j in range(pages_per_step):
            pid = pt_ref[b, gg * pages_per_step + j]
            copies.append((pltpu.make_async_copy(ck_hbm.at[pid], kbuf.at[sl, j], sem.at[0, sl]), 0))
            copies.append((pltpu.make_async_copy(cv_hbm.at[pid], vbuf.at[sl, j], sem.at[1, sl]), 1))
        return copies

    def fetch(st):
        for cp, prio in page_copies(st // n_steps, st % n_steps, st % _PAGE_SLOTS):
            cp.start(priority=prio)

    @pl.when(step == 0)
    def _first_fetches():
        for ahead in range(_PAGE_SLOTS - 1):
            pl.when(ahead < total)(lambda ahead=ahead: fetch(ahead))

    @pl.when(step + (_PAGE_SLOTS - 1) < total)
    def _prefetch():
        fetch(step + (_PAGE_SLOTS - 1))

    for cp, _ in page_copies(pl.program_id(0), g, slot):
        cp.wait()
    k_refs = [kbuf.at[slot, j] for j in range(pages_per_step)]
    v_refs = [vbuf.at[slot, j] for j in range(pages_per_step)]
    q = q_ref[...]
    t_new = q.shape[0]
    lane = lax.broadcasted_iota(jnp.int32, (t_new, DA_HEAD_W), 1)
    head_cols = [slice(h * DA_HEAD_W, (h + 1) * DA_HEAD_W) for h in range(heads)]
    q2 = [jnp.concatenate([jnp.where(lane < DA_HEAD_DIM, q[:, hs], 0.0),
                           jnp.where(lane >= DA_HEAD_DIM, q[:, hs], 0.0)], axis=0).astype(BF16)
          for hs in head_cols]

    @pl.when(g == 0)
    def _new_tokens():
        pad = jnp.zeros((LANES - t_new, DA_HEAD_W), F32)
        for h, hs in enumerate(head_cols):
            kn = jnp.concatenate([kn_ref[:, hs], pad], axis=0).astype(BF16)
            vn = jnp.concatenate([vn_ref[:, hs], pad], axis=0).astype(BF16)
            s = lax.dot_general(q2[h], kn, _NT, preferred_element_type=F32)
            r = lax.broadcasted_iota(jnp.int32, s.shape, 0)
            c = lax.broadcasted_iota(jnp.int32, s.shape, 1)
            tok = jnp.where(r >= t_new, r - t_new, r)
            s = jnp.where(c <= tok, s, NEG_INF)
            m = jnp.max(s, axis=-1, keepdims=True)
            p = jnp.exp2(s - m)
            m_sc[h] = m
            l_sc[h] = jnp.sum(p, axis=-1, keepdims=True)
            acc_sc[h] = jnp.dot(p.astype(BF16), vn, preferred_element_type=F32)

    def head_rows(refs, h):
        return jnp.concatenate([r_[pl.ds(h, page, stride=heads), :] for r_ in refs],
                               axis=0).astype(BF16)

    scores = [lax.dot_general(q2[h], head_rows(k_refs, h), _NT, preferred_element_type=F32)
              for h in range(heads)]
    probs = []
    for h in range(heads):
        m_new = jnp.maximum(m_sc[h], jnp.max(scores[h], axis=-1, keepdims=True))
        p = jnp.exp2(scores[h] - m_new)
        alpha = jnp.exp2(m_sc[h] - m_new)
        m_sc[h] = m_new
        l_sc[h] = alpha * l_sc[h] + jnp.sum(p, axis=-1, keepdims=True)
        probs.append((p.astype(BF16), alpha))
    for h in range(heads):
        p, alpha = probs[h]
        acc_sc[h] = alpha * acc_sc[h] + jnp.dot(p, head_rows(v_refs, h),
                                                preferred_element_type=F32)

    @pl.when(g == pl.num_programs(1) - 1)
    def _finish():
        lam = _lambda(lq1_ref, lk1_ref, lq2_ref, lk2_ref, lam_init)
        sw = sw_ref[...]
        for h in range(heads):
            on = acc_sc[h] * (1.0 / l_sc[h])
            o = on[:t_new] - lam * on[t_new:]
            o_ref[:, h * DA_HEAD_W:(h + 1) * DA_HEAD_W] = _sub_ln(o, sw, lam_init)


def _attn_decode(q3, kn3, vn3, cache_k3, cache_v3, page_table, lq1, lk1, lq2, lk2, sw,
                 heads, lam_init):
    nb, t_new, da_w = q3.shape
    n_pages = page_table.shape[1]
    page = cache_k3.shape[1] // heads
    pps = math.gcd(n_pages, 16)
    n_steps = n_pages // pps
    tok = pl.BlockSpec((None, t_new, da_w), lambda b, g, pt: (b, 0, 0))
    small = lambda a: pl.BlockSpec(a.shape, lambda b, g, pt: (0, 0))
    hbm = pl.BlockSpec(memory_space=pl.ANY)
    page_buf = pltpu.VMEM((_PAGE_SLOTS, pps, page * heads, DA_HEAD_W), cache_k3.dtype)

    grid_spec = pltpu.PrefetchScalarGridSpec(
        num_scalar_prefetch=1,
        grid=(nb, n_steps),
        in_specs=[tok, tok, tok, small(lq1), small(lk1), small(lq2), small(lk2), small(sw), hbm, hbm],
        out_specs=tok,
        scratch_shapes=[page_buf, page_buf,
                        pltpu.SemaphoreType.DMA((2, _PAGE_SLOTS)),
                        pltpu.VMEM((heads, 2 * t_new, 1), F32),
                        pltpu.VMEM((heads, 2 * t_new, 1), F32),
                        pltpu.VMEM((heads, 2 * t_new, DA_HEAD_W), F32)],
    )
    return pl.pallas_call(
        functools.partial(_attn_decode_kernel, pages_per_step=pps, heads=heads, page=page,
                          lam_init=lam_init),
        grid_spec=grid_spec,
        out_shape=jax.ShapeDtypeStruct((nb, t_new, da_w), F32),
        compiler_params=_params("arbitrary", "arbitrary"),
    )(page_table, q3, kn3, vn3, lq1, lk1, lq2, lk2, sw, cache_k3, cache_v3)


def _hgrn_kernel(q_ref, f_ref, v_ref, g_ref, s0_ref, nw_ref, o_ref, s_out_ref, st_sc, b_sc,
                 *, heads, n_chunks):
    c_len = HG_CHUNK
    step = pl.program_id(1)

    @pl.when(step == 0)
    def _load_state():
        for h in range(heads):
            st_sc[h] = s0_ref[h].T

    r = lax.broadcasted_iota(jnp.int32, (c_len, c_len), 0)
    c = lax.broadcasted_iota(jnp.int32, (c_len, c_len), 1)
    tri = jnp.where(c <= r, 1.0, 0.0).astype(BF16)

    def midpoint_mask(lv):
        return ((r // lv) == (c // lv)) & ((r % lv) >= lv // 2) & ((c % lv) < lv // 2)

    level_masks = {lv: midpoint_mask(lv) for lv in (16, 32, 64, 128)}
    blk = _HG_FAST_BLOCK
    diag_mask = ((r // blk) == (c // blk)) & (c <= r)
    sub = lax.broadcasted_iota(jnp.int32, (c_len // 8, 8, 1), 1)
    nw = nw_ref[...]

    def cumsum_rows(x):
        x1 = x.astype(BF16)
        r1 = x - x1.astype(F32)
        x2 = r1.astype(BF16)
        x3 = (r1 - x2.astype(F32)).astype(BF16)
        return (jnp.dot(tri, x1, preferred_element_type=F32)
                + jnp.dot(tri, x2, preferred_element_type=F32)
                + jnp.dot(tri, x3, preferred_element_type=F32))

    def midpoint_term(q, kk, b, lv):
        bl = b.reshape(c_len // lv, lv, HG_DK)
        mid = bl[:, lv // 2 - 1:lv // 2, :]
        e = jnp.exp2(-jnp.abs(bl - mid)).reshape(c_len, HG_DK)
        a_lv = lax.dot_general((q * e).astype(BF16), (kk * e).astype(BF16), _NT,
                               preferred_element_type=F32)
        return jnp.where(level_masks[lv], a_lv, 0.0)

    def intra_fast(data):
        mats = []
        for q, kk, _, b, f in data:
            bl = b.reshape(c_len // blk, blk, HG_DK)
            lf0 = jnp.log2(f.reshape(c_len // blk, blk, HG_DK)[:, 0:1, :])
            d = (bl - (bl[:, 0:1, :] - lf0)).reshape(c_len, HG_DK)
            a_blk = lax.dot_general((q * jnp.exp2(d)).astype(BF16),
                                    (kk * jnp.exp2(-d)).astype(BF16), _NT,
                                    preferred_element_type=F32)
            mats.append(jnp.where(diag_mask, a_blk, 0.0))
        for lv in (32, 64, 128):
            for h, (q, kk, _, b, _) in enumerate(data):
                mats[h] = mats[h] + midpoint_term(q, kk, b, lv)
        return [(a_mat, None) for a_mat in mats]

    def intra_exact(data):
        out = []
        for q, kk, v, b, _ in data:
            q3 = q.reshape(c_len // 8, 8, HG_DK)
            b3 = b.reshape(c_len // 8, 8, HG_DK)
            k3 = kk.reshape(c_len // 8, 8, HG_DK)
            v3 = v.reshape(c_len // 8, 8, HG_DV)
            o3 = jnp.zeros((c_len // 8, 8, HG_DV), F32)
            for s in range(8):
                dec = jnp.exp2(jnp.minimum(b3 - b3[:, s:s + 1, :], 0.0))
                a = jnp.sum(q3 * k3[:, s:s + 1, :] * dec, axis=-1, keepdims=True)
                o3 = o3 + jnp.where(sub >= s, a, 0.0) * v3[:, s:s + 1, :]
            a_mat = sum(midpoint_term(q, kk, b, lv) for lv in (16, 32, 64, 128))
            out.append((a_mat, o3.reshape(c_len, HG_DV)))
        return out

    def head_cols(h):
        return slice(h * HG_DK, (h + 1) * HG_DK)

    def prefix(ci, growth):
        r0 = pl.multiple_of(ci * c_len, c_len)
        for h in range(heads):
            lf = jnp.log2(f_ref[pl.ds(r0, c_len), head_cols(h)])
            b = cumsum_rows(lf)
            b_sc[pl.ds(r0, c_len), head_cols(h)] = b
            bl = b.reshape(c_len // blk, blk, HG_DK)
            before = bl[:, 0:1, :] - lf.reshape(c_len // blk, blk, HG_DK)[:, 0:1, :]
            growth = jnp.maximum(growth, before - bl[:, blk - 1:blk, :])
        return growth

    growth = jnp.zeros((c_len // blk, 1, HG_DK), F32)
    for ci in range(n_chunks):
        growth = prefix(ci, growth)
    safe = jnp.max(growth) < _HG_FAST_LIMIT

    def run_chunks(intra):
        for ci in range(n_chunks):
            rows = pl.ds(ci * c_len, c_len)
            data = []
            for h in range(heads):
                hs = head_cols(h)
                f = f_ref[rows, hs]
                data.append((q_ref[rows, hs], 1.0 - f, v_ref[rows, hs], b_sc[rows, hs], f))
            mats = intra(data)
            outs = []
            for h, (q, kk, v, b, _) in enumerate(data):
                a_mat, o_blk = mats[h]
                st = st_sc[h]
                o = (jnp.dot(a_mat.astype(BF16), v.astype(BF16), preferred_element_type=F32)
                     + lax.dot_general((q * jnp.exp2(b)).astype(BF16), st.astype(BF16), _NT,
                                       preferred_element_type=F32))
                outs.append(o if o_blk is None else o + o_blk)
                b_end = b[c_len - 1:c_len, :]
                k_end = kk * jnp.exp2(b_end - b)
                st_sc[h] = st * jnp.exp2(b_end) + jnp.dot(
                    v.T.astype(BF16), k_end.astype(BF16), preferred_element_type=F32)
            for h, o in enumerate(outs):
                y = o * lax.rsqrt(jnp.mean(o * o, axis=-1, keepdims=True) + EPS) * nw
                o_ref[rows, head_cols(h)] = y * g_ref[rows, head_cols(h)]

    pl.when(safe)(lambda: run_chunks(intra_fast))
    pl.when(jnp.logical_not(safe))(lambda: run_chunks(intra_exact))

    @pl.when(step == pl.num_programs(1) - 1)
    def _store_state():
        for h in range(heads):
            s_out_ref[h] = st_sc[h].T


def _hgrn(hq, f, hi, hg, s0, nw, batch, seq):
    n, hg_w = hq.shape
    heads = hg_w // HG_DK
    tb = HG_CHUNK * math.gcd(seq // HG_CHUNK, 4)
    steps = seq // tb
    row = pl.BlockSpec((tb, hg_w), lambda b, t: (b * steps + t, 0))
    st = pl.BlockSpec((None, heads, HG_DK, HG_DV), lambda b, t: (b, 0, 0, 0))
    return pl.pallas_call(
        functools.partial(_hgrn_kernel, heads=heads, n_chunks=tb // HG_CHUNK),
        grid=(batch, steps),
        in_specs=[row, row, row, row, st, pl.BlockSpec(nw.shape, lambda b, t: (0, 0))],
        out_specs=[row, st],
        out_shape=[jax.ShapeDtypeStruct((n, hg_w), F32),
                   jax.ShapeDtypeStruct(s0.shape, F32)],
        scratch_shapes=[pltpu.VMEM((heads, HG_DV, HG_DK), F32), pltpu.VMEM((tb, hg_w), F32)],
        compiler_params=_params("parallel", "arbitrary"),
    )(hq, f, hi, hg, s0, nw)


_HG_SHORT = 8


def _hgrn_short_kernel(q_ref, f_ref, v_ref, g_ref, s0_ref, nw_ref, o_ref, s_out_ref, *, heads):
    t = _HG_SHORT
    row = lax.broadcasted_iota(jnp.int32, (t, 1), 0)
    pad8 = jnp.zeros((t, HG_DK), F32)
    pad = jnp.zeros((LANES - t, HG_DK), F32)
    nw = nw_ref[...]
    for h in range(heads):
        hs = slice(h * HG_DK, (h + 1) * HG_DK)
        q = q_ref[:, hs]
        f = f_ref[:, hs]
        v = v_ref[:, hs]
        kk = 1.0 - f
        lf = jnp.log2(f)
        b = jnp.zeros_like(lf)
        for s in range(t):
            b = b + jnp.where(row >= s, lf[s:s + 1, :], 0.0)
        o = jnp.zeros((t, HG_DV), F32)
        for s in range(t):
            dec = jnp.exp2(jnp.minimum(b - b[s:s + 1, :], 0.0))
            a = jnp.sum(q * kk[s:s + 1, :] * dec, axis=-1, keepdims=True)
            o = o + jnp.where(row >= s, a, 0.0) * v[s:s + 1, :]
        st = s0_ref[h].T
        q_dec = jnp.concatenate([q * jnp.exp2(b), pad8], axis=0).astype(BF16)
        o = o + lax.dot_general(q_dec, st.astype(BF16), _NT, preferred_element_type=F32)[:t]
        b_end = b[t - 1:t, :]
        k_end = jnp.concatenate([kk * jnp.exp2(b_end - b), pad], axis=0)
        v_pad = jnp.concatenate([v, pad], axis=0)
        st_new = st * jnp.exp2(b_end) + jnp.dot(v_pad.T.astype(BF16), k_end.astype(BF16),
                                                preferred_element_type=F32)
        s_out_ref[h] = st_new.T
        y = o * lax.rsqrt(jnp.mean(o * o, axis=-1, keepdims=True) + EPS) * nw
        o_ref[:, hs] = y * g_ref[:, hs]


def _hgrn_short(hq, f, hi, hg, s0, nw, batch):
    n, hg_w = hq.shape
    heads = hg_w // HG_DK
    r3 = lambda a: a.reshape(batch, _HG_SHORT, hg_w)
    tok = pl.BlockSpec((None, _HG_SHORT, hg_w), lambda b: (b, 0, 0))
    st = pl.BlockSpec((None, heads, HG_DK, HG_DV), lambda b: (b, 0, 0, 0))
    o, s_new = pl.pallas_call(
        functools.partial(_hgrn_short_kernel, heads=heads),
        grid=(batch,),
        in_specs=[tok, tok, tok, tok, st, pl.BlockSpec(nw.shape, lambda b: (0, 0))],
        out_specs=[tok, st],
        out_shape=[jax.ShapeDtypeStruct((batch, _HG_SHORT, hg_w), F32),
                   jax.ShapeDtypeStruct(s0.shape, F32)],
        compiler_params=_params("parallel"),
    )(r3(hq), r3(f), r3(hi), r3(hg), s0, nw)
    return o.reshape(n, hg_w), s_new


def _outproj_kernel(oda_ref, ohg_ref, x_ref, wo_ref, n2_ref, wrh_ref, wrl_ref, br_ref,
                    h_ref, xn_ref, gates_ref, *, da_w):
    n_parts = 2 if oda_ref.shape[0] % 16 == 0 else 1
    part = oda_ref.shape[0] // n_parts

    def project(i):
        rows = pl.ds(i * part, part)
        o = (jnp.dot(oda_ref[rows, :].astype(BF16), wo_ref[:da_w, :], preferred_element_type=F32)
             + jnp.dot(ohg_ref[rows, :].astype(BF16), wo_ref[da_w:, :], preferred_element_type=F32))
        h = x_ref[rows, :] + o
        h_ref[rows, :] = h
        return h

    def route(i, h):
        rows = pl.ds(i * part, part)
        xn = h * lax.rsqrt(jnp.mean(h * h, axis=-1, keepdims=True) + EPS) * n2_ref[...]
        xh = xn.astype(BF16)
        xn_ref[rows, :] = xh
        xl = (xn - xh.astype(F32)).astype(BF16)
        wrh = wrh_ref[...]
        both = jnp.dot(xh, jnp.concatenate([wrh, wrl_ref[...]], axis=-1),
                       preferred_element_type=F32)
        logits = (both[:, :LANES] + jnp.dot(xl, wrh, preferred_element_type=F32)
                  + both[:, LANES:]) + br_ref[...]
        lane = lax.broadcasted_iota(jnp.int32, logits.shape, 1)
        lane_f = lane.astype(F32)
        big = float(LANES)
        is_g = (lane >= N_EXPERTS) & (lane < N_EXPERTS + N_GROUPS)
        gl = jnp.where(is_g, logits, NEG_INF)
        g_max = jnp.max(gl, axis=-1, keepdims=True)
        g_idx = jnp.min(jnp.where(gl == g_max, lane_f, big), axis=-1, keepdims=True) - N_EXPERTS
        g_w = 1.0 / jnp.sum(jnp.exp(gl - g_max), axis=-1, keepdims=True)
        grp_of_lane = (lane // EXPERTS_PER_GROUP).astype(F32)
        sel = (lane < N_EXPERTS) & (grp_of_lane == g_idx)
        el = jnp.where(sel, logits, NEG_INF)
        v1 = jnp.max(el, axis=-1, keepdims=True)
        i1 = jnp.min(jnp.where(sel, jnp.where(el == v1, lane_f, big), big), axis=-1, keepdims=True)
        el2 = jnp.where(lane_f == i1, NEG_INF, el)
        v2 = jnp.max(el2, axis=-1, keepdims=True)
        sel2 = sel & (lane_f != i1)
        i2 = jnp.min(jnp.where(sel2, jnp.where(el2 == v2, lane_f, big), big), axis=-1,
                     keepdims=True)
        t = jnp.exp(v2 - v1)
        p1 = 1.0 / (1.0 + t)
        p2 = t * p1
        gates_ref[rows, :] = (jnp.where(lane_f == i1, p1 * g_w, 0.0)
                              + jnp.where(lane_f == i2, p2 * g_w, 0.0))

    hs = [project(i) for i in range(n_parts)]
    for i in range(n_parts):
        route(i, hs[i])


def _outproj(oda, ohg, x2d, wo_bf, n2, wr_hi, wr_lo, br):
    n, d = x2d.shape
    da_w = oda.shape[1]
    tm = _row_tile(n, 512)
    row = lambda w: pl.BlockSpec((tm, w), lambda i: (i, 0))
    full = lambda a: pl.BlockSpec(a.shape, lambda i: (0, 0))
    return pl.pallas_call(
        functools.partial(_outproj_kernel, da_w=da_w),
        grid=(n // tm,),
        in_specs=[row(da_w), row(ohg.shape[1]), row(d), full(wo_bf), full(n2), full(wr_hi),
                  full(wr_lo), full(br)],
        out_specs=[row(d), row(d), row(LANES)],
        out_shape=[jax.ShapeDtypeStruct((n, d), F32), jax.ShapeDtypeStruct((n, d), BF16),
                   jax.ShapeDtypeStruct((n, LANES), F32)],
        compiler_params=_params("parallel"),
    )(oda, ohg, x2d, wo_bf, n2, wr_hi, wr_lo, br)


_MOE_EXPERTS_PER_STEP = 4


def _moe_kernel(x_ref, h_ref, gates_ref, wg_ref, wu_ref, wd_ref, y_ref):
    eb = pl.program_id(1)
    per_step = wg_ref.shape[0]

    @pl.when(eb == 0)
    def _init():
        y_ref[...] = h_ref[...]

    x = x_ref[...]
    gates = gates_ref[...]
    lane = lax.broadcasted_iota(jnp.int32, gates.shape, 1)
    hid = []
    for j in range(per_step):
        a = jnp.dot(x, wg_ref[j], preferred_element_type=F32)
        u = jnp.dot(x, wu_ref[j], preferred_element_type=F32)
        gate = jnp.sum(jnp.where(lane == eb * per_step + j, gates, 0.0), axis=-1, keepdims=True)
        hid.append(((a * _sigmoid(a)) * u * gate).astype(BF16))
    y_ref[...] += jnp.dot(jnp.concatenate(hid, axis=-1), wd_ref[...], preferred_element_type=F32)


def _moe(xn_bf, h, gates, wg_bf, wu_bf, wd_bf):
    n, d = h.shape
    n_exp, _, ff = wg_bf.shape
    per_step = math.gcd(n_exp, _MOE_EXPERTS_PER_STEP)
    tm = _row_tile(n, 1024)
    row = lambda w: pl.BlockSpec((tm, w), lambda i, e: (i, 0))
    return pl.pallas_call(
        _moe_kernel,
        grid=(n // tm, n_exp // per_step),
        in_specs=[row(d), row(d), row(LANES),
                  pl.BlockSpec((per_step, d, ff), lambda i, e: (e, 0, 0)),
                  pl.BlockSpec((per_step, d, ff), lambda i, e: (e, 0, 0)),
                  pl.BlockSpec((per_step * ff, d), lambda i, e: (e, 0))],
        out_specs=row(d),
        out_shape=jax.ShapeDtypeStruct((n, d), F32),
        compiler_params=_params("parallel", "arbitrary"),
    )(xn_bf, h, gates, wg_bf, wu_bf, wd_bf.reshape(n_exp * ff, d))


def _pad_tokens(a, batch, seq, seq_pad, value):
    a3 = a.reshape(batch, seq, a.shape[-1])
    a3 = jnp.pad(a3, ((0, 0), (0, seq_pad - seq), (0, 0)), constant_values=value)
    return a3.reshape(batch * seq_pad, a.shape[-1])


def _layer(x, s0, past, layer_idx, w):
    batch, seq, d = x.shape
    n = batch * seq
    x2d = x.reshape(n, d)
    da_w = w["qw"].shape[1]
    heads = da_w // DA_HEAD_W
    lam_init = 0.8 - 0.6 * math.exp(-0.3 * layer_idx)
    lam_args = (w["lq1"], w["lk1"], w["lq2"], w["lk2"], w["sw"])

    q, k, v, hq, f, hi, hg, *attn_in = _inproj(x2d, w["n1"], w["w_in"], w["qw"], w["kw"], w["lb"],
                                               w["gm"], past is None)
    if past is None:
        oda = _attn_prompt(q, *attn_in, *lam_args, batch, seq, lam_init)
    else:
        cache_k3, cache_v3, page_table = past
        r3 = lambda a: a.reshape(batch, seq, da_w)
        oda = _attn_decode(r3(q), r3(k), r3(v), cache_k3, cache_v3, page_table, *lam_args,
                           heads, lam_init).reshape(n, da_w)

    seq_pad = -(-seq // HG_CHUNK) * HG_CHUNK
    if seq == _HG_SHORT:
        ohg, s_new = _hgrn_short(hq, f, hi, hg, s0, w["hnw"], batch)
    elif seq_pad != seq:
        hq_p, hi_p, hg_p = (_pad_tokens(a, batch, seq, seq_pad, 0.0) for a in (hq, hi, hg))
        f_p = _pad_tokens(f, batch, seq, seq_pad, 1.0)
        ohg, s_new = _hgrn(hq_p, f_p, hi_p, hg_p, s0, w["hnw"], batch, seq_pad)
        ohg = ohg.reshape(batch, seq_pad, -1)[:, :seq].reshape(n, -1)
    else:
        ohg, s_new = _hgrn(hq, f, hi, hg, s0, w["hnw"], batch, seq)

    h, xn_bf, gates = _outproj(oda, ohg, x2d, w["w_out"], w["n2"], w["wr_hi"], w["wr_lo"], w["br"])
    y = _moe(xn_bf, h, gates, w["wg"], w["wu"], w["wd"])
    kv_shape = (batch, seq, heads, DA_HEAD_W)
    return y.reshape(batch, seq, d), k.reshape(kv_shape), v.reshape(kv_shape), s_new


def _layer_weights(l, lb_all, norm1_w, w_in, q_norm_w, k_norm_w, lambda_q1, lambda_k1, lambda_q2,
                   lambda_k2, subln_w, hg_norm_w, w_out, norm2_w, w_router_grp, b_router_grp,
                   w_router_exp, b_router_exp, w_gate, w_up, w_down):
    d = w_in.shape[1]
    da_w = w_out.shape[1] // 2
    n_grp_norm = da_w // DA_HEAD_DIM
    row = lambda a: a.reshape(1, -1).astype(F32)
    g_id = jnp.arange(math.gcd(da_w, 2 * LANES)) // DA_HEAD_DIM
    gm = jnp.where(g_id[:, None] == g_id[None, :], 1.0 / DA_HEAD_DIM, 0.0).astype(BF16)
    wr = jnp.zeros((d, LANES), F32)
    wr = wr.at[:, :N_EXPERTS].set(w_router_exp[l]).at[:, N_EXPERTS:N_EXPERTS + N_GROUPS].set(
        w_router_grp[l])
    wr_hi = wr.astype(BF16)
    wr_lo = (wr - wr_hi.astype(F32)).astype(BF16)
    br = jnp.zeros((1, LANES), F32)
    br = br.at[0, :N_EXPERTS].set(b_router_exp[l]).at[0, N_EXPERTS:N_EXPERTS + N_GROUPS].set(
        b_router_grp[l])
    return dict(
        n1=row(norm1_w[l]), w_in=w_in[l].astype(BF16),
        qw=row(jnp.tile(q_norm_w[l], n_grp_norm)), kw=row(jnp.tile(k_norm_w[l], n_grp_norm)),
        lb=row(lb_all[l]), gm=gm,
        lq1=row(lambda_q1[l]), lk1=row(lambda_k1[l]), lq2=row(lambda_q2[l]), lk2=row(lambda_k2[l]),
        sw=row(subln_w[l]), hnw=row(hg_norm_w[l]),
        w_out=w_out[l].astype(BF16), n2=row(norm2_w[l]), wr_hi=wr_hi, wr_lo=wr_lo, br=br,
        wg=w_gate[l].astype(BF16), wu=w_up[l].astype(BF16), wd=w_down[l].astype(BF16),
    )


def kernel(x_prompt, x_sample, cache_k, cache_v, state_hgrn, page_table, norm1_w, w_in, q_norm_w, k_norm_w, lambda_q1, lambda_k1, lambda_q2, lambda_k2, subln_w, hg_lb, hg_norm_w, w_out, norm2_w, w_router_grp, b_router_grp, w_router_exp, b_router_exp, w_gate, w_up, w_down):
    depth = norm1_w.shape[0]
    n_phys, page, heads, head_w = cache_k.shape[1:]
    lb_all = jnp.cumsum(jax.nn.softmax(hg_lb.astype(F32), axis=0), axis=0)
    yp, ys = x_prompt, x_sample
    outs = [[] for _ in range(6)]
    for l in range(depth):
        w = _layer_weights(l, lb_all, norm1_w, w_in, q_norm_w, k_norm_w, lambda_q1, lambda_k1,
                           lambda_q2, lambda_k2, subln_w, hg_norm_w, w_out, norm2_w,
                           w_router_grp, b_router_grp, w_router_exp, b_router_exp,
                           w_gate, w_up, w_down)
        s0p = jnp.zeros((x_prompt.shape[0],) + state_hgrn.shape[2:], F32)
        yp, k_r, v_r, s_r = _layer(yp, s0p, None, l, w)
        past = (cache_k[l].reshape(n_phys, page * heads, head_w),
                cache_v[l].reshape(n_phys, page * heads, head_w), page_table)
        ys, k_s, v_s, s_s = _layer(ys, state_hgrn[l], past, l, w)
        for lst, val in zip(outs, (k_r, v_r, s_r, k_s, v_s, s_s)):
            lst.append(val)
    kp, vp, sp, kss, vss, sss = (jnp.stack(o) for o in outs)
    return (yp, ys, kp, vp, sp, kss, vss, sss)
```
